```python
import math
import jax, jax.numpy as jnp
from jax import lax
import numpy as np

D_MODEL = 1024
BATCH = 8
SEQ = 2048
DEPTH = 2

PLE_DIM = 256
HEAD_DIM = 64
N_HEADS_A = 8
N_HEADS_B = 8
N_KV_B = 2
DILATED_PATTERNS = ((128, 1), (512, 4), (2048, 16))
WINDOW_B = 128
NUM_BUCKETS = 32
MAX_DISTANCE = 1024
D_FF = ((8 * D_MODEL + 3 * 256 - 1) // (3 * 256)) * 256
RMS_EPS = 1e-6
NEG_INF = -1e30
WIDTH_A = N_HEADS_A * HEAD_DIM
WIDTH_BQ = N_HEADS_B * HEAD_DIM
WIDTH_BKV = N_KV_B * HEAD_DIM
IN_SPLITS = (WIDTH_A, WIDTH_A, WIDTH_A, WIDTH_BQ, WIDTH_BKV, WIDTH_BKV, D_MODEL, D_MODEL)
D_IN = sum(IN_SPLITS)

kernel_name = "hybrid_dilated_window_gqa_encoder"


def rmsnorm(x, g):
    xf = x.astype(jnp.float32)
    y = xf * lax.rsqrt(jnp.mean(xf * xf, axis=-1, keepdims=True) + RMS_EPS)
    return (y * g.astype(jnp.float32)).astype(x.dtype)


def t5_bucket(rel):
    half_b = NUM_BUCKETS // 2
    max_exact = half_b // 2
    sign = jnp.where(rel > 0, half_b, 0)
    n = jnp.abs(rel)
    nf = jnp.maximum(n, 1).astype(jnp.float32)
    large = max_exact + (jnp.log(nf / max_exact) / math.log(MAX_DISTANCE / max_exact)
                         * (half_b - max_exact)).astype(jnp.int32)
    large = jnp.minimum(large, half_b - 1)
    return sign + jnp.where(n < max_exact, n, large)


def band_rel(blk):
    i = jnp.arange(blk, dtype=jnp.int32)[:, None]
    j = jnp.arange(3 * blk, dtype=jnp.int32)[None, :]
    return j - blk - i


def rel_bias(table, blk, dilation):
    buckets = t5_bucket(band_rel(blk) * dilation)
    return jnp.transpose(table[buckets], (2, 0, 1))


def banded_attention(q, k, v, bias, half, blk, sink=None):
    N, L, Hq, hd = q.shape
    Hk = k.shape[2]
    G = Hq // Hk
    nb = -(-L // blk)
    Lp = nb * blk
    qp = jnp.pad(q, ((0, 0), (0, Lp - L), (0, 0), (0, 0)))
    kp = jnp.pad(k, ((0, 0), (blk, Lp - L + blk), (0, 0), (0, 0))).reshape(N, nb + 2, blk, Hk, hd)
    vp = jnp.pad(v, ((0, 0), (blk, Lp - L + blk), (0, 0), (0, 0))).reshape(N, nb + 2, blk, Hk, hd)
    kw = jnp.concatenate([kp[:, :-2], kp[:, 1:-1], kp[:, 2:]], axis=2)
    vw = jnp.concatenate([vp[:, :-2], vp[:, 1:-1], vp[:, 2:]], axis=2)
    qb = qp.reshape(N, nb, blk, Hk, G, hd)
    s = jnp.einsum('nbqkgd,nbjkd->nbkgqj', qb, kw).astype(jnp.float32) * (hd ** -0.5)
    s = s + bias.reshape(Hk, G, blk, 3 * blk).astype(jnp.float32)[None, None]
    rel = band_rel(blk)
    key_pos = (jnp.arange(nb, dtype=jnp.int32)[:, None, None] * blk
               + (jnp.arange(3 * blk, dtype=jnp.int32) - blk)[None, None, :])
    mask = (jnp.abs(rel) <= half)[None] & (key_pos >= 0) & (key_pos < L)
    s = jnp.where(mask[None, :, None, None], s, NEG_INF)
    m = jnp.max(s, axis=-1)
    if sink is not None:
        sk = sink.astype(jnp.float32).reshape(Hk, G)[None, None, :, :, None]
        m = jnp.maximum(m, sk)
    pr = jnp.exp(s - m[..., None])
    den = jnp.sum(pr, axis=-1)
    if sink is not None:
        den = den + jnp.exp(sk - m)
    pr = (pr / den[..., None]).astype(v.dtype)
    o = jnp.einsum('nbkgqj,nbjkd->nbqkgd', pr, vw).reshape(N, Lp, Hq, hd)[:, :L]
    lse = jnp.transpose(m + jnp.log(den), (0, 1, 4, 2, 3)).reshape(N, Lp, Hq)[:, :L]
    return o, lse


def dilated_mixer(q, k, v, biases):
    B_, S_, H, hd = q.shape
    outs, lses = [], []
    for (w, d), bias in zip(DILATED_PATTERNS, biases):
        half = w // (2 * d)

        def to_sub(t):
            return t.reshape(B_, S_ // d, d, H, hd).transpose(0, 2, 1, 3, 4).reshape(B_ * d, S_ // d, H, hd)

        o, lse = banded_attention(to_sub(q), to_sub(k), to_sub(v), bias, half, half)
        outs.append(o.reshape(B_, d, S_ // d, H, hd).transpose(0, 2, 1, 3, 4).reshape(B_, S_, H, hd))
        lses.append(lse.reshape(B_, d, S_ // d, H).transpose(0, 2, 1, 3).reshape(B_, S_, H))
    wts = jax.nn.softmax(jnp.stack(lses, axis=0), axis=0)
    return jnp.einsum('pbsh,pbshd->bshd', wts.astype(q.dtype), jnp.stack(outs, axis=0))


def setup_inputs(seed: int = 0) -> dict:
    key = jax.random.key(seed)
    ks = jax.random.split(key, 24)
    f32 = jnp.float32

    def nrm(k_, shape, scale):
        return jax.random.normal(k_, shape, f32) * scale

    def gain(k_, shape):
        return 1.0 + 0.05 * jax.random.normal(k_, shape, f32)

    return {
        "x": nrm(ks[0], (BATCH, SEQ, D_MODEL), 1.0),
        "p": nrm(ks[1], (DEPTH, BATCH, SEQ, PLE_DIM), 1.0),
        "rel_table": nrm(ks[2], (NUM_BUCKETS, N_HEADS_A + N_HEADS_B), 0.1),
        "norm_mix_g": gain(ks[3], (DEPTH, D_MODEL)),
        "w_in": nrm(ks[4], (DEPTH, D_MODEL, D_IN), D_MODEL ** -0.5),
        "qnorm_a_g": gain(ks[5], (DEPTH, HEAD_DIM)),
        "knorm_a_g": gain(ks[6], (DEPTH, HEAD_DIM)),
        "qnorm_b_g": gain(ks[7], (DEPTH, HEAD_DIM)),
        "knorm_b_g": gain(ks[8], (DEPTH, HEAD_DIM)),
        "sink_b": nrm(ks[9], (DEPTH, N_HEADS_B), 0.5),
        "w_branch_a": nrm(ks[10], (DEPTH, WIDTH_A, D_MODEL), WIDTH_A ** -0.5),
        "w_branch_b": nrm(ks[11], (DEPTH, WIDTH_BQ, D_MODEL), WIDTH_BQ ** -0.5),
        "w_out": nrm(ks[12], (DEPTH, D_MODEL, D_MODEL), D_MODEL ** -0.5),
        "norm_ffn_g": gain(ks[13], (DEPTH, D_MODEL)),
        "w_ffn_gate": nrm(ks[14], (DEPTH, D_MODEL, D_FF), D_MODEL ** -0.5),
        "w_ffn_up": nrm(ks[15], (DEPTH, D_MODEL, D_FF), D_MODEL ** -0.5),
        "w_ffn_down": nrm(ks[16], (DEPTH, D_FF, D_MODEL), D_FF ** -0.5),
        "norm_ple_g": gain(ks[17], (DEPTH, D_MODEL)),
        "w_ple_gate": nrm(ks[18], (DEPTH, D_MODEL, D_MODEL), D_MODEL ** -0.5),
        "w_ple_proj": nrm(ks[19], (DEPTH, PLE_DIM, D_MODEL), PLE_DIM ** -0.5),
    }


def reference(x, p, rel_table, norm_mix_g, w_in, qnorm_a_g, knorm_a_g, qnorm_b_g, knorm_b_g,
              sink_b, w_branch_a, w_branch_b, w_out, norm_ffn_g, w_ffn_gate, w_ffn_up, w_ffn_down,
              norm_ple_g, w_ple_gate, w_ple_proj):
    B_, S_, _ = x.shape
    table_a = rel_table[:, :N_HEADS_A]
    table_b = rel_table[:, N_HEADS_A:]
    biases_a = [rel_bias(table_a, w // (2 * d), d) for (w, d) in DILATED_PATTERNS]
    bias_b = rel_bias(table_b, WINDOW_B, 1)
    split_points = [int(v) for v in np.cumsum(IN_SPLITS)[:-1]]

    for l in range(DEPTH):
        h = rmsnorm(x, norm_mix_g[l])
        proj = jnp.einsum('bsd,de->bse', h, w_in[l])
        qa, ka, va, qb, kb, vb, ga, gb = jnp.split(proj, split_points, axis=-1)
        qa = rmsnorm(qa.reshape(B_, S_, N_HEADS_A, HEAD_DIM), qnorm_a_g[l])
        ka = rmsnorm(ka.reshape(B_, S_, N_HEADS_A, HEAD_DIM), knorm_a_g[l])
        va = va.reshape(B_, S_, N_HEADS_A, HEAD_DIM)
        qb = rmsnorm(qb.reshape(B_, S_, N_HEADS_B, HEAD_DIM), qnorm_b_g[l])
        kb = rmsnorm(kb.reshape(B_, S_, N_KV_B, HEAD_DIM), knorm_b_g[l])
        vb = vb.reshape(B_, S_, N_KV_B, HEAD_DIM)

        ya = dilated_mixer(qa, ka, va, biases_a).reshape(B_, S_, WIDTH_A)
        yb, _ = banded_attention(qb, kb, vb, bias_b, WINDOW_B, WINDOW_B, sink=sink_b[l])
        yb = yb.reshape(B_, S_, WIDTH_BQ)

        merged = (jax.nn.sigmoid(ga) * jnp.einsum('bsc,cd->bsd', ya, w_branch_a[l])
                  + jax.nn.sigmoid(gb) * jnp.einsum('bsc,cd->bsd', yb, w_branch_b[l]))
        x = x + jnp.einsum('bsd,de->bse', merged, w_out[l])

        h = rmsnorm(x, norm_ffn_g[l])
        hid = jax.nn.silu(jnp.einsum('bsd,df->bsf', h, w_ffn_gate[l])) * jnp.einsum('bsd,df->bsf', h, w_ffn_up[l])
        x = x + jnp.einsum('bsf,fd->bsd', hid, w_ffn_down[l])

        h = rmsnorm(x, norm_ple_g[l])
        x = x + jax.nn.sigmoid(jnp.einsum('bsd,de->bse', h, w_ple_gate[l])) * jnp.einsum('bsq,qd->bsd', p[l], w_ple_proj[l])
    return x
```

```python
import functools
import math

import jax
import jax.numpy as jnp
from jax import lax
from jax.experimental import pallas as pl
from jax.experimental.pallas import tpu as pltpu

F32 = jnp.float32
BF16 = jnp.bfloat16

HEAD_DIM = 64
N_HEADS_A = 8
N_HEADS_B = 8
N_KV_B = 2
DILATED_PATTERNS = ((128, 1), (512, 4), (2048, 16))
WINDOW_B = 128
NUM_BUCKETS = 32
MAX_DISTANCE = 1024
RMS_EPS = 1e-6
NEG_INF = -1e30

LANES = 128
MXU_COLS = 256
BQ = 128
TM = 512
VMEM_LIMIT = 56 * 1024 * 1024


def _dot(a, b):
    return jnp.dot(a, b, preferred_element_type=F32)


def _rms(x, g):
    ms = jnp.mean(x * x, axis=-1, keepdims=True)
    return x * lax.rsqrt(ms + RMS_EPS) * g


def _low_lanes():
    return lax.broadcasted_iota(jnp.int32, (1, LANES), 1) < HEAD_DIM


def _head_rms(y, gain, lo):
    sq = y * y
    s0 = jnp.sum(jnp.where(lo, sq, 0.0), axis=-1, keepdims=True)
    s1 = jnp.sum(jnp.where(lo, 0.0, sq), axis=-1, keepdims=True)
    ms = jnp.where(lo, s0, s1) * (1.0 / HEAD_DIM)
    return y * lax.rsqrt(ms + RMS_EPS) * gain


def _qkv_kernel(x_ref, g_ref, w_ref, gains_ref, qa_ref, ka_ref, va_ref, qb_ref, kb_ref, vb_ref):
    lo = _low_lanes()
    h = _rms(x_ref[...], g_ref[...]).astype(BF16)
    n_slabs = w_ref.shape[1] // LANES
    for c in range(n_slabs // 2):
        y = _dot(h, w_ref[:, c * MXU_COLS:(c + 1) * MXU_COLS])
        for s in range(2):
            sl = 2 * c + s
            ys = y[:, s * LANES:(s + 1) * LANES]
            if sl < 4:
                qa_ref[sl] = _head_rms(ys, gains_ref[0:1, :], lo)
            elif sl < 8:
                ka_ref[sl - 4] = _head_rms(ys, gains_ref[1:2, :], lo)
            elif sl < 12:
                va_ref[sl - 8] = ys
            elif sl < 16:
                qb_ref[sl - 12] = _head_rms(ys, gains_ref[2:3, :], lo).astype(BF16)
            elif sl < 18:
                kb_ref[sl - 16] = _head_rms(ys, gains_ref[3:4, :], lo).astype(BF16)
            else:
                vb_ref[sl - 18] = ys.astype(BF16)


def _const_spec(shape):
    nd = len(shape)
    return pl.BlockSpec(shape, lambda *_: (0,) * nd, pipeline_mode=pl.Buffered(1))


def _qkv_call(x2, g, w, gains):
    t, d = x2.shape
    n_q = N_HEADS_A * HEAD_DIM // LANES
    n_kv = N_KV_B
    slab = lambda n, dt: jax.ShapeDtypeStruct((n, t, LANES), dt)
    slab_spec = lambda n: pl.BlockSpec((n, TM, LANES), lambda i: (0, i, 0))
    return pl.pallas_call(
        _qkv_kernel,
        grid=(t // TM,),
        in_specs=[pl.BlockSpec((TM, d), lambda i: (i, 0)), _const_spec(g.shape),
                  _const_spec(w.shape), _const_spec(gains.shape)],
        out_specs=[slab_spec(n_q), slab_spec(n_q), slab_spec(n_q), slab_spec(n_q),
                   slab_spec(n_kv), slab_spec(n_kv)],
        out_shape=[slab(n_q, F32), slab(n_q, F32), slab(n_q, F32), slab(n_q, BF16),
                   slab(n_kv, BF16), slab(n_kv, BF16)],
        compiler_params=pltpu.CompilerParams(dimension_semantics=("arbitrary",),
                                             vmem_limit_bytes=VMEM_LIMIT),
        name="qkv",
    )(x2, g, w, gains)


def _attn_block(q, kw, vw, tab, lo):
    zero = jnp.zeros_like(q)
    q2 = jnp.concatenate([jnp.where(lo, q, zero), jnp.where(lo, zero, q)], axis=0).astype(BF16)
    s = lax.dot_general(q2, kw.astype(BF16), (((1,), (1,)), ((), ())),
                        preferred_element_type=F32) + tab
    m = jnp.max(s, axis=-1, keepdims=True)
    p = jnp.exp(s - m)
    l = jnp.sum(p, axis=-1, keepdims=True)
    o2 = _dot(p.astype(BF16), vw.astype(BF16))
    return o2, m, l


def _merge_heads(a2, lo):
    bq = a2.shape[0] // 2
    return jnp.where(lo, a2[:bq], a2[bq:])


def _block_geometry(i, nb, seq_len, win):
    q0 = pl.multiple_of(i * BQ, BQ)
    w0 = pl.multiple_of(jnp.clip(q0 - (win - BQ) // 2, 0, seq_len - win), HEAD_DIM)
    tidx = jnp.where(i == 0, 0, jnp.where(i == nb - 1, 2, 1))
    return q0, w0, tidx


def _attn_a_kernel(q_ref, k_ref, v_ref, t1_ref, t4_ref, t16_ref, o_ref,
                   acc1, m1, l1, acc4, m4, l4, acc16, m16, l16, qs, ks, vs):
    lo = _low_lanes()
    seq = q_ref.shape[0]
    win = t1_ref.shape[-1]

    def run_blocks(qr, kr, vr, tab_ref, seq_len, store):
        nb = seq_len // BQ

        def body(i, carry):
            q0, w0, tidx = _block_geometry(i, nb, seq_len, win)
            o2, m, l = _attn_block(qr[pl.ds(q0, BQ), :], kr[pl.ds(w0, win), :], vr[pl.ds(w0, win), :],
                                   tab_ref[tidx], lo)
            store(q0, _merge_heads(o2, lo), _merge_heads(m, lo), _merge_heads(l, lo))
            return carry

        lax.fori_loop(0, nb, body, 0)

    def store1(q0, a, m, l):
        acc1[pl.ds(q0, BQ), :] = a
        m1[pl.ds(q0, BQ), :] = m
        l1[pl.ds(q0, BQ), :] = l

    run_blocks(q_ref, k_ref, v_ref, t1_ref, seq, store1)

    d4 = DILATED_PATTERNS[1][1]
    sub4 = seq // d4
    for r in range(d4):
        qs[...] = q_ref[pl.ds(r, sub4, stride=d4), :]
        ks[...] = k_ref[pl.ds(r, sub4, stride=d4), :]
        vs[...] = v_ref[pl.ds(r, sub4, stride=d4), :]

        def store4(q0, a, m, l, r=r):
            rows = pl.ds(r + d4 * q0, BQ, stride=d4)
            acc4[rows, :] = a
            m4[rows, :] = m
            l4[rows, :] = l

        run_blocks(qs, ks, vs, t4_ref, sub4, store4)

    d16 = DILATED_PATTERNS[2][1]
    sub16 = seq // d16

    def body16(r, carry):
        rows = pl.ds(r, sub16, stride=d16)
        o2, m, l = _attn_block(q_ref[rows, :], k_ref[rows, :], v_ref[rows, :], t16_ref[0], lo)
        acc16[rows, :] = _merge_heads(o2, lo)
        m16[rows, :] = _merge_heads(m, lo)
        l16[rows, :] = _merge_heads(l, lo)
        return carry

    lax.fori_loop(0, d16, body16, 0)

    chunk = 2 * BQ

    def combine(i, carry):
        rows = pl.ds(pl.multiple_of(i * chunk, chunk), chunk)
        ma, mb, mc = m1[rows, :], m4[rows, :], m16[rows, :]
        mx = jnp.maximum(jnp.maximum(ma, mb), mc)
        wa, wb, wc = jnp.exp(ma - mx), jnp.exp(mb - mx), jnp.exp(mc - mx)
        num = wa * acc1[rows, :] + wb * acc4[rows, :] + wc * acc16[rows, :]
        den = wa * l1[rows, :] + wb * l4[rows, :] + wc * l16[rows, :]
        o_ref[rows, :] = (num / den).astype(o_ref.dtype)
        return carry

    lax.fori_loop(0, seq // chunk, combine, 0)


def _attn_a_call(qa, ka, va, t1, t4, t16, batch, seq):
    n_slab, t, _ = qa.shape
    assert seq // DILATED_PATTERNS[2][1] == BQ
    qkv_spec = pl.BlockSpec((None, seq, LANES), lambda hp, b: (hp, b, 0))
    tab_spec = lambda tab: pl.BlockSpec((None,) + tab.shape[1:], lambda hp, b: (hp, 0, 0, 0))
    full = pltpu.VMEM((seq, LANES), F32)
    sub = pltpu.VMEM((seq // DILATED_PATTERNS[1][1], LANES), F32)
    return pl.pallas_call(
        _attn_a_kernel,
        grid=(n_slab, batch),
        in_specs=[qkv_spec, qkv_spec, qkv_spec, tab_spec(t1), tab_spec(t4), tab_spec(t16)],
        out_specs=pl.BlockSpec((seq, LANES), lambda hp, b: (b, hp)),
        out_shape=jax.ShapeDtypeStruct((t, n_slab * LANES), BF16),
        scratch_shapes=[full] * 9 + [sub] * 3,
        compiler_params=pltpu.CompilerParams(dimension_semantics=("arbitrary", "arbitrary"),
                                             vmem_limit_bytes=VMEM_LIMIT),
        name="attn_a",
    )(qa, ka, va, t1, t4, t16)


def _attn_b_kernel(q_ref, k_ref, v_ref, tab_ref, sink_ref, o_ref):
    lo = _low_lanes()
    seq = q_ref.shape[0]
    win = tab_ref.shape[-1]
    nb = seq // BQ
    sink = sink_ref[...]

    def body(i, carry):
        q0, w0, tidx = _block_geometry(i, nb, seq, win)
        o2, m, l = _attn_block(q_ref[pl.ds(q0, BQ), :], k_ref[pl.ds(w0, win), :], v_ref[pl.ds(w0, win), :],
                               tab_ref[tidx], lo)
        m_new = jnp.maximum(m, sink)
        scale = jnp.exp(m - m_new)
        den = l * scale + jnp.exp(sink - m_new)
        o_ref[pl.ds(q0, BQ), :] = _merge_heads(o2 * (scale / den), lo).astype(o_ref.dtype)
        return carry

    lax.fori_loop(0, nb, body, 0)


def _attn_b_call(qb, kb, vb, tab, sink, batch, seq):
    n_slab, t, _ = qb.shape
    group = n_slab // kb.shape[0]
    q_spec = pl.BlockSpec((None, seq, LANES), lambda hp, b: (hp, b, 0))
    kv_spec = pl.BlockSpec((None, seq, LANES), lambda hp, b: (hp // group, b, 0))
    return pl.pallas_call(
        _attn_b_kernel,
        grid=(n_slab, batch),
        in_specs=[q_spec, kv_spec, kv_spec,
                  pl.BlockSpec((None,) + tab.shape[1:], lambda hp, b: (hp, 0, 0, 0)),
                  pl.BlockSpec((None,) + sink.shape[1:], lambda hp, b: (hp, 0, 0))],
        out_specs=pl.BlockSpec((seq, LANES), lambda hp, b: (b, hp)),
        out_shape=jax.ShapeDtypeStruct((t, n_slab * LANES), BF16),
        compiler_params=pltpu.CompilerParams(dimension_semantics=("arbitrary", "arbitrary"),
                                             vmem_limit_bytes=VMEM_LIMIT),
        name="attn_b",
    )(qb, kb, vb, tab, sink)


def _mix_kernel(x_ref, ya_ref, yb_ref, g_ref, wg_ref, wa_ref, wb_ref, wo_ref, o_ref, merged):
    d = x_ref.shape[1]
    h = _rms(x_ref[...], g_ref[...]).astype(BF16)
    ya = ya_ref[...]
    yb = yb_ref[...]
    for c in range(d // MXU_COLS):
        cs = slice(c * MXU_COLS, (c + 1) * MXU_COLS)
        ga = _dot(h, wg_ref[:, cs])
        gb = _dot(h, wg_ref[:, d + c * MXU_COLS:d + (c + 1) * MXU_COLS])
        pa = _dot(ya, wa_ref[:, cs])
        pb = _dot(yb, wb_ref[:, cs])
        merged[:, cs] = (jax.nn.sigmoid(ga) * pa + jax.nn.sigmoid(gb) * pb).astype(BF16)
    mg = merged[...]
    for c in range(d // MXU_COLS):
        cs = slice(c * MXU_COLS, (c + 1) * MXU_COLS)
        o_ref[:, cs] = x_ref[:, cs] + _dot(mg, wo_ref[:, cs])


def _row_call(kernel, name, row_inputs, const_inputs, out_dim, scratch):
    t = row_inputs[0].shape[0]
    row_spec = lambda a: pl.BlockSpec((TM, a.shape[1]), lambda i: (i, 0))
    return pl.pallas_call(
        kernel,
        grid=(t // TM,),
        in_specs=[row_spec(a) for a in row_inputs] + [_const_spec(a.shape) for a in const_inputs],
        out_specs=pl.BlockSpec((TM, out_dim), lambda i: (i, 0)),
        out_shape=jax.ShapeDtypeStruct((t, out_dim), F32),
        scratch_shapes=scratch,
        compiler_params=pltpu.CompilerParams(dimension_semantics=("arbitrary",),
                                             vmem_limit_bytes=VMEM_LIMIT),
        name=name,
    )(*row_inputs, *const_inputs)


def _ffn_kernel(x_ref, g_ref, wgate_ref, wup_ref, wdown_ref, o_ref, hid):
    d = x_ref.shape[1]
    h = _rms(x_ref[...], g_ref[...]).astype(BF16)
    for c in range(wgate_ref.shape[1] // MXU_COLS):
        cs = slice(c * MXU_COLS, (c + 1) * MXU_COLS)
        a = _dot(h, wgate_ref[:, cs])
        u = _dot(h, wup_ref[:, cs])
        hid[:, cs] = (a * jax.nn.sigmoid(a) * u).astype(BF16)
    hv = hid[...]
    for c in range(d // MXU_COLS):
        cs = slice(c * MXU_COLS, (c + 1) * MXU_COLS)
        o_ref[:, cs] = x_ref[:, cs] + _dot(hv, wdown_ref[:, cs])


def _ple_kernel(x_ref, p_ref, g_ref, wpg_ref, wpp_ref, o_ref):
    d = x_ref.shape[1]
    h = _rms(x_ref[...], g_ref[...]).astype(BF16)
    pv = p_ref[...].astype(BF16)
    for c in range(d // MXU_COLS):
        cs = slice(c * MXU_COLS, (c + 1) * MXU_COLS)
        o_ref[:, cs] = x_ref[:, cs] + jax.nn.sigmoid(_dot(h, wpg_ref[:, cs])) * _dot(pv, wpp_ref[:, cs])


def _t5_bucket(rel):
    half_b = NUM_BUCKETS // 2
    max_exact = half_b // 2
    sign = jnp.where(rel > 0, half_b, 0)
    n = jnp.abs(rel)
    nf = jnp.maximum(n, 1).astype(F32)
    large = max_exact + (jnp.log(nf / max_exact) / math.log(MAX_DISTANCE / max_exact)
                         * (half_b - max_exact)).astype(jnp.int32)
    large = jnp.minimum(large, half_b - 1)
    return sign + jnp.where(n < max_exact, n, large)


def _band_tables(table, seq_len, win, half, dilation):
    nb = seq_len // BQ
    blocks = (0, min(1, nb - 1), nb - 1) if nb > 1 else (0,)
    starts = [max(0, min(i * BQ - (win - BQ) // 2, seq_len - win)) - i * BQ for i in blocks]
    i = jnp.arange(BQ, dtype=jnp.int32)[:, None]
    j = jnp.arange(win, dtype=jnp.int32)[None, :]
    tabs = []
    for s in starts:
        rel = s + j - i
        bias = jnp.transpose(table[_t5_bucket(rel * dilation)], (2, 0, 1)).astype(F32)
        tabs.append(jnp.where((jnp.abs(rel) <= half)[None], bias, NEG_INF))
    t = jnp.stack(tabs, axis=1)
    h = t.shape[0]
    return t.reshape(h // 2, 2, len(starts), BQ, win).transpose(0, 2, 1, 3, 4).reshape(
        h // 2, len(starts), 2 * BQ, win)


def _pair_gain(g, scale=1.0):
    return jnp.tile(g.astype(F32) * scale, 2)


def kernel(x, p, rel_table, norm_mix_g, w_in, qnorm_a_g, knorm_a_g, qnorm_b_g, knorm_b_g, sink_b,
           w_branch_a, w_branch_b, w_out, norm_ffn_g, w_ffn_gate, w_ffn_up, w_ffn_down,
           norm_ple_g, w_ple_gate, w_ple_proj):
    batch, seq, d = x.shape
    depth = p.shape[0]
    t = batch * seq
    wa_ = N_HEADS_A * HEAD_DIM
    wbq = N_HEADS_B * HEAD_DIM
    wkv = N_KV_B * HEAD_DIM
    q_scale = HEAD_DIM ** -0.5

    table_a = rel_table[:, :N_HEADS_A]
    table_b = rel_table[:, N_HEADS_A:]
    tabs_a = []
    for w, dil in DILATED_PATTERNS:
        sub = seq // dil
        tabs_a.append(_band_tables(table_a, sub, min(2 * BQ, sub), w // (2 * dil), dil))
    tab_b = _band_tables(table_b, seq, BQ + 2 * WINDOW_B, WINDOW_B, 1)

    x2 = x.reshape(t, d)
    for l in range(depth):
        o = 0
        cols = {}
        for name, width in (("qa", wa_), ("ka", wa_), ("va", wa_), ("qb", wbq), ("kb", wkv), ("vb", wkv),
                            ("gates", 2 * d)):
            cols[name] = w_in[l][:, o:o + width]
            o += width
        dup = lambda wmat: jnp.repeat(wmat.reshape(d, N_KV_B, 1, HEAD_DIM), 2, axis=2).reshape(d, 2 * wkv)
        w_qkv = jnp.concatenate([cols["qa"], cols["ka"], cols["va"], cols["qb"], dup(cols["kb"]),
                                 dup(cols["vb"])], axis=1).astype(BF16)
        gains = jnp.stack([_pair_gain(qnorm_a_g[l], q_scale), _pair_gain(knorm_a_g[l]),
                           _pair_gain(qnorm_b_g[l], q_scale), _pair_gain(knorm_b_g[l])])
        row = lambda v: v.astype(F32).reshape(1, -1)

        qa, ka, va, qb, kb, vb = _qkv_call(x2, row(norm_mix_g[l]), w_qkv, gains)
        ya = _attn_a_call(qa, ka, va, *tabs_a, batch, seq)
        sink = jnp.repeat(sink_b[l].astype(F32).reshape(N_HEADS_B // 2, 2, 1), BQ, axis=1).reshape(
            N_HEADS_B // 2, 2 * BQ, 1)
        yb = _attn_b_call(qb, kb, vb, tab_b, sink, batch, seq)

        x2 = _row_call(_mix_kernel, "mix", [x2, ya, yb],
                       [row(norm_mix_g[l]), cols["gates"].astype(BF16), w_branch_a[l].astype(BF16),
                        w_branch_b[l].astype(BF16), w_out[l].astype(BF16)],
                       d, [pltpu.VMEM((TM, d), BF16)])
        x2 = _row_call(_ffn_kernel, "ffn", [x2],
                       [row(norm_ffn_g[l]), w_ffn_gate[l].astype(BF16), w_ffn_up[l].astype(BF16),
                        w_ffn_down[l].astype(BF16)],
                       d, [pltpu.VMEM((TM, w_ffn_gate.shape[2]), BF16)])
        x2 = _row_call(_ple_kernel, "ple", [x2, p[l].reshape(t, -1)],
                       [row(norm_ple_g[l]), w_ple_gate[l].astype(BF16), w_ple_proj[l].astype(BF16)],
                       d, [])
    return x2.reshape(batch, seq, d)
```

```python
import math

import jax
import jax.numpy as jnp
from jax import lax
from jax.experimental import pallas as pl
from jax.experimental.pallas import tpu as pltpu

F32 = jnp.float32
BF16 = jnp.bfloat16

HEAD_DIM = 64
N_HEADS_A = 8
N_HEADS_B = 8
N_KV_B = 2
DILATED_PATTERNS = ((128, 1), (512, 4), (2048, 16))
WINDOW_B = 128
NUM_BUCKETS = 32
MAX_DISTANCE = 1024
RMS_EPS = 1e-6
NEG_INF = -1e30

WIDTH_A = N_HEADS_A * HEAD_DIM
WIDTH_BQ = N_HEADS_B * HEAD_DIM
WIDTH_BKV = N_KV_B * HEAD_DIM
GATE_COL = 3 * WIDTH_A + WIDTH_BQ + 2 * WIDTH_BKV

LANES = 128
MXU_COLS = 256
BQ = 128
BLOCK_UNROLL = 4
TM = 512
VMEM_LIMIT = 56 * 1024 * 1024


def _dot(a, b):
    return jnp.dot(a, b, preferred_element_type=F32)


def _rms(x, g):
    ms = jnp.mean(x * x, axis=-1, keepdims=True)
    return x * lax.rsqrt(ms + RMS_EPS) * g


def _low_lanes():
    return lax.broadcasted_iota(jnp.int32, (1, LANES), 1) < HEAD_DIM


def _head_rms(y, gain, lo):
    sq = y * y
    s0 = jnp.sum(jnp.where(lo, sq, 0.0), axis=-1, keepdims=True)
    s1 = jnp.sum(jnp.where(lo, 0.0, sq), axis=-1, keepdims=True)
    ms = jnp.where(lo, s0, s1) * (1.0 / HEAD_DIM)
    return y * lax.rsqrt(ms + RMS_EPS) * gain


def _const_spec(shape):
    nd = len(shape)
    return pl.BlockSpec(shape, lambda *_: (0,) * nd, pipeline_mode=pl.Buffered(1))


def _layer_spec(stacked, layer):
    nd = stacked.ndim - 1
    return pl.BlockSpec((None,) + stacked.shape[1:], lambda *_: (layer,) + (0,) * nd,
                        pipeline_mode=pl.Buffered(1))


def _qkv_kernel(x_ref, g_ref, w_ref, gains_ref, qa_ref, ka_ref, va_ref, qb_ref, kb_ref, vb_ref):
    lo = _low_lanes()
    h = _rms(x_ref[...], g_ref[...]).astype(BF16)
    n_pair = WIDTH_A // MXU_COLS
    for c in range(GATE_COL // MXU_COLS):
        y = _dot(h, w_ref[:, c * MXU_COLS:(c + 1) * MXU_COLS])
        y0, y1 = y[:, :LANES], y[:, LANES:]
        if c < n_pair:
            qa_ref[2 * c] = _head_rms(y0, gains_ref[0:1, :], lo)
            qa_ref[2 * c + 1] = _head_rms(y1, gains_ref[0:1, :], lo)
        elif c < 2 * n_pair:
            ka_ref[2 * (c - n_pair)] = _head_rms(y0, gains_ref[1:2, :], lo)
            ka_ref[2 * (c - n_pair) + 1] = _head_rms(y1, gains_ref[1:2, :], lo)
        elif c < 3 * n_pair:
            va_ref[2 * (c - 2 * n_pair)] = y0
            va_ref[2 * (c - 2 * n_pair) + 1] = y1
        elif c < 4 * n_pair:
            qb_ref[2 * (c - 3 * n_pair)] = _head_rms(y0, gains_ref[2:3, :], lo).astype(BF16)
            qb_ref[2 * (c - 3 * n_pair) + 1] = _head_rms(y1, gains_ref[2:3, :], lo).astype(BF16)
        else:
            kn = _head_rms(y0, gains_ref[3:4, :], lo)
            ks = pltpu.roll(kn, HEAD_DIM, axis=1)
            vs = pltpu.roll(y1, HEAD_DIM, axis=1)
            kb_ref[0] = jnp.where(lo, kn, ks).astype(BF16)
            kb_ref[1] = jnp.where(lo, ks, kn).astype(BF16)
            vb_ref[0] = jnp.where(lo, y1, vs).astype(BF16)
            vb_ref[1] = jnp.where(lo, vs, y1).astype(BF16)


def _qkv_call(x2, g, w_in, gains, layer):
    t, d = x2.shape
    n_q = WIDTH_A // LANES
    n_kv = N_KV_B
    assert WIDTH_BQ == WIDTH_A and 2 * WIDTH_BKV == MXU_COLS
    slab = lambda n, dt: jax.ShapeDtypeStruct((n, t, LANES), dt)
    slab_spec = lambda n: pl.BlockSpec((n, TM, LANES), lambda i: (0, i, 0))
    return pl.pallas_call(
        _qkv_kernel,
        grid=(t // TM,),
        in_specs=[pl.BlockSpec((TM, d), lambda i: (i, 0)), _layer_spec(g, layer),
                  _layer_spec(w_in, layer), _const_spec(gains.shape)],
        out_specs=[slab_spec(n_q), slab_spec(n_q), slab_spec(n_q), slab_spec(n_q),
                   slab_spec(n_kv), slab_spec(n_kv)],
        out_shape=[slab(n_q, F32), slab(n_q, F32), slab(n_q, F32), slab(n_q, BF16),
                   slab(n_kv, BF16), slab(n_kv, BF16)],
        compiler_params=pltpu.CompilerParams(dimension_semantics=("arbitrary",),
                                             vmem_limit_bytes=VMEM_LIMIT),
        name="qkv",
    )(x2, g, w_in, gains)


def _attn_block(q, kw, vw, tab, lo):
    zero = jnp.zeros_like(q)
    q2 = jnp.concatenate([jnp.where(lo, q, zero), jnp.where(lo, zero, q)], axis=0).astype(BF16)
    s = lax.dot_general(q2, kw.astype(BF16), (((1,), (1,)), ((), ())),
                        preferred_element_type=F32) + tab
    m = jnp.max(s, axis=-1, keepdims=True)
    p = jnp.exp(s - m)
    l = jnp.sum(p, axis=-1, keepdims=True)
    o2 = _dot(p.astype(BF16), vw.astype(BF16))
    return o2, m, l


def _merge_heads(a2, lo):
    bq = a2.shape[0] // 2
    return jnp.where(lo, a2[:bq], a2[bq:])


def _block_geometry(i, nb, seq_len, win):
    q0 = pl.multiple_of(i * BQ, BQ)
    w0 = pl.multiple_of(jnp.clip(q0 - (win - BQ) // 2, 0, seq_len - win), HEAD_DIM)
    tidx = jnp.where(i == 0, 0, jnp.where(i == nb - 1, 2, 1))
    return q0, w0, tidx


def _attn_a_kernel(q_ref, k_ref, v_ref, t1_ref, t4_ref, t16_ref, o_ref,
                   acc1, m1, l1, acc4, m4, l4, acc16, m16, l16, qs, ks, vs):
    lo = _low_lanes()
    seq = q_ref.shape[0]
    win = t1_ref.shape[-1]

    def run_blocks(qr, kr, vr, tab_ref, seq_len, store):
        nb = seq_len // BQ

        def body(i, carry):
            q0, w0, tidx = _block_geometry(i, nb, seq_len, win)
            o2, m, l = _attn_block(qr[pl.ds(q0, BQ), :], kr[pl.ds(w0, win), :], vr[pl.ds(w0, win), :],
                                   tab_ref[tidx], lo)
            store(q0, _merge_heads(o2, lo), _merge_heads(m, lo), _merge_heads(l, lo))
            return carry

        lax.fori_loop(0, nb, body, 0, unroll=BLOCK_UNROLL)

    def store1(q0, a, m, l):
        acc1[pl.ds(q0, BQ), :] = a
        m1[pl.ds(q0, BQ), :] = m
        l1[pl.ds(q0, BQ), :] = l

    run_blocks(q_ref, k_ref, v_ref, t1_ref, seq, store1)

    d4 = DILATED_PATTERNS[1][1]
    sub4 = seq // d4
    for r in range(d4):
        qs[...] = q_ref[pl.ds(r, sub4, stride=d4), :]
        ks[...] = k_ref[pl.ds(r, sub4, stride=d4), :]
        vs[...] = v_ref[pl.ds(r, sub4, stride=d4), :]

        def store4(q0, a, m, l, r=r):
            rows = pl.ds(r + d4 * q0, BQ, stride=d4)
            acc4[rows, :] = a
            m4[rows, :] = m
            l4[rows, :] = l

        run_blocks(qs, ks, vs, t4_ref, sub4, store4)

    d16 = DILATED_PATTERNS[2][1]
    sub16 = seq // d16

    def body16(r, carry):
        rows = pl.ds(r, sub16, stride=d16)
        o2, m, l = _attn_block(q_ref[rows, :], k_ref[rows, :], v_ref[rows, :], t16_ref[0], lo)
        acc16[rows, :] = _merge_heads(o2, lo)
        m16[rows, :] = _merge_heads(m, lo)
        l16[rows, :] = _merge_heads(l, lo)
        return carry

    lax.fori_loop(0, d16, body16, 0, unroll=BLOCK_UNROLL)

    chunk = 2 * BQ

    def combine(i, carry):
        rows = pl.ds(pl.multiple_of(i * chunk, chunk), chunk)
        ma, mb, mc = m1[rows, :], m4[rows, :], m16[rows, :]
        mx = jnp.maximum(jnp.maximum(ma, mb), mc)
        wa, wb, wc = jnp.exp(ma - mx), jnp.exp(mb - mx), jnp.exp(mc - mx)
        num = wa * acc1[rows, :] + wb * acc4[rows, :] + wc * acc16[rows, :]
        den = wa * l1[rows, :] + wb * l4[rows, :] + wc * l16[rows, :]
        o_ref[rows, :] = (num / den).astype(o_ref.dtype)
        return carry

    lax.fori_loop(0, seq // chunk, combine, 0)


def _attn_a_call(qa, ka, va, t1, t4, t16, batch, seq):
    n_slab, t, _ = qa.shape
    assert seq // DILATED_PATTERNS[2][1] == BQ
    qkv_spec = pl.BlockSpec((None, seq, LANES), lambda hp, b: (hp, b, 0))
    tab_spec = lambda tab: pl.BlockSpec((None,) + tab.shape[1:], lambda hp, b: (hp, 0, 0, 0))
    full = pltpu.VMEM((seq, LANES), F32)
    sub = pltpu.VMEM((seq // DILATED_PATTERNS[1][1], LANES), F32)
    return pl.pallas_call(
        _attn_a_kernel,
        grid=(n_slab, batch),
        in_specs=[qkv_spec, qkv_spec, qkv_spec, tab_spec(t1), tab_spec(t4), tab_spec(t16)],
        out_specs=pl.BlockSpec((seq, LANES), lambda hp, b: (b, hp)),
        out_shape=jax.ShapeDtypeStruct((t, n_slab * LANES), BF16),
        scratch_shapes=[full] * 9 + [sub] * 3,
        compiler_params=pltpu.CompilerParams(dimension_semantics=("arbitrary", "arbitrary"),
                                             vmem_limit_bytes=VMEM_LIMIT),
        name="attn_a",
    )(qa, ka, va, t1, t4, t16)


def _attn_b_kernel(q_ref, k_ref, v_ref, tab_ref, sink_ref, o_ref):
    lo = _low_lanes()
    seq = q_ref.shape[0]
    win = tab_ref.shape[-1]
    nb = seq // BQ
    sink = sink_ref[...]

    def body(i, carry):
        q0, w0, tidx = _block_geometry(i, nb, seq, win)
        o2, m, l = _attn_block(q_ref[pl.ds(q0, BQ), :], k_ref[pl.ds(w0, win), :], v_ref[pl.ds(w0, win), :],
                               tab_ref[tidx], lo)
        m_new = jnp.maximum(m, sink)
        scale = jnp.exp(m - m_new)
        den = l * scale + jnp.exp(sink - m_new)
        o_ref[pl.ds(q0, BQ), :] = _merge_heads(o2 * (scale / den), lo).astype(o_ref.dtype)
        return carry

    lax.fori_loop(0, nb, body, 0, unroll=BLOCK_UNROLL)


def _attn_b_call(qb, kb, vb, tab, sink, batch, seq):
    n_slab, t, _ = qb.shape
    group = n_slab // kb.shape[0]
    q_spec = pl.BlockSpec((None, seq, LANES), lambda hp, b: (hp, b, 0))
    kv_spec = pl.BlockSpec((None, seq, LANES), lambda hp, b: (hp // group, b, 0))
    return pl.pallas_call(
        _attn_b_kernel,
        grid=(n_slab, batch),
        in_specs=[q_spec, kv_spec, kv_spec,
                  pl.BlockSpec((None,) + tab.shape[1:], lambda hp, b: (hp, 0, 0, 0)),
                  pl.BlockSpec((None,) + sink.shape[1:], lambda hp, b: (hp, 0, 0))],
        out_specs=pl.BlockSpec((seq, LANES), lambda hp, b: (b, hp)),
        out_shape=jax.ShapeDtypeStruct((t, n_slab * LANES), BF16),
        compiler_params=pltpu.CompilerParams(dimension_semantics=("arbitrary", "arbitrary"),
                                             vmem_limit_bytes=VMEM_LIMIT),
        name="attn_b",
    )(qb, kb, vb, tab, sink)


def _row_call(kernel, name, row_inputs, layer_inputs, layer, out_dim, scratch):
    t = row_inputs[0].shape[0]

    def row_spec(a):
        if a.ndim == 2:
            return pl.BlockSpec((TM, a.shape[1]), lambda i: (i, 0))
        return pl.BlockSpec((None, TM, a.shape[2]), lambda i: (layer, i, 0))

    return pl.pallas_call(
        kernel,
        grid=(t // TM,),
        in_specs=[row_spec(a) for a in row_inputs] + [_layer_spec(a, layer) for a in layer_inputs],
        out_specs=pl.BlockSpec((TM, out_dim), lambda i: (i, 0)),
        out_shape=jax.ShapeDtypeStruct((t, out_dim), F32),
        scratch_shapes=scratch,
        compiler_params=pltpu.CompilerParams(dimension_semantics=("arbitrary",),
                                             vmem_limit_bytes=VMEM_LIMIT),
        name=name,
    )(*row_inputs, *layer_inputs)


def _mix_kernel(x_ref, ya_ref, yb_ref, g_ref, win_ref, wa_ref, wb_ref, wo_ref, o_ref, merged):
    d = x_ref.shape[1]
    h = _rms(x_ref[...], g_ref[...]).astype(BF16)
    ya = ya_ref[...]
    yb = yb_ref[...]
    for c in range(d // MXU_COLS):
        cs = slice(c * MXU_COLS, (c + 1) * MXU_COLS)
        ga = _dot(h, win_ref[:, GATE_COL + c * MXU_COLS:GATE_COL + (c + 1) * MXU_COLS])
        gb = _dot(h, win_ref[:, GATE_COL + d + c * MXU_COLS:GATE_COL + d + (c + 1) * MXU_COLS])
        pa = _dot(ya, wa_ref[:, cs])
        pb = _dot(yb, wb_ref[:, cs])
        merged[:, cs] = (jax.nn.sigmoid(ga) * pa + jax.nn.sigmoid(gb) * pb).astype(BF16)
    mg = merged[...]
    for c in range(d // MXU_COLS):
        cs = slice(c * MXU_COLS, (c + 1) * MXU_COLS)
        o_ref[:, cs] = x_ref[:, cs] + _dot(mg, wo_ref[:, cs])


def _ffn_kernel(x_ref, g_ref, wgate_ref, wup_ref, wdown_ref, o_ref, hid):
    d = x_ref.shape[1]
    h = _rms(x_ref[...], g_ref[...]).astype(BF16)
    for c in range(wgate_ref.shape[1] // MXU_COLS):
        cs = slice(c * MXU_COLS, (c + 1) * MXU_COLS)
        a = _dot(h, wgate_ref[:, cs])
        u = _dot(h, wup_ref[:, cs])
        hid[:, cs] = (a * jax.nn.sigmoid(a) * u).astype(BF16)
    hv = hid[...]
    for c in range(d // MXU_COLS):
        cs = slice(c * MXU_COLS, (c + 1) * MXU_COLS)
        o_ref[:, cs] = x_ref[:, cs] + _dot(hv, wdown_ref[:, cs])


def _ple_kernel(x_ref, p_ref, g_ref, wpg_ref, wpp_ref, o_ref):
    d = x_ref.shape[1]
    h = _rms(x_ref[...], g_ref[...]).astype(BF16)
    pv = p_ref[...].astype(BF16)
    for c in range(d // MXU_COLS):
        cs = slice(c * MXU_COLS, (c + 1) * MXU_COLS)
        o_ref[:, cs] = x_ref[:, cs] + jax.nn.sigmoid(_dot(h, wpg_ref[:, cs])) * _dot(pv, wpp_ref[:, cs])


def _t5_bucket(rel):
    half_b = NUM_BUCKETS // 2
    max_exact = half_b // 2
    sign = jnp.where(rel > 0, half_b, 0)
    n = jnp.abs(rel)
    nf = jnp.maximum(n, 1).astype(F32)
    large = max_exact + (jnp.log(nf / max_exact) / math.log(MAX_DISTANCE / max_exact)
                         * (half_b - max_exact)).astype(jnp.int32)
    large = jnp.minimum(large, half_b - 1)
    return sign + jnp.where(n < max_exact, n, large)


def _band_tables(table, seq_len, win, half, dilation):
    nb = seq_len // BQ
    blocks = (0, min(1, nb - 1), nb - 1) if nb > 1 else (0,)
    starts = [max(0, min(i * BQ - (win - BQ) // 2, seq_len - win)) - i * BQ for i in blocks]
    n = BQ + win - 1
    tabs = []
    for s in starts:
        rel = s - (BQ - 1) + jnp.arange(n, dtype=jnp.int32)
        vals = jnp.where((jnp.abs(rel) <= half)[:, None], table[_t5_bucket(rel * dilation)].astype(F32), NEG_INF)
        vpad = jnp.pad(vals.T, ((0, 0), (0, 1)))
        skew = jnp.tile(vpad, (1, BQ))[:, :BQ * n].reshape(-1, BQ, n)
        tabs.append(skew[:, :, BQ - 1:BQ - 1 + win])
    t = jnp.stack(tabs, axis=1)
    h = t.shape[0]
    return t.reshape(h // 2, 2, len(starts), BQ, win).transpose(0, 2, 1, 3, 4).reshape(
        h // 2, len(starts), 2 * BQ, win)


def _pair_gain(g, scale=1.0):
    return jnp.tile(g.astype(F32) * scale, 2)


def kernel(x, p, rel_table, norm_mix_g, w_in, qnorm_a_g, knorm_a_g, qnorm_b_g, knorm_b_g, sink_b,
           w_branch_a, w_branch_b, w_out, norm_ffn_g, w_ffn_gate, w_ffn_up, w_ffn_down,
           norm_ple_g, w_ple_gate, w_ple_proj):
    batch, seq, d = x.shape
    depth = p.shape[0]
    t = batch * seq
    q_scale = HEAD_DIM ** -0.5

    table_a = rel_table[:, :N_HEADS_A]
    table_b = rel_table[:, N_HEADS_A:]
    tabs_a = []
    for w, dil in DILATED_PATTERNS:
        sub = seq // dil
        tabs_a.append(_band_tables(table_a, sub, min(2 * BQ, sub), w // (2 * dil), dil))
    tab_b = _band_tables(table_b, seq, BQ + 2 * WINDOW_B, WINDOW_B, 1)

    bf = lambda w: w.astype(BF16)
    rows = lambda g: g.astype(F32).reshape(depth, 1, -1)
    w_in_b, w_a_b, w_b_b, w_o_b = bf(w_in), bf(w_branch_a), bf(w_branch_b), bf(w_out)
    w_gate_b, w_up_b, w_down_b = bf(w_ffn_gate), bf(w_ffn_up), bf(w_ffn_down)
    w_pg_b, w_pp_b = bf(w_ple_gate), bf(w_ple_proj)
    g_mix, g_ffn, g_ple = rows(norm_mix_g), rows(norm_ffn_g), rows(norm_ple_g)
    p2 = p.reshape(depth, t, -1)

    x2 = x.reshape(t, d)
    for l in range(depth):
        gains = jnp.stack([_pair_gain(qnorm_a_g[l], q_scale), _pair_gain(knorm_a_g[l]),
                           _pair_gain(qnorm_b_g[l], q_scale), _pair_gain(knorm_b_g[l])])
        qa, ka, va, qb, kb, vb = _qkv_call(x2, g_mix, w_in_b, gains, l)
        ya = _attn_a_call(qa, ka, va, *tabs_a, batch, seq)
        sink = jnp.repeat(sink_b[l].astype(F32).reshape(N_HEADS_B // 2, 2, 1), BQ, axis=1)
        yb = _attn_b_call(qb, kb, vb, tab_b, sink, batch, seq)
        x2 = _row_call(_mix_kernel, "mix", [x2, ya, yb], [g_mix, w_in_b, w_a_b, w_b_b, w_o_b], l,
                       d, [pltpu.VMEM((TM, d), BF16)])
        x2 = _row_call(_ffn_kernel, "ffn", [x2], [g_ffn, w_gate_b, w_up_b, w_down_b], l,
                       d, [pltpu.VMEM((TM, w_ffn_gate.shape[2]), BF16)])
        x2 = _row_call(_ple_kernel, "ple", [x2, p2], [g_ple, w_pg_b, w_pp_b], l, d, [])
    return x2.reshape(batch, seq, d)
```

```python
import math

import jax
import jax.numpy as jnp
from jax import lax
from jax.experimental import pallas as pl
from jax.experimental.pallas import tpu as pltpu

F32 = jnp.float32
BF16 = jnp.bfloat16

HEAD_DIM = 64
N_HEADS_A = 8
N_HEADS_B = 8
N_KV_B = 2
DILATED_PATTERNS = ((128, 1), (512, 4), (2048, 16))
WINDOW_B = 128
NUM_BUCKETS = 32
MAX_DISTANCE = 1024
RMS_EPS = 1e-6
NEG_INF = -1e30
LOG2_E = math.log2(math.e)

WIDTH_A = N_HEADS_A * HEAD_DIM
WIDTH_BQ = N_HEADS_B * HEAD_DIM
WIDTH_BKV = N_KV_B * HEAD_DIM
GATE_COL = 3 * WIDTH_A + WIDTH_BQ + 2 * WIDTH_BKV

LANES = 128
MXU_COLS = 256
BQ = 128
BLOCK_UNROLL = 16
TM = 512
VMEM_LIMIT = 56 * 1024 * 1024


def _dot(a, b):
    return jnp.dot(a, b, preferred_element_type=F32)


def _rms(x, g):
    ms = jnp.mean(x * x, axis=-1, keepdims=True)
    return x * lax.rsqrt(ms + RMS_EPS) * g


def _low_lanes():
    return lax.broadcasted_iota(jnp.int32, (1, LANES), 1) < HEAD_DIM


def _head_rms(y, gain, lo):
    sq = y * y
    s0 = jnp.sum(jnp.where(lo, sq, 0.0), axis=-1, keepdims=True)
    s1 = jnp.sum(jnp.where(lo, 0.0, sq), axis=-1, keepdims=True)
    ms = jnp.where(lo, s0, s1) * (1.0 / HEAD_DIM)
    return y * lax.rsqrt(ms + RMS_EPS) * gain


def _const_spec(shape):
    nd = len(shape)
    return pl.BlockSpec(shape, lambda *_: (0,) * nd, pipeline_mode=pl.Buffered(1))


def _layer_spec(stacked, layer):
    nd = stacked.ndim - 1
    return pl.BlockSpec((None,) + stacked.shape[1:], lambda *_: (layer,) + (0,) * nd,
                        pipeline_mode=pl.Buffered(1))


def _qkv_kernel(x_ref, g_ref, w_ref, gains_ref, qa_ref, ka_ref, va_ref, qb_ref, kb_ref, vb_ref):
    lo = _low_lanes()
    h = _rms(x_ref[...], g_ref[...]).astype(BF16)
    n_pair = WIDTH_A // MXU_COLS
    for c in range(GATE_COL // MXU_COLS):
        y = _dot(h, w_ref[:, c * MXU_COLS:(c + 1) * MXU_COLS])
        y0, y1 = y[:, :LANES], y[:, LANES:]
        if c < n_pair:
            qa_ref[2 * c] = _head_rms(y0, gains_ref[0:1, :], lo)
            qa_ref[2 * c + 1] = _head_rms(y1, gains_ref[0:1, :], lo)
        elif c < 2 * n_pair:
            ka_ref[2 * (c - n_pair)] = _head_rms(y0, gains_ref[1:2, :], lo)
            ka_ref[2 * (c - n_pair) + 1] = _head_rms(y1, gains_ref[1:2, :], lo)
        elif c < 3 * n_pair:
            va_ref[2 * (c - 2 * n_pair)] = y0
            va_ref[2 * (c - 2 * n_pair) + 1] = y1
        elif c < 4 * n_pair:
            qb_ref[2 * (c - 3 * n_pair)] = _head_rms(y0, gains_ref[2:3, :], lo).astype(BF16)
            qb_ref[2 * (c - 3 * n_pair) + 1] = _head_rms(y1, gains_ref[2:3, :], lo).astype(BF16)
        else:
            kn = _head_rms(y0, gains_ref[3:4, :], lo)
            ks = pltpu.roll(kn, HEAD_DIM, axis=1)
            vs = pltpu.roll(y1, HEAD_DIM, axis=1)
            kb_ref[0] = jnp.where(lo, kn, ks).astype(BF16)
            kb_ref[1] = jnp.where(lo, ks, kn).astype(BF16)
            vb_ref[0] = jnp.where(lo, y1, vs).astype(BF16)
            vb_ref[1] = jnp.where(lo, vs, y1).astype(BF16)


def _qkv_call(x2, g, w_in, gains, layer):
    t, d = x2.shape
    n_q = WIDTH_A // LANES
    n_kv = N_KV_B
    assert WIDTH_BQ == WIDTH_A and 2 * WIDTH_BKV == MXU_COLS
    slab = lambda n, dt: jax.ShapeDtypeStruct((n, t, LANES), dt)
    slab_spec = lambda n: pl.BlockSpec((n, TM, LANES), lambda i: (0, i, 0))
    return pl.pallas_call(
        _qkv_kernel,
        grid=(t // TM,),
        in_specs=[pl.BlockSpec((TM, d), lambda i: (i, 0)), _layer_spec(g, layer),
                  _layer_spec(w_in, layer), _const_spec(gains.shape)],
        out_specs=[slab_spec(n_q), slab_spec(n_q), slab_spec(n_q), slab_spec(n_q),
                   slab_spec(n_kv), slab_spec(n_kv)],
        out_shape=[slab(n_q, F32), slab(n_q, F32), slab(n_q, F32), slab(n_q, BF16),
                   slab(n_kv, BF16), slab(n_kv, BF16)],
        compiler_params=pltpu.CompilerParams(dimension_semantics=("arbitrary",),
                                             vmem_limit_bytes=VMEM_LIMIT),
        name="qkv",
    )(x2, g, w_in, gains)


def _attn_block(q, kw, vw, tab, lo):
    zero = jnp.zeros_like(q)
    q2 = jnp.concatenate([jnp.where(lo, q, zero), jnp.where(lo, zero, q)], axis=0).astype(BF16)
    s = lax.dot_general(q2, kw.astype(BF16), (((1,), (1,)), ((), ())),
                        preferred_element_type=F32) + tab
    m = jnp.max(s, axis=-1, keepdims=True)
    p = jnp.exp2(s - m)
    vw1 = jnp.concatenate([vw.astype(BF16), jnp.ones(vw.shape, BF16)], axis=1)
    o2 = _dot(p.astype(BF16), vw1)
    return o2[:, :LANES], m, o2[:, LANES:]


def _merge_heads(a2, lo):
    bq = a2.shape[0] // 2
    return jnp.where(lo, a2[:bq], a2[bq:])


def _block_geometry(i, nb, seq_len, win):
    q0 = pl.multiple_of(i * BQ, BQ)
    w0 = pl.multiple_of(jnp.clip(q0 - (win - BQ) // 2, 0, seq_len - win), HEAD_DIM)
    tidx = jnp.where(i == 0, 0, jnp.where(i == nb - 1, 2, 1))
    return q0, w0, tidx


def _attn_a_kernel(q_ref, k_ref, v_ref, t1_ref, t4_ref, t16_ref, o_ref,
                   acc1, m1, l1, outf, *sub_scratch):
    lo = _low_lanes()
    seq = q_ref.shape[0]
    win = t1_ref.shape[-1]
    d4 = DILATED_PATTERNS[1][1]
    step = DILATED_PATTERNS[2][1] // d4
    sub4 = seq // d4

    def block(qr, kr, vr, q_rows, kv_rows, tab, outs, out_rows):
        o2, m, l = _attn_block(qr[q_rows, :], kr[kv_rows, :], vr[kv_rows, :], tab, lo)
        for ref, val in zip(outs, (o2, m, l)):
            ref[out_rows, :] = _merge_heads(val, lo)

    def geometry(i, n_blocks, seq_len):
        w0 = max(0, min(i * BQ - (win - BQ) // 2, seq_len - win))
        return pl.ds(i * BQ, BQ), pl.ds(w0, win), (0 if i == 0 else (2 if i == n_blocks - 1 else 1))

    nb = seq // BQ

    def block1(i):
        q_rows, kv_rows, tidx = geometry(i, nb, seq)
        block(q_ref, k_ref, v_ref, q_rows, kv_rows, t1_ref[tidx], (acc1, m1, l1), q_rows)

    nb4 = sub4 // BQ
    class_blocks, class_merges = [], []
    for r in range(d4):
        qs, ks, vs, acc4, m4, l4, acc16, m16, l16 = sub_scratch[9 * r:9 * (r + 1)]
        res = pl.ds(r, sub4, stride=d4)
        qs[...] = q_ref[res, :]
        ks[...] = k_ref[res, :]
        vs[...] = v_ref[res, :]

        def block4(i, qs=qs, ks=ks, vs=vs, outs=(acc4, m4, l4)):
            q_rows, kv_rows, tidx = geometry(i, nb4, sub4)
            block(qs, ks, vs, q_rows, kv_rows, t4_ref[tidx], outs, q_rows)

        def block16(b, qs=qs, ks=ks, vs=vs, outs=(acc16, m16, l16)):
            rows = pl.ds(b, BQ, stride=step)
            block(qs, ks, vs, rows, rows, t16_ref[0], outs, rows)

        def merge(i, r=r, acc4=acc4, m4=m4, l4=l4, acc16=acc16, m16=m16, l16=l16):
            rows = pl.ds(i * BQ, BQ)
            nat = pl.ds(r + d4 * i * BQ, BQ, stride=d4)
            ma, mb, mc = m1[nat, :], m4[rows, :], m16[rows, :]
            mx = jnp.maximum(jnp.maximum(ma, mb), mc)
            wa, wb, wc = jnp.exp2(ma - mx), jnp.exp2(mb - mx), jnp.exp2(mc - mx)
            num = wa * acc1[nat, :] + wb * acc4[rows, :] + wc * acc16[rows, :]
            den = wa * l1[nat, :] + wb * l4[rows, :] + wc * l16[rows, :]
            outf[nat, :] = num / den

        class_blocks.append([(block4, i) for i in range(nb4)] + [(block16, b) for b in range(step)])
        class_merges.append([(merge, i) for i in range(nb4)])

    for i in range(nb):
        block1(i)
    for blocks, merges in zip(class_blocks, class_merges):
        for fn, arg in blocks + merges:
            fn(arg)

    o_ref[...] = outf[...].astype(o_ref.dtype)


def _attn_a_call(qa, ka, va, t1, t4, t16, batch, seq):
    n_slab, t, _ = qa.shape
    assert seq // DILATED_PATTERNS[2][1] == BQ
    qkv_spec = pl.BlockSpec((None, seq, LANES), lambda hp, b: (hp, b, 0))
    tab_spec = lambda tab: pl.BlockSpec((None,) + tab.shape[1:], lambda hp, b: (hp, 0, 0, 0))
    full = pltpu.VMEM((seq, LANES), F32)
    d4 = DILATED_PATTERNS[1][1]
    sub = pltpu.VMEM((seq // d4, LANES), F32)
    return pl.pallas_call(
        _attn_a_kernel,
        grid=(n_slab, batch),
        in_specs=[qkv_spec, qkv_spec, qkv_spec, tab_spec(t1), tab_spec(t4), tab_spec(t16)],
        out_specs=pl.BlockSpec((seq, LANES), lambda hp, b: (b, hp)),
        out_shape=jax.ShapeDtypeStruct((t, n_slab * LANES), BF16),
        scratch_shapes=[full] * 4 + [sub] * (9 * d4),
        compiler_params=pltpu.CompilerParams(dimension_semantics=("arbitrary", "arbitrary"),
                                             vmem_limit_bytes=VMEM_LIMIT),
        name="attn_a",
    )(qa, ka, va, t1, t4, t16)


def _attn_b_kernel(q_ref, k_ref, v_ref, tab_ref, sink_ref, o_ref):
    lo = _low_lanes()
    seq = q_ref.shape[0]
    win = tab_ref.shape[-1]
    nb = seq // BQ
    sink = sink_ref[...]

    def body(i, carry):
        q0, w0, tidx = _block_geometry(i, nb, seq, win)
        o2, m, l = _attn_block(q_ref[pl.ds(q0, BQ), :], k_ref[pl.ds(w0, win), :], v_ref[pl.ds(w0, win), :],
                               tab_ref[tidx], lo)
        o, m, l = _merge_heads(o2, lo), _merge_heads(m, lo), _merge_heads(l, lo)
        m_new = jnp.maximum(m, sink)
        scale = jnp.exp2(m - m_new)
        den = l * scale + jnp.exp2(sink - m_new)
        o_ref[pl.ds(q0, BQ), :] = (o * (scale / den)).astype(o_ref.dtype)
        return carry

    lax.fori_loop(0, nb, body, 0, unroll=BLOCK_UNROLL)


def _attn_b_call(qb, kb, vb, tab, sink, batch, seq):
    n_slab, t, _ = qb.shape
    group = n_slab // kb.shape[0]
    q_spec = pl.BlockSpec((None, seq, LANES), lambda hp, b: (hp, b, 0))
    kv_spec = pl.BlockSpec((None, seq, LANES), lambda hp, b: (hp // group, b, 0))
    return pl.pallas_call(
        _attn_b_kernel,
        grid=(n_slab, batch),
        in_specs=[q_spec, kv_spec, kv_spec,
                  pl.BlockSpec((None,) + tab.shape[1:], lambda hp, b: (hp, 0, 0, 0)),
                  pl.BlockSpec((None,) + sink.shape[1:], lambda hp, b: (hp, 0, 0))],
        out_specs=pl.BlockSpec((seq, LANES), lambda hp, b: (b, hp)),
        out_shape=jax.ShapeDtypeStruct((t, n_slab * LANES), BF16),
        compiler_params=pltpu.CompilerParams(dimension_semantics=("arbitrary", "arbitrary"),
                                             vmem_limit_bytes=VMEM_LIMIT),
        name="attn_b",
    )(qb, kb, vb, tab, sink)


def _row_call(kernel, name, row_inputs, layer_inputs, layer, out_dim, scratch):
    t = row_inputs[0].shape[0]

    def row_spec(a):
        if a.ndim == 2:
            return pl.BlockSpec((TM, a.shape[1]), lambda i: (i, 0))
        return pl.BlockSpec((None, TM, a.shape[2]), lambda i: (layer, i, 0))

    return pl.pallas_call(
        kernel,
        grid=(t // TM,),
        in_specs=[row_spec(a) for a in row_inputs] + [_layer_spec(a, layer) for a in layer_inputs],
        out_specs=pl.BlockSpec((TM, out_dim), lambda i: (i, 0)),
        out_shape=jax.ShapeDtypeStruct((t, out_dim), F32),
        scratch_shapes=scratch,
        compiler_params=pltpu.CompilerParams(dimension_semantics=("arbitrary",),
                                             vmem_limit_bytes=VMEM_LIMIT),
        name=name,
    )(*row_inputs, *layer_inputs)


def _mix_kernel(x_ref, ya_ref, yb_ref, g_ref, win_ref, wa_ref, wb_ref, wo_ref, o_ref, merged):
    d = x_ref.shape[1]
    h = _rms(x_ref[...], g_ref[...]).astype(BF16)
    ya = ya_ref[...]
    yb = yb_ref[...]
    for c in range(d // MXU_COLS):
        cs = slice(c * MXU_COLS, (c + 1) * MXU_COLS)
        ga = _dot(h, win_ref[:, GATE_COL + c * MXU_COLS:GATE_COL + (c + 1) * MXU_COLS])
        gb = _dot(h, win_ref[:, GATE_COL + d + c * MXU_COLS:GATE_COL + d + (c + 1) * MXU_COLS])
        pa = _dot(ya, wa_ref[:, cs])
        pb = _dot(yb, wb_ref[:, cs])
        merged[:, cs] = (jax.nn.sigmoid(ga) * pa + jax.nn.sigmoid(gb) * pb).astype(BF16)
    mg = merged[...]
    for c in range(d // MXU_COLS):
        cs = slice(c * MXU_COLS, (c + 1) * MXU_COLS)
        o_ref[:, cs] = x_ref[:, cs] + _dot(mg, wo_ref[:, cs])


def _ffn_kernel(x_ref, g_ref, wgate_ref, wup_ref, wdown_ref, o_ref, hid):
    d = x_ref.shape[1]
    h = _rms(x_ref[...], g_ref[...]).astype(BF16)
    for c in range(wgate_ref.shape[1] // MXU_COLS):
        cs = slice(c * MXU_COLS, (c + 1) * MXU_COLS)
        a = _dot(h, wgate_ref[:, cs])
        u = _dot(h, wup_ref[:, cs])
        hid[:, cs] = (a * jax.nn.sigmoid(a) * u).astype(BF16)
    hv = hid[...]
    for c in range(d // MXU_COLS):
        cs = slice(c * MXU_COLS, (c + 1) * MXU_COLS)
        o_ref[:, cs] = x_ref[:, cs] + _dot(hv, wdown_ref[:, cs])


def _ple_kernel(x_ref, p_ref, g_ref, wpg_ref, wpp_ref, o_ref):
    d = x_ref.shape[1]
    h = _rms(x_ref[...], g_ref[...]).astype(BF16)
    pv = p_ref[...].astype(BF16)
    for c in range(d // MXU_COLS):
        cs = slice(c * MXU_COLS, (c + 1) * MXU_COLS)
        o_ref[:, cs] = x_ref[:, cs] + jax.nn.sigmoid(_dot(h, wpg_ref[:, cs])) * _dot(pv, wpp_ref[:, cs])


def _t5_bucket(rel):
    half_b = NUM_BUCKETS // 2
    max_exact = half_b // 2
    sign = jnp.where(rel > 0, half_b, 0)
    n = jnp.abs(rel)
    nf = jnp.maximum(n, 1).astype(F32)
    large = max_exact + (jnp.log(nf / max_exact) / math.log(MAX_DISTANCE / max_exact)
                         * (half_b - max_exact)).astype(jnp.int32)
    large = jnp.minimum(large, half_b - 1)
    return sign + jnp.where(n < max_exact, n, large)


def _band_tables(table, seq_len, win, half, dilation):
    nb = seq_len // BQ
    blocks = (0, min(1, nb - 1), nb - 1) if nb > 1 else (0,)
    starts = [max(0, min(i * BQ - (win - BQ) // 2, seq_len - win)) - i * BQ for i in blocks]
    n = BQ + win - 1
    tabs = []
    for s in starts:
        rel = s - (BQ - 1) + jnp.arange(n, dtype=jnp.int32)
        bias = table[_t5_bucket(rel * dilation)].astype(F32) * LOG2_E
        vals = jnp.where((jnp.abs(rel) <= half)[:, None], bias, NEG_INF)
        vpad = jnp.pad(vals.T, ((0, 0), (0, 1)))
        skew = jnp.tile(vpad, (1, BQ))[:, :BQ * n].reshape(-1, BQ, n)
        tabs.append(skew[:, :, BQ - 1:BQ - 1 + win])
    t = jnp.stack(tabs, axis=1)
    h = t.shape[0]
    return t.reshape(h // 2, 2, len(starts), BQ, win).transpose(0, 2, 1, 3, 4).reshape(
        h // 2, len(starts), 2 * BQ, win)


def _pair_gain(g, scale=1.0):
    return jnp.tile(g.astype(F32) * scale, 2)


def kernel(x, p, rel_table, norm_mix_g, w_in, qnorm_a_g, knorm_a_g, qnorm_b_g, knorm_b_g, sink_b,
           w_branch_a, w_branch_b, w_out, norm_ffn_g, w_ffn_gate, w_ffn_up, w_ffn_down,
           norm_ple_g, w_ple_gate, w_ple_proj):
    batch, seq, d = x.shape
    depth = p.shape[0]
    t = batch * seq
    q_scale = HEAD_DIM ** -0.5 * LOG2_E

    table_a = rel_table[:, :N_HEADS_A]
    table_b = rel_table[:, N_HEADS_A:]
    tabs_a = []
    for w, dil in DILATED_PATTERNS:
        sub = seq // dil
        tabs_a.append(_band_tables(table_a, sub, min(2 * BQ, sub), w // (2 * dil), dil))
    tab_b = _band_tables(table_b, seq, BQ + 2 * WINDOW_B, WINDOW_B, 1)

    bf = lambda w: w.astype(BF16)
    rows = lambda g: g.astype(F32).reshape(depth, 1, -1)
    w_in_b, w_a_b, w_b_b, w_o_b = bf(w_in), bf(w_branch_a), bf(w_branch_b), bf(w_out)
    w_gate_b, w_up_b, w_down_b = bf(w_ffn_gate), bf(w_ffn_up), bf(w_ffn_down)
    w_pg_b, w_pp_b = bf(w_ple_gate), bf(w_ple_proj)
    g_mix, g_ffn, g_ple = rows(norm_mix_g), rows(norm_ffn_g), rows(norm_ple_g)
    p2 = p.reshape(depth, t, -1)

    x2 = x.reshape(t, d)
    for l in range(depth):
        gains = jnp.stack([_pair_gain(qnorm_a_g[l], q_scale), _pair_gain(knorm_a_g[l]),
                           _pair_gain(qnorm_b_g[l], q_scale), _pair_gain(knorm_b_g[l])])
        qa, ka, va, qb, kb, vb = _qkv_call(x2, g_mix, w_in_b, gains, l)
        ya = _attn_a_call(qa, ka, va, *tabs_a, batch, seq)
        sink = jnp.repeat(sink_b[l].astype(F32) * LOG2_E, HEAD_DIM).reshape(N_HEADS_B // 2, 1, LANES)
        yb = _attn_b_call(qb, kb, vb, tab_b, sink, batch, seq)
        x2 = _row_call(_mix_kernel, "mix", [x2, ya, yb], [g_mix, w_in_b, w_a_b, w_b_b, w_o_b], l,
                       d, [pltpu.VMEM((TM, d), BF16)])
        x2 = _row_call(_ffn_kernel, "ffn", [x2], [g_ffn, w_gate_b, w_up_b, w_down_b], l,
                       d, [pltpu.VMEM((TM, w_ffn_gate.shape[2]), BF16)])
        x2 = _row_call(_ple_kernel, "ple", [x2, p2], [g_ple, w_pg_b, w_pp_b], l, d, [])
    return x2.reshape(batch, seq, d)
```

```python
import math

import jax
import jax.numpy as jnp
from jax import lax
from jax.experimental import pallas as pl
from jax.experimental.pallas import tpu as pltpu

F32 = jnp.float32
BF16 = jnp.bfloat16

HEAD_DIM = 64
N_HEADS_A = 8
N_HEADS_B = 8
N_KV_B = 2
DILATED_PATTERNS = ((128, 1), (512, 4), (2048, 16))
WINDOW_B = 128
NUM_BUCKETS = 32
MAX_DISTANCE = 1024
RMS_EPS = 1e-6
NEG_INF = -1e30
LOG2_E = math.log2(math.e)

WIDTH_A = N_HEADS_A * HEAD_DIM
WIDTH_BQ = N_HEADS_B * HEAD_DIM
WIDTH_BKV = N_KV_B * HEAD_DIM
GATE_COL = 3 * WIDTH_A + WIDTH_BQ + 2 * WIDTH_BKV

LANES = 128
MXU_COLS = 256
BQ = 128
BLOCK_UNROLL = 16
TM = 512
VMEM_LIMIT = 56 * 1024 * 1024


def _dot(a, b):
    return jnp.dot(a, b, preferred_element_type=F32)


def _rms(x, g):
    ms = jnp.mean(x * x, axis=-1, keepdims=True)
    return x * lax.rsqrt(ms + RMS_EPS) * g


def _low_lanes():
    return lax.broadcasted_iota(jnp.int32, (1, LANES), 1) < HEAD_DIM


def _head_rms(y, gain, lo):
    sq = y * y
    s0 = jnp.sum(jnp.where(lo, sq, 0.0), axis=-1, keepdims=True)
    s1 = jnp.sum(jnp.where(lo, 0.0, sq), axis=-1, keepdims=True)
    ms = jnp.where(lo, s0, s1) * (1.0 / HEAD_DIM)
    return y * lax.rsqrt(ms + RMS_EPS) * gain


def _const_spec(shape):
    nd = len(shape)
    return pl.BlockSpec(shape, lambda *_: (0,) * nd, pipeline_mode=pl.Buffered(1))


def _layer_spec(stacked, layer):
    nd = stacked.ndim - 1
    return pl.BlockSpec((None,) + stacked.shape[1:], lambda *_: (layer,) + (0,) * nd,
                        pipeline_mode=pl.Buffered(1))


def _qkv_kernel(x_ref, g_ref, w_ref, gains_ref, qa_ref, ka_ref, va_ref, qb_ref, kb_ref, vb_ref):
    lo = _low_lanes()
    h = _rms(x_ref[...], g_ref[...]).astype(BF16)
    n_pair = WIDTH_A // MXU_COLS
    order = sorted(range(GATE_COL // MXU_COLS), key=lambda c: 2 * n_pair <= c < 3 * n_pair)
    for c in order:
        y = _dot(h, w_ref[:, c * MXU_COLS:(c + 1) * MXU_COLS])
        y0, y1 = y[:, :LANES], y[:, LANES:]
        if c < n_pair:
            qa_ref[2 * c] = _head_rms(y0, gains_ref[0:1, :], lo)
            qa_ref[2 * c + 1] = _head_rms(y1, gains_ref[0:1, :], lo)
        elif c < 2 * n_pair:
            ka_ref[2 * (c - n_pair)] = _head_rms(y0, gains_ref[1:2, :], lo)
            ka_ref[2 * (c - n_pair) + 1] = _head_rms(y1, gains_ref[1:2, :], lo)
        elif c < 3 * n_pair:
            va_ref[2 * (c - 2 * n_pair)] = y0
            va_ref[2 * (c - 2 * n_pair) + 1] = y1
        elif c < 4 * n_pair:
            qb_ref[2 * (c - 3 * n_pair)] = _head_rms(y0, gains_ref[2:3, :], lo).astype(BF16)
            qb_ref[2 * (c - 3 * n_pair) + 1] = _head_rms(y1, gains_ref[2:3, :], lo).astype(BF16)
        else:
            kn = _head_rms(y0, gains_ref[3:4, :], lo)
            ks = pltpu.roll(kn, HEAD_DIM, axis=1)
            vs = pltpu.roll(y1, HEAD_DIM, axis=1)
            kb_ref[0] = jnp.where(lo, kn, ks).astype(BF16)
            kb_ref[1] = jnp.where(lo, ks, kn).astype(BF16)
            vb_ref[0] = jnp.where(lo, y1, vs).astype(BF16)
            vb_ref[1] = jnp.where(lo, vs, y1).astype(BF16)


def _qkv_call(x2, g, w_in, gains, layer):
    t, d = x2.shape
    n_q = WIDTH_A // LANES
    n_kv = N_KV_B
    assert WIDTH_BQ == WIDTH_A and 2 * WIDTH_BKV == MXU_COLS
    slab = lambda n, dt: jax.ShapeDtypeStruct((n, t, LANES), dt)
    slab_spec = lambda n: pl.BlockSpec((n, TM, LANES), lambda i: (0, i, 0))
    return pl.pallas_call(
        _qkv_kernel,
        grid=(t // TM,),
        in_specs=[pl.BlockSpec((TM, d), lambda i: (i, 0)), _layer_spec(g, layer),
                  _layer_spec(w_in, layer), _const_spec(gains.shape)],
        out_specs=[slab_spec(n_q), slab_spec(n_q), slab_spec(n_q), slab_spec(n_q),
                   slab_spec(n_kv), slab_spec(n_kv)],
        out_shape=[slab(n_q, F32), slab(n_q, F32), slab(n_q, F32), slab(n_q, BF16),
                   slab(n_kv, BF16), slab(n_kv, BF16)],
        compiler_params=pltpu.CompilerParams(dimension_semantics=("arbitrary",),
                                             vmem_limit_bytes=VMEM_LIMIT),
        name="qkv",
    )(x2, g, w_in, gains)


def _attn_block(q, kw, vw, tab, lo):
    zero = jnp.zeros_like(q)
    q2 = jnp.concatenate([jnp.where(lo, q, zero), jnp.where(lo, zero, q)], axis=0).astype(BF16)
    s = lax.dot_general(q2, kw.astype(BF16), (((1,), (1,)), ((), ())),
                        preferred_element_type=F32) + tab
    m = jnp.max(s, axis=-1, keepdims=True)
    p = jnp.exp2(s - m)
    vw1 = jnp.concatenate([vw.astype(BF16), jnp.ones(vw.shape, BF16)], axis=1)
    o2 = _dot(p.astype(BF16), vw1)
    return o2[:, :LANES], m, o2[:, LANES:]


def _merge_heads(a2, lo):
    bq = a2.shape[0] // 2
    return jnp.where(lo, a2[:bq], a2[bq:])


def _block_geometry(i, nb, seq_len, win):
    q0 = pl.multiple_of(i * BQ, BQ)
    w0 = pl.multiple_of(jnp.clip(q0 - (win - BQ) // 2, 0, seq_len - win), HEAD_DIM)
    tidx = jnp.where(i == 0, 0, jnp.where(i == nb - 1, 2, 1))
    return q0, w0, tidx


def _attn_a_kernel(q_ref, k_ref, v_ref, t1_ref, t4_ref, t16_ref, o_ref,
                   acc1, m1, l1, outf, *sub_scratch):
    lo = _low_lanes()
    seq = q_ref.shape[0]
    win = t1_ref.shape[-1]
    d4 = DILATED_PATTERNS[1][1]
    step = DILATED_PATTERNS[2][1] // d4
    sub4 = seq // d4

    def block(qr, kr, vr, q_rows, kv_rows, tab, outs, out_rows):
        o2, m, l = _attn_block(qr[q_rows, :], kr[kv_rows, :], vr[kv_rows, :], tab, lo)
        for ref, val in zip(outs, (o2, m, l)):
            ref[out_rows, :] = _merge_heads(val, lo)

    def geometry(i, n_blocks, seq_len):
        w0 = max(0, min(i * BQ - (win - BQ) // 2, seq_len - win))
        return pl.ds(i * BQ, BQ), pl.ds(w0, win), (0 if i == 0 else (2 if i == n_blocks - 1 else 1))

    nb = seq // BQ

    def block1(i):
        q_rows, kv_rows, tidx = geometry(i, nb, seq)
        block(q_ref, k_ref, v_ref, q_rows, kv_rows, t1_ref[tidx], (acc1, m1, l1), q_rows)

    nb4 = sub4 // BQ
    class_blocks, class_merges = [], []
    for r in range(d4):
        qs, ks, vs, acc4, m4, l4, acc16, m16, l16 = sub_scratch[9 * r:9 * (r + 1)]
        res = pl.ds(r, sub4, stride=d4)
        qs[...] = q_ref[res, :]
        ks[...] = k_ref[res, :]
        vs[...] = v_ref[res, :]

        def block4(i, qs=qs, ks=ks, vs=vs, outs=(acc4, m4, l4)):
            q_rows, kv_rows, tidx = geometry(i, nb4, sub4)
            block(qs, ks, vs, q_rows, kv_rows, t4_ref[tidx], outs, q_rows)

        def block16(b, qs=qs, ks=ks, vs=vs, outs=(acc16, m16, l16)):
            rows = pl.ds(b, BQ, stride=step)
            block(qs, ks, vs, rows, rows, t16_ref[0], outs, rows)

        def merge(i, r=r, acc4=acc4, m4=m4, l4=l4, acc16=acc16, m16=m16, l16=l16):
            rows = pl.ds(i * BQ, BQ)
            nat = pl.ds(r + d4 * i * BQ, BQ, stride=d4)
            ma, mb, mc = m1[nat, :], m4[rows, :], m16[rows, :]
            mx = jnp.maximum(jnp.maximum(ma, mb), mc)
            wa, wb, wc = jnp.exp2(ma - mx), jnp.exp2(mb - mx), jnp.exp2(mc - mx)
            num = wa * acc1[nat, :] + wb * acc4[rows, :] + wc * acc16[rows, :]
            den = wa * l1[nat, :] + wb * l4[rows, :] + wc * l16[rows, :]
            outf[nat, :] = num / den

        class_blocks.append([(block4, i) for i in range(nb4)] + [(block16, b) for b in range(step)])
        class_merges.append([(merge, i) for i in range(nb4)])

    for i in range(nb):
        block1(i)
    for blocks, merges in zip(class_blocks, class_merges):
        for fn, arg in blocks + merges:
            fn(arg)

    o_ref[...] = outf[...].astype(o_ref.dtype)


def _attn_a_call(qa, ka, va, t1, t4, t16, batch, seq):
    n_slab, t, _ = qa.shape
    assert seq // DILATED_PATTERNS[2][1] == BQ
    qkv_spec = pl.BlockSpec((None, seq, LANES), lambda hp, b: (hp, b, 0))
    tab_spec = lambda tab: pl.BlockSpec((None,) + tab.shape[1:], lambda hp, b: (hp, 0, 0, 0))
    full = pltpu.VMEM((seq, LANES), F32)
    d4 = DILATED_PATTERNS[1][1]
    sub = pltpu.VMEM((seq // d4, LANES), F32)
    return pl.pallas_call(
        _attn_a_kernel,
        grid=(n_slab, batch),
        in_specs=[qkv_spec, qkv_spec, qkv_spec, tab_spec(t1), tab_spec(t4), tab_spec(t16)],
        out_specs=pl.BlockSpec((seq, LANES), lambda hp, b: (b, hp)),
        out_shape=jax.ShapeDtypeStruct((t, n_slab * LANES), BF16),
        scratch_shapes=[full] * 4 + [sub] * (9 * d4),
        compiler_params=pltpu.CompilerParams(dimension_semantics=("arbitrary", "arbitrary"),
                                             vmem_limit_bytes=VMEM_LIMIT),
        name="attn_a",
    )(qa, ka, va, t1, t4, t16)


def _attn_b_kernel(q_ref, k_ref, v_ref, tab_ref, sink_ref, o_ref):
    lo = _low_lanes()
    seq = q_ref.shape[0]
    win = tab_ref.shape[-1]
    nb = seq // BQ
    sink = sink_ref[...]

    def body(i, carry):
        q0, w0, tidx = _block_geometry(i, nb, seq, win)
        o2, m, l = _attn_block(q_ref[pl.ds(q0, BQ), :], k_ref[pl.ds(w0, win), :], v_ref[pl.ds(w0, win), :],
                               tab_ref[tidx], lo)
        o, m, l = _merge_heads(o2, lo), _merge_heads(m, lo), _merge_heads(l, lo)
        m_new = jnp.maximum(m, sink)
        scale = jnp.exp2(m - m_new)
        den = l * scale + jnp.exp2(sink - m_new)
        o_ref[pl.ds(q0, BQ), :] = (o * (scale / den)).astype(o_ref.dtype)
        return carry

    lax.fori_loop(0, nb, body, 0, unroll=BLOCK_UNROLL)


def _attn_b_call(qb, kb, vb, tab, sink, batch, seq):
    n_slab, t, _ = qb.shape
    group = n_slab // kb.shape[0]
    q_spec = pl.BlockSpec((None, seq, LANES), lambda hp, b: (hp, b, 0))
    kv_spec = pl.BlockSpec((None, seq, LANES), lambda hp, b: (hp // group, b, 0))
    return pl.pallas_call(
        _attn_b_kernel,
        grid=(n_slab, batch),
        in_specs=[q_spec, kv_spec, kv_spec,
                  pl.BlockSpec((None,) + tab.shape[1:], lambda hp, b: (hp, 0, 0, 0)),
                  pl.BlockSpec((None,) + sink.shape[1:], lambda hp, b: (hp, 0, 0))],
        out_specs=pl.BlockSpec((seq, LANES), lambda hp, b: (b, hp)),
        out_shape=jax.ShapeDtypeStruct((t, n_slab * LANES), BF16),
        compiler_params=pltpu.CompilerParams(dimension_semantics=("arbitrary", "arbitrary"),
                                             vmem_limit_bytes=VMEM_LIMIT),
        name="attn_b",
    )(qb, kb, vb, tab, sink)


def _row_call(kernel, name, row_inputs, layer_inputs, layer, out_dim, scratch):
    t = row_inputs[0].shape[0]

    def row_spec(a):
        if a.ndim == 2:
            return pl.BlockSpec((TM, a.shape[1]), lambda i: (i, 0))
        return pl.BlockSpec((None, TM, a.shape[2]), lambda i: (layer, i, 0))

    return pl.pallas_call(
        kernel,
        grid=(t // TM,),
        in_specs=[row_spec(a) for a in row_inputs] + [_layer_spec(a, layer) for a in layer_inputs],
        out_specs=pl.BlockSpec((TM, out_dim), lambda i: (i, 0)),
        out_shape=jax.ShapeDtypeStruct((t, out_dim), F32),
        scratch_shapes=scratch,
        compiler_params=pltpu.CompilerParams(dimension_semantics=("arbitrary",),
                                             vmem_limit_bytes=VMEM_LIMIT),
        name=name,
    )(*row_inputs, *layer_inputs)


def _post_kernel(x_ref, ya_ref, yb_ref, p_ref, gmix_ref, gffn_ref, gple_ref, wgates_ref, wa_ref, wb_ref, wo_ref,
                 wgate_ref, wup_ref, wdown_ref, wpg_ref, wpp_ref, o_ref, merged, hid, xs):
    d = x_ref.shape[1]
    chunks = [slice(c * MXU_COLS, (c + 1) * MXU_COLS) for c in range(d // MXU_COLS)]

    h = _rms(x_ref[...], gmix_ref[...]).astype(BF16)
    ya = ya_ref[...]
    yb = yb_ref[...]
    for c, cs in enumerate(chunks):
        ga = _dot(h, wgates_ref[:, cs])
        gb = _dot(h, wgates_ref[:, d + c * MXU_COLS:d + (c + 1) * MXU_COLS])
        pa = _dot(ya, wa_ref[:, cs])
        pb = _dot(yb, wb_ref[:, cs])
        merged[:, cs] = (jax.nn.sigmoid(ga) * pa + jax.nn.sigmoid(gb) * pb).astype(BF16)
    mg = merged[...]
    for cs in chunks:
        xs[:, cs] = x_ref[:, cs] + _dot(mg, wo_ref[:, cs])

    h = _rms(xs[...], gffn_ref[...]).astype(BF16)
    for c in range(wgate_ref.shape[1] // MXU_COLS):
        cs = slice(c * MXU_COLS, (c + 1) * MXU_COLS)
        a = _dot(h, wgate_ref[:, cs])
        u = _dot(h, wup_ref[:, cs])
        hid[:, cs] = (a * jax.nn.sigmoid(a) * u).astype(BF16)
    hv = hid[...]
    for cs in chunks:
        xs[:, cs] = xs[:, cs] + _dot(hv, wdown_ref[:, cs])

    h = _rms(xs[...], gple_ref[...]).astype(BF16)
    pv = p_ref[...].astype(BF16)
    for cs in chunks:
        o_ref[:, cs] = xs[:, cs] + jax.nn.sigmoid(_dot(h, wpg_ref[:, cs])) * _dot(pv, wpp_ref[:, cs])


def _t5_bucket(rel):
    half_b = NUM_BUCKETS // 2
    max_exact = half_b // 2
    sign = jnp.where(rel > 0, half_b, 0)
    n = jnp.abs(rel)
    nf = jnp.maximum(n, 1).astype(F32)
    large = max_exact + (jnp.log(nf / max_exact) / math.log(MAX_DISTANCE / max_exact)
                         * (half_b - max_exact)).astype(jnp.int32)
    large = jnp.minimum(large, half_b - 1)
    return sign + jnp.where(n < max_exact, n, large)


def _band_tables(table, seq_len, win, half, dilation):
    nb = seq_len // BQ
    blocks = (0, min(1, nb - 1), nb - 1) if nb > 1 else (0,)
    starts = [max(0, min(i * BQ - (win - BQ) // 2, seq_len - win)) - i * BQ for i in blocks]
    n = BQ + win - 1
    tabs = []
    for s in starts:
        rel = s - (BQ - 1) + jnp.arange(n, dtype=jnp.int32)
        bias = table[_t5_bucket(rel * dilation)].astype(F32) * LOG2_E
        vals = jnp.where((jnp.abs(rel) <= half)[:, None], bias, NEG_INF)
        vpad = jnp.pad(vals.T, ((0, 0), (0, 1)))
        skew = jnp.tile(vpad, (1, BQ))[:, :BQ * n].reshape(-1, BQ, n)
        tabs.append(skew[:, :, BQ - 1:BQ - 1 + win])
    t = jnp.stack(tabs, axis=1)
    h = t.shape[0]
    return t.reshape(h // 2, 2, len(starts), BQ, win).transpose(0, 2, 1, 3, 4).reshape(
        h // 2, len(starts), 2 * BQ, win)


def _pair_gain(g, scale=1.0):
    return jnp.tile(g.astype(F32) * scale, 2)


def kernel(x, p, rel_table, norm_mix_g, w_in, qnorm_a_g, knorm_a_g, qnorm_b_g, knorm_b_g, sink_b,
           w_branch_a, w_branch_b, w_out, norm_ffn_g, w_ffn_gate, w_ffn_up, w_ffn_down,
           norm_ple_g, w_ple_gate, w_ple_proj):
    batch, seq, d = x.shape
    depth = p.shape[0]
    t = batch * seq
    q_scale = HEAD_DIM ** -0.5 * LOG2_E

    table_a = rel_table[:, :N_HEADS_A]
    table_b = rel_table[:, N_HEADS_A:]
    tabs_a = []
    for w, dil in DILATED_PATTERNS:
        sub = seq // dil
        tabs_a.append(_band_tables(table_a, sub, min(2 * BQ, sub), w // (2 * dil), dil))
    tab_b = _band_tables(table_b, seq, BQ + 2 * WINDOW_B, WINDOW_B, 1)

    bf = lambda w: w.astype(BF16)
    rows = lambda g: g.astype(F32).reshape(depth, 1, -1)
    w_qkv_b, w_gates_b = bf(w_in[:, :, :GATE_COL]), bf(w_in[:, :, GATE_COL:])
    w_a_b, w_b_b, w_o_b = bf(w_branch_a), bf(w_branch_b), bf(w_out)
    w_gate_b, w_up_b, w_down_b = bf(w_ffn_gate), bf(w_ffn_up), bf(w_ffn_down)
    w_pg_b, w_pp_b = bf(w_ple_gate), bf(w_ple_proj)
    g_mix, g_ffn, g_ple = rows(norm_mix_g), rows(norm_ffn_g), rows(norm_ple_g)
    p2 = p.reshape(depth, t, -1)

    x2 = x.reshape(t, d)
    for l in range(depth):
        gains = jnp.stack([_pair_gain(qnorm_a_g[l], q_scale), _pair_gain(knorm_a_g[l]),
                           _pair_gain(qnorm_b_g[l], q_scale), _pair_gain(knorm_b_g[l])])
        qa, ka, va, qb, kb, vb = _qkv_call(x2, g_mix, w_qkv_b, gains, l)
        ya = _attn_a_call(qa, ka, va, *tabs_a, batch, seq)
        sink = jnp.repeat(sink_b[l].astype(F32) * LOG2_E, HEAD_DIM).reshape(N_HEADS_B // 2, 1, LANES)
        yb = _attn_b_call(qb, kb, vb, tab_b, sink, batch, seq)
        x2 = _row_call(_post_kernel, "post", [x2, ya, yb, p2],
                       [g_mix, g_ffn, g_ple, w_gates_b, w_a_b, w_b_b, w_o_b, w_gate_b, w_up_b, w_down_b,
                        w_pg_b, w_pp_b], l, d,
                       [pltpu.VMEM((TM, d), BF16), pltpu.VMEM((TM, w_ffn_gate.shape[2]), BF16),
                        pltpu.VMEM((TM, d), F32)])
    return x2.reshape(batch, seq, d)
```

```python
import math

import jax
import jax.numpy as jnp
from jax import lax
from jax.experimental import pallas as pl
from jax.experimental.pallas import tpu as pltpu

F32 = jnp.float32
BF16 = jnp.bfloat16

HEAD_DIM = 64
N_HEADS_A = 8
N_HEADS_B = 8
N_KV_B = 2
DILATED_PATTERNS = ((128, 1), (512, 4), (2048, 16))
WINDOW_B = 128
NUM_BUCKETS = 32
MAX_DISTANCE = 1024
RMS_EPS = 1e-6
NEG_INF = -1e30
LOG2_E = math.log2(math.e)

WIDTH_A = N_HEADS_A * HEAD_DIM
WIDTH_BQ = N_HEADS_B * HEAD_DIM
WIDTH_BKV = N_KV_B * HEAD_DIM
GATE_COL = 3 * WIDTH_A + WIDTH_BQ + 2 * WIDTH_BKV

LANES = 128
MXU_COLS = 256
BQ = 128
BLOCK_UNROLL = 16
TM = 512
VMEM_LIMIT = 56 * 1024 * 1024


def _dot(a, b):
    return jnp.dot(a, b, preferred_element_type=F32)


def _rms(x, g):
    ms = jnp.mean(x * x, axis=-1, keepdims=True)
    return x * lax.rsqrt(ms + RMS_EPS) * g


def _low_lanes():
    return lax.broadcasted_iota(jnp.int32, (1, LANES), 1) < HEAD_DIM


def _head_rms(y, gain, lo):
    sq = y * y
    s0 = jnp.sum(jnp.where(lo, sq, 0.0), axis=-1, keepdims=True)
    s1 = jnp.sum(jnp.where(lo, 0.0, sq), axis=-1, keepdims=True)
    ms = jnp.where(lo, s0, s1) * (1.0 / HEAD_DIM)
    return y * lax.rsqrt(ms + RMS_EPS) * gain


def _const_spec(shape):
    nd = len(shape)
    return pl.BlockSpec(shape, lambda *_: (0,) * nd, pipeline_mode=pl.Buffered(1))


def _layer_spec(stacked, layer):
    nd = stacked.ndim - 1
    return pl.BlockSpec((None,) + stacked.shape[1:], lambda *_: (layer,) + (0,) * nd,
                        pipeline_mode=pl.Buffered(1))


def _qkv_kernel(x_ref, g_ref, w_ref, gains_ref, qa_ref, ka_ref, va_ref, qb_ref, kb_ref, vb_ref):
    lo = _low_lanes()
    h = _rms(x_ref[...], g_ref[...]).astype(BF16)
    n_pair = WIDTH_A // MXU_COLS
    order = sorted(range(GATE_COL // MXU_COLS), key=lambda c: 2 * n_pair <= c < 3 * n_pair)
    for c in order:
        y = _dot(h, w_ref[:, c * MXU_COLS:(c + 1) * MXU_COLS])
        y0, y1 = y[:, :LANES], y[:, LANES:]
        if c < n_pair:
            qa_ref[2 * c] = _head_rms(y0, gains_ref[0:1, :], lo)
            qa_ref[2 * c + 1] = _head_rms(y1, gains_ref[0:1, :], lo)
        elif c < 2 * n_pair:
            ka_ref[2 * (c - n_pair)] = _head_rms(y0, gains_ref[1:2, :], lo)
            ka_ref[2 * (c - n_pair) + 1] = _head_rms(y1, gains_ref[1:2, :], lo)
        elif c < 3 * n_pair:
            va_ref[2 * (c - 2 * n_pair)] = y0
            va_ref[2 * (c - 2 * n_pair) + 1] = y1
        elif c < 4 * n_pair:
            qb_ref[2 * (c - 3 * n_pair)] = _head_rms(y0, gains_ref[2:3, :], lo).astype(BF16)
            qb_ref[2 * (c - 3 * n_pair) + 1] = _head_rms(y1, gains_ref[2:3, :], lo).astype(BF16)
        else:
            kn = _head_rms(y0, gains_ref[3:4, :], lo)
            ks = pltpu.roll(kn, HEAD_DIM, axis=1)
            vs = pltpu.roll(y1, HEAD_DIM, axis=1)
            kb_ref[0] = jnp.where(lo, kn, ks).astype(BF16)
            kb_ref[1] = jnp.where(lo, ks, kn).astype(BF16)
            vb_ref[0] = jnp.where(lo, y1, vs).astype(BF16)
            vb_ref[1] = jnp.where(lo, vs, y1).astype(BF16)


def _qkv_call(x2, g, w_in, gains, layer):
    t, d = x2.shape
    n_q = WIDTH_A // LANES
    n_kv = N_KV_B
    assert WIDTH_BQ == WIDTH_A and 2 * WIDTH_BKV == MXU_COLS
    slab = lambda n, dt: jax.ShapeDtypeStruct((n, t, LANES), dt)
    slab_spec = lambda n: pl.BlockSpec((n, TM, LANES), lambda i: (0, i, 0))
    return pl.pallas_call(
        _qkv_kernel,
        grid=(t // TM,),
        in_specs=[pl.BlockSpec((TM, d), lambda i: (i, 0)), _layer_spec(g, layer),
                  _layer_spec(w_in, layer), _const_spec(gains.shape)],
        out_specs=[slab_spec(n_q), slab_spec(n_q), slab_spec(n_q), slab_spec(n_q),
                   slab_spec(n_kv), slab_spec(n_kv)],
        out_shape=[slab(n_q, F32), slab(n_q, F32), slab(n_q, F32), slab(n_q, BF16),
                   slab(n_kv, BF16), slab(n_kv, BF16)],
        compiler_params=pltpu.CompilerParams(dimension_semantics=("arbitrary",),
                                             vmem_limit_bytes=VMEM_LIMIT),
        name="qkv",
    )(x2, g, w_in, gains)


def _attn_block(q, kw, vw, tab, lo):
    zero = jnp.zeros_like(q)
    q2 = jnp.concatenate([jnp.where(lo, q, zero), jnp.where(lo, zero, q)], axis=0).astype(BF16)
    s = lax.dot_general(q2, kw.astype(BF16), (((1,), (1,)), ((), ())),
                        preferred_element_type=F32) + tab
    m = jnp.max(s, axis=-1, keepdims=True)
    p = jnp.exp2(s - m)
    vw1 = jnp.concatenate([vw.astype(BF16), jnp.ones(vw.shape, BF16)], axis=1)
    o2 = _dot(p.astype(BF16), vw1)
    return o2[:, :LANES], m, o2[:, LANES:]


def _merge_heads(a2, lo):
    bq = a2.shape[0] // 2
    return jnp.where(lo, a2[:bq], a2[bq:])


def _block_geometry(i, nb, seq_len, win):
    q0 = pl.multiple_of(i * BQ, BQ)
    w0 = pl.multiple_of(jnp.clip(q0 - (win - BQ) // 2, 0, seq_len - win), HEAD_DIM)
    tidx = jnp.where(i == 0, 0, jnp.where(i == nb - 1, 2, 1))
    return q0, w0, tidx


def _attn_a_kernel(q_ref, k_ref, v_ref, t1_ref, t4_ref, t16_ref, out_ref,
                   o1, lse1, outf, qs, ks, vs, o4, lse4, o16g, lse16g, o16, lse16, outc):
    lo = _low_lanes()
    seq = q_ref.shape[0]
    win = t1_ref.shape[-1]
    d4 = DILATED_PATTERNS[1][1]
    step = DILATED_PATTERNS[2][1] // d4
    sub4 = seq // d4

    def block(qr, kr, vr, q_rows, kv_rows, tab, o_out, lse_out, out_rows):
        o2, m, l = _attn_block(qr[q_rows, :], kr[kv_rows, :], vr[kv_rows, :], tab, lo)
        l = _merge_heads(l, lo)
        o_out[out_rows, :] = _merge_heads(o2, lo) / l
        lse_out[out_rows, :] = _merge_heads(m, lo) + jnp.log2(l)

    def geometry(i, n_blocks, seq_len):
        w0 = max(0, min(i * BQ - (win - BQ) // 2, seq_len - win))
        return pl.ds(i * BQ, BQ), pl.ds(w0, win), (0 if i == 0 else (2 if i == n_blocks - 1 else 1))

    for r in range(d4):
        res = pl.ds(r, sub4, stride=d4)
        qs[r] = q_ref[res, :]
        ks[r] = k_ref[res, :]
        vs[r] = v_ref[res, :]

    nb = seq // BQ
    for i in range(nb):
        q_rows, kv_rows, tidx = geometry(i, nb, seq)
        block(q_ref, k_ref, v_ref, q_rows, kv_rows, t1_ref[tidx], o1, lse1, q_rows)

    nb4 = sub4 // BQ
    for i in range(nb4):
        for r in range(d4):
            q_rows, kv_rows, tidx = geometry(i, nb4, sub4)
            block(qs.at[r], ks.at[r], vs.at[r], q_rows, kv_rows, t4_ref[tidx], o4.at[r], lse4.at[r], q_rows)

    for b in range(step):
        for r in range(d4):
            rows = pl.ds(b, BQ, stride=step)
            block(qs.at[r], ks.at[r], vs.at[r], rows, rows, t16_ref[0], o16g.at[r], lse16g.at[r],
                  pl.ds(b * BQ, BQ))

    def merge_class(r, carry):
        for grouped, ordered in ((o16g, o16), (lse16g, lse16)):
            for b in range(step):
                ordered[r, pl.ds(b, BQ, stride=step), :] = grouped[r, pl.ds(b * BQ, BQ), :]
        for i in range(nb4):
            rows = pl.ds(i * BQ, BQ)
            nat = pl.ds(r + d4 * i * BQ, BQ, stride=d4)
            la, lb, lc = lse1[nat, :], lse4[r, rows, :], lse16[r, rows, :]
            mx = jnp.maximum(jnp.maximum(la, lb), lc)
            wa, wb, wc = jnp.exp2(la - mx), jnp.exp2(lb - mx), jnp.exp2(lc - mx)
            num = wa * o1[nat, :] + wb * o4[r, rows, :] + wc * o16[r, rows, :]
            outc[r, rows, :] = num / (wa + wb + wc)
        outf[pl.ds(r, sub4, stride=d4), :] = outc[r]
        return carry

    lax.fori_loop(0, d4, merge_class, 0)
    out_ref[...] = outf[...].astype(out_ref.dtype)


def _attn_a_call(qa, ka, va, t1, t4, t16, batch, seq):
    n_slab, t, _ = qa.shape
    assert seq // DILATED_PATTERNS[2][1] == BQ
    qkv_spec = pl.BlockSpec((None, seq, LANES), lambda hp, b: (hp, b, 0))
    tab_spec = lambda tab: pl.BlockSpec((None,) + tab.shape[1:], lambda hp, b: (hp, 0, 0, 0))
    d4 = DILATED_PATTERNS[1][1]
    full = pltpu.VMEM((seq, LANES), F32)
    by_class = pltpu.VMEM((d4, seq // d4, LANES), F32)
    return pl.pallas_call(
        _attn_a_kernel,
        grid=(n_slab, batch),
        in_specs=[qkv_spec, qkv_spec, qkv_spec, tab_spec(t1), tab_spec(t4), tab_spec(t16)],
        out_specs=pl.BlockSpec((seq, LANES), lambda hp, b: (b, hp)),
        out_shape=jax.ShapeDtypeStruct((t, n_slab * LANES), BF16),
        scratch_shapes=[full] * 3 + [by_class] * 10,
        compiler_params=pltpu.CompilerParams(dimension_semantics=("arbitrary", "arbitrary"),
                                             vmem_limit_bytes=VMEM_LIMIT),
        name="attn_a",
    )(qa, ka, va, t1, t4, t16)


def _attn_b_kernel(q_ref, k_ref, v_ref, tab_ref, sink_ref, o_ref):
    lo = _low_lanes()
    seq = q_ref.shape[0]
    win = tab_ref.shape[-1]
    nb = seq // BQ
    sink = sink_ref[...]

    def body(i, carry):
        q0, w0, tidx = _block_geometry(i, nb, seq, win)
        o2, m, l = _attn_block(q_ref[pl.ds(q0, BQ), :], k_ref[pl.ds(w0, win), :], v_ref[pl.ds(w0, win), :],
                               tab_ref[tidx], lo)
        o, m, l = _merge_heads(o2, lo), _merge_heads(m, lo), _merge_heads(l, lo)
        m_new = jnp.maximum(m, sink)
        scale = jnp.exp2(m - m_new)
        den = l * scale + jnp.exp2(sink - m_new)
        o_ref[pl.ds(q0, BQ), :] = (o * (scale / den)).astype(o_ref.dtype)
        return carry

    lax.fori_loop(0, nb, body, 0, unroll=BLOCK_UNROLL)


def _attn_b_call(qb, kb, vb, tab, sink, batch, seq):
    n_slab, t, _ = qb.shape
    group = n_slab // kb.shape[0]
    q_spec = pl.BlockSpec((None, seq, LANES), lambda hp, b: (hp, b, 0))
    kv_spec = pl.BlockSpec((None, seq, LANES), lambda hp, b: (hp // group, b, 0))
    return pl.pallas_call(
        _attn_b_kernel,
        grid=(n_slab, batch),
        in_specs=[q_spec, kv_spec, kv_spec,
                  pl.BlockSpec((None,) + tab.shape[1:], lambda hp, b: (hp, 0, 0, 0)),
                  pl.BlockSpec((None,) + sink.shape[1:], lambda hp, b: (hp, 0, 0))],
        out_specs=pl.BlockSpec((seq, LANES), lambda hp, b: (b, hp)),
        out_shape=jax.ShapeDtypeStruct((t, n_slab * LANES), BF16),
        compiler_params=pltpu.CompilerParams(dimension_semantics=("arbitrary", "arbitrary"),
                                             vmem_limit_bytes=VMEM_LIMIT),
        name="attn_b",
    )(qb, kb, vb, tab, sink)


def _row_call(kernel, name, row_inputs, layer_inputs, layer, out_dim, scratch):
    t = row_inputs[0].shape[0]

    def row_spec(a):
        if a.ndim == 2:
            return pl.BlockSpec((TM, a.shape[1]), lambda i: (i, 0))
        return pl.BlockSpec((None, TM, a.shape[2]), lambda i: (layer, i, 0))

    return pl.pallas_call(
        kernel,
        grid=(t // TM,),
        in_specs=[row_spec(a) for a in row_inputs] + [_layer_spec(a, layer) for a in layer_inputs],
        out_specs=pl.BlockSpec((TM, out_dim), lambda i: (i, 0)),
        out_shape=jax.ShapeDtypeStruct((t, out_dim), F32),
        scratch_shapes=scratch,
        compiler_params=pltpu.CompilerParams(dimension_semantics=("arbitrary",),
                                             vmem_limit_bytes=VMEM_LIMIT),
        name=name,
    )(*row_inputs, *layer_inputs)


def _post_kernel(x_ref, ya_ref, yb_ref, p_ref, gmix_ref, gffn_ref, gple_ref, wgates_ref, wa_ref, wb_ref, wo_ref,
                 wgate_ref, wup_ref, wdown_ref, wpg_ref, wpp_ref, o_ref, merged, hid, xs):
    d = x_ref.shape[1]
    chunks = [slice(c * MXU_COLS, (c + 1) * MXU_COLS) for c in range(d // MXU_COLS)]

    h = _rms(x_ref[...], gmix_ref[...]).astype(BF16)
    ya = ya_ref[...]
    yb = yb_ref[...]
    for c, cs in enumerate(chunks):
        ga = _dot(h, wgates_ref[:, cs])
        gb = _dot(h, wgates_ref[:, d + c * MXU_COLS:d + (c + 1) * MXU_COLS])
        pa = _dot(ya, wa_ref[:, cs])
        pb = _dot(yb, wb_ref[:, cs])
        merged[:, cs] = (jax.nn.sigmoid(ga) * pa + jax.nn.sigmoid(gb) * pb).astype(BF16)
    mg = merged[...]
    for cs in chunks:
        xs[:, cs] = x_ref[:, cs] + _dot(mg, wo_ref[:, cs])

    h = _rms(xs[...], gffn_ref[...]).astype(BF16)
    for c in range(wgate_ref.shape[1] // MXU_COLS):
        cs = slice(c * MXU_COLS, (c + 1) * MXU_COLS)
        a = _dot(h, wgate_ref[:, cs])
        u = _dot(h, wup_ref[:, cs])
        hid[:, cs] = (a * jax.nn.sigmoid(a) * u).astype(BF16)
    hv = hid[...]
    for cs in chunks:
        xs[:, cs] = xs[:, cs] + _dot(hv, wdown_ref[:, cs])

    h = _rms(xs[...], gple_ref[...]).astype(BF16)
    pv = p_ref[...].astype(BF16)
    for cs in chunks:
        o_ref[:, cs] = xs[:, cs] + jax.nn.sigmoid(_dot(h, wpg_ref[:, cs])) * _dot(pv, wpp_ref[:, cs])


def _t5_bucket(rel):
    half_b = NUM_BUCKETS // 2
    max_exact = half_b // 2
    sign = jnp.where(rel > 0, half_b, 0)
    n = jnp.abs(rel)
    nf = jnp.maximum(n, 1).astype(F32)
    large = max_exact + (jnp.log(nf / max_exact) / math.log(MAX_DISTANCE / max_exact)
                         * (half_b - max_exact)).astype(jnp.int32)
    large = jnp.minimum(large, half_b - 1)
    return sign + jnp.where(n < max_exact, n, large)


def _band_tables(table, seq_len, win, half, dilation):
    nb = seq_len // BQ
    blocks = (0, min(1, nb - 1), nb - 1) if nb > 1 else (0,)
    starts = [max(0, min(i * BQ - (win - BQ) // 2, seq_len - win)) - i * BQ for i in blocks]
    n = BQ + win - 1
    tabs = []
    for s in starts:
        rel = s - (BQ - 1) + jnp.arange(n, dtype=jnp.int32)
        bias = table[_t5_bucket(rel * dilation)].astype(F32) * LOG2_E
        vals = jnp.where((jnp.abs(rel) <= half)[:, None], bias, NEG_INF)
        vpad = jnp.pad(vals.T, ((0, 0), (0, 1)))
        skew = jnp.tile(vpad, (1, BQ))[:, :BQ * n].reshape(-1, BQ, n)
        tabs.append(skew[:, :, BQ - 1:BQ - 1 + win])
    t = jnp.stack(tabs, axis=1)
    h = t.shape[0]
    return t.reshape(h // 2, 2, len(starts), BQ, win).transpose(0, 2, 1, 3, 4).reshape(
        h // 2, len(starts), 2 * BQ, win)


def _pair_gain(g, scale=1.0):
    return jnp.tile(g.astype(F32) * scale, 2)


def kernel(x, p, rel_table, norm_mix_g, w_in, qnorm_a_g, knorm_a_g, qnorm_b_g, knorm_b_g, sink_b,
           w_branch_a, w_branch_b, w_out, norm_ffn_g, w_ffn_gate, w_ffn_up, w_ffn_down,
           norm_ple_g, w_ple_gate, w_ple_proj):
    batch, seq, d = x.shape
    depth = p.shape[0]
    t = batch * seq
    q_scale = HEAD_DIM ** -0.5 * LOG2_E

    table_a = rel_table[:, :N_HEADS_A]
    table_b = rel_table[:, N_HEADS_A:]
    tabs_a = []
    for w, dil in DILATED_PATTERNS:
        sub = seq // dil
        tabs_a.append(_band_tables(table_a, sub, min(2 * BQ, sub), w // (2 * dil), dil))
    tab_b = _band_tables(table_b, seq, BQ + 2 * WINDOW_B, WINDOW_B, 1)

    bf = lambda w: w.astype(BF16)
    rows = lambda g: g.astype(F32).reshape(depth, 1, -1)
    w_qkv_b, w_gates_b = bf(w_in[:, :, :GATE_COL]), bf(w_in[:, :, GATE_COL:])
    w_a_b, w_b_b, w_o_b = bf(w_branch_a), bf(w_branch_b), bf(w_out)
    w_gate_b, w_up_b, w_down_b = bf(w_ffn_gate), bf(w_ffn_up), bf(w_ffn_down)
    w_pg_b, w_pp_b = bf(w_ple_gate), bf(w_ple_proj)
    g_mix, g_ffn, g_ple = rows(norm_mix_g), rows(norm_ffn_g), rows(norm_ple_g)
    p2 = p.reshape(depth, t, -1)

    x2 = x.reshape(t, d)
    for l in range(depth):
        gains = jnp.stack([_pair_gain(qnorm_a_g[l], q_scale), _pair_gain(knorm_a_g[l]),
                           _pair_gain(qnorm_b_g[l], q_scale), _pair_gain(knorm_b_g[l])])
        qa, ka, va, qb, kb, vb = _qkv_call(x2, g_mix, w_qkv_b, gains, l)
        ya = _attn_a_call(qa, ka, va, *tabs_a, batch, seq)
        sink = jnp.repeat(sink_b[l].astype(F32) * LOG2_E, HEAD_DIM).reshape(N_HEADS_B // 2, 1, LANES)
        yb = _attn_b_call(qb, kb, vb, tab_b, sink, batch, seq)
        x2 = _row_call(_post_kernel, "post", [x2, ya, yb, p2],
                       [g_mix, g_ffn, g_ple, w_gates_b, w_a_b, w_b_b, w_o_b, w_gate_b, w_up_b, w_down_b,
                        w_pg_b, w_pp_b], l, d,
                       [pltpu.VMEM((TM, d), BF16), pltpu.VMEM((TM, w_ffn_gate.shape[2]), BF16),
                        pltpu.VMEM((TM, d), F32)])
    return x2.reshape(batch, seq, d)
```

```python
import math

import jax
import jax.numpy as jnp
from jax import lax
from jax.experimental import pallas as pl
from jax.experimental.pallas import tpu as pltpu

F32 = jnp.float32
BF16 = jnp.bfloat16

HEAD_DIM = 64
N_HEADS_A = 8
N_HEADS_B = 8
N_KV_B = 2
DILATED_PATTERNS = ((128, 1), (512, 4), (2048, 16))
WINDOW_B = 128
NUM_BUCKETS = 32
MAX_DISTANCE = 1024
RMS_EPS = 1e-6
NEG_INF = -1e30
LOG2_E = math.log2(math.e)

WIDTH_A = N_HEADS_A * HEAD_DIM
WIDTH_BQ = N_HEADS_B * HEAD_DIM
WIDTH_BKV = N_KV_B * HEAD_DIM
GATE_COL = 3 * WIDTH_A + WIDTH_BQ + 2 * WIDTH_BKV

LANES = 128
MXU_COLS = 256
BQ = 128
TM = 512
VMEM_LIMIT = 56 * 1024 * 1024


def _dot(a, b):
    return jnp.dot(a, b, preferred_element_type=F32)


def _rms(x, g):
    ms = jnp.mean(x * x, axis=-1, keepdims=True)
    return x * lax.rsqrt(ms + RMS_EPS) * g


def _low_lanes():
    return lax.broadcasted_iota(jnp.int32, (1, LANES), 1) < HEAD_DIM


def _head_rms(y, gain, lo):
    sq = y * y
    s0 = jnp.sum(jnp.where(lo, sq, 0.0), axis=-1, keepdims=True)
    s1 = jnp.sum(jnp.where(lo, 0.0, sq), axis=-1, keepdims=True)
    ms = jnp.where(lo, s0, s1) * (1.0 / HEAD_DIM)
    return y * lax.rsqrt(ms + RMS_EPS) * gain


def _const_spec(shape):
    nd = len(shape)
    return pl.BlockSpec(shape, lambda *_: (0,) * nd, pipeline_mode=pl.Buffered(1))


def _layer_spec(stacked, layer):
    nd = stacked.ndim - 1
    return pl.BlockSpec((None,) + stacked.shape[1:], lambda *_: (layer,) + (0,) * nd,
                        pipeline_mode=pl.Buffered(1))


def _qkv_kernel(x_ref, g_ref, w_ref, gains_ref, qa_ref, ka_ref, va_ref, qb_ref, kb_ref, vb_ref):
    lo = _low_lanes()
    h = _rms(x_ref[...], g_ref[...]).astype(BF16)
    n_pair = WIDTH_A // MXU_COLS
    order = sorted(range(GATE_COL // MXU_COLS), key=lambda c: 2 * n_pair <= c < 3 * n_pair)
    for c in order:
        y = _dot(h, w_ref[:, c * MXU_COLS:(c + 1) * MXU_COLS])
        y0, y1 = y[:, :LANES], y[:, LANES:]
        if c < n_pair:
            qa_ref[2 * c] = _head_rms(y0, gains_ref[0:1, :], lo)
            qa_ref[2 * c + 1] = _head_rms(y1, gains_ref[0:1, :], lo)
        elif c < 2 * n_pair:
            ka_ref[2 * (c - n_pair)] = _head_rms(y0, gains_ref[1:2, :], lo)
            ka_ref[2 * (c - n_pair) + 1] = _head_rms(y1, gains_ref[1:2, :], lo)
        elif c < 3 * n_pair:
            va_ref[2 * (c - 2 * n_pair)] = y0
            va_ref[2 * (c - 2 * n_pair) + 1] = y1
        elif c < 4 * n_pair:
            qb_ref[2 * (c - 3 * n_pair)] = _head_rms(y0, gains_ref[2:3, :], lo).astype(BF16)
            qb_ref[2 * (c - 3 * n_pair) + 1] = _head_rms(y1, gains_ref[2:3, :], lo).astype(BF16)
        else:
            kn = _head_rms(y0, gains_ref[3:4, :], lo)
            ks = pltpu.roll(kn, HEAD_DIM, axis=1)
            vs = pltpu.roll(y1, HEAD_DIM, axis=1)
            kb_ref[0] = jnp.where(lo, kn, ks).astype(BF16)
            kb_ref[1] = jnp.where(lo, ks, kn).astype(BF16)
            vb_ref[0] = jnp.where(lo, y1, vs).astype(BF16)
            vb_ref[1] = jnp.where(lo, vs, y1).astype(BF16)


def _qkv_call(x2, g, w_in, gains, layer):
    t, d = x2.shape
    n_q = WIDTH_A // LANES
    n_kv = N_KV_B
    assert WIDTH_BQ == WIDTH_A and 2 * WIDTH_BKV == MXU_COLS
    slab = lambda n, dt: jax.ShapeDtypeStruct((n, t, LANES), dt)
    slab_spec = lambda n: pl.BlockSpec((n, TM, LANES), lambda i: (0, i, 0))
    return pl.pallas_call(
        _qkv_kernel,
        grid=(t // TM,),
        in_specs=[pl.BlockSpec((TM, d), lambda i: (i, 0)), _layer_spec(g, layer),
                  _layer_spec(w_in, layer), _const_spec(gains.shape)],
        out_specs=[slab_spec(n_q), slab_spec(n_q), slab_spec(n_q), slab_spec(n_q),
                   slab_spec(n_kv), slab_spec(n_kv)],
        out_shape=[slab(n_q, F32), slab(n_q, F32), slab(n_q, F32), slab(n_q, BF16),
                   slab(n_kv, BF16), slab(n_kv, BF16)],
        compiler_params=pltpu.CompilerParams(dimension_semantics=("arbitrary",),
                                             vmem_limit_bytes=VMEM_LIMIT),
        name="qkv",
    )(x2, g, w_in, gains)


def _attn_block(q, kw, vw, tab, lo):
    zero = jnp.zeros_like(q)
    q2 = jnp.concatenate([jnp.where(lo, q, zero), jnp.where(lo, zero, q)], axis=0).astype(BF16)
    s = lax.dot_general(q2, kw.astype(BF16), (((1,), (1,)), ((), ())),
                        preferred_element_type=F32) + tab
    m = jnp.max(s, axis=-1, keepdims=True)
    p = jnp.exp2(s - m)
    vw1 = jnp.concatenate([vw.astype(BF16), jnp.ones(vw.shape, BF16)], axis=1)
    o2 = _dot(p.astype(BF16), vw1)
    return o2[:, :LANES], m, o2[:, LANES:]


def _merge_heads(a2, lo):
    bq = a2.shape[0] // 2
    return jnp.where(lo, a2[:bq], a2[bq:])


def _block_geometry(i, n_blocks, seq_len, win):
    w0 = max(0, min(i * BQ - (win - BQ) // 2, seq_len - win))
    return pl.ds(i * BQ, BQ), pl.ds(w0, win), (0 if i == 0 else (2 if i == n_blocks - 1 else 1))


def _attn_a_kernel(q_ref, k_ref, v_ref, t1_ref, t4_ref, t16_ref, out_ref,
                   o1, lse1, outf, qs, ks, vs, o4, lse4, o16g, lse16g, o16, lse16, outc):
    lo = _low_lanes()
    seq = q_ref.shape[0]
    win = t1_ref.shape[-1]
    d4 = DILATED_PATTERNS[1][1]
    step = DILATED_PATTERNS[2][1] // d4
    sub4 = seq // d4

    def block(qr, kr, vr, q_rows, kv_rows, tab, o_out, lse_out, out_rows):
        o2, m, l = _attn_block(qr[q_rows, :], kr[kv_rows, :], vr[kv_rows, :], tab, lo)
        l = _merge_heads(l, lo)
        o_out[out_rows, :] = _merge_heads(o2, lo) / l
        lse_out[out_rows, :] = _merge_heads(m, lo) + jnp.log2(l)

    for r in range(d4):
        res = pl.ds(r, sub4, stride=d4)
        qs[r] = q_ref[res, :]
        ks[r] = k_ref[res, :]
        vs[r] = v_ref[res, :]

    nb = seq // BQ
    for i in range(nb):
        q_rows, kv_rows, tidx = _block_geometry(i, nb, seq, win)
        block(q_ref, k_ref, v_ref, q_rows, kv_rows, t1_ref[tidx], o1, lse1, q_rows)

    nb4 = sub4 // BQ
    for i in range(nb4):
        for r in range(d4):
            q_rows, kv_rows, tidx = _block_geometry(i, nb4, sub4, win)
            block(qs.at[r], ks.at[r], vs.at[r], q_rows, kv_rows, t4_ref[tidx], o4.at[r], lse4.at[r], q_rows)

    for b in range(step):
        for r in range(d4):
            rows = pl.ds(b, BQ, stride=step)
            block(qs.at[r], ks.at[r], vs.at[r], rows, rows, t16_ref[0], o16g.at[r], lse16g.at[r],
                  pl.ds(b * BQ, BQ))

    def merge_class(r, carry):
        for grouped, ordered in ((o16g, o16), (lse16g, lse16)):
            for b in range(step):
                ordered[r, pl.ds(b, BQ, stride=step), :] = grouped[r, pl.ds(b * BQ, BQ), :]
        for i in range(nb4):
            rows = pl.ds(i * BQ, BQ)
            nat = pl.ds(r + d4 * i * BQ, BQ, stride=d4)
            la, lb, lc = lse1[nat, :], lse4[r, rows, :], lse16[r, rows, :]
            mx = jnp.maximum(jnp.maximum(la, lb), lc)
            wa, wb, wc = jnp.exp2(la - mx), jnp.exp2(lb - mx), jnp.exp2(lc - mx)
            num = wa * o1[nat, :] + wb * o4[r, rows, :] + wc * o16[r, rows, :]
            outc[r, rows, :] = num / (wa + wb + wc)
        outf[pl.ds(r, sub4, stride=d4), :] = outc[r]
        return carry

    lax.fori_loop(0, d4, merge_class, 0)
    out_ref[...] = outf[...].astype(out_ref.dtype)


def _attn_a_call(qa, ka, va, t1, t4, t16, batch, seq):
    n_slab, t, _ = qa.shape
    assert seq // DILATED_PATTERNS[2][1] == BQ
    qkv_spec = pl.BlockSpec((None, seq, LANES), lambda hp, b: (hp, b, 0))
    tab_spec = lambda tab: pl.BlockSpec((None,) + tab.shape[1:], lambda hp, b: (hp, 0, 0, 0))
    d4 = DILATED_PATTERNS[1][1]
    full = pltpu.VMEM((seq, LANES), F32)
    by_class = pltpu.VMEM((d4, seq // d4, LANES), F32)
    return pl.pallas_call(
        _attn_a_kernel,
        grid=(n_slab, batch),
        in_specs=[qkv_spec, qkv_spec, qkv_spec, tab_spec(t1), tab_spec(t4), tab_spec(t16)],
        out_specs=pl.BlockSpec((seq, LANES), lambda hp, b: (b, hp)),
        out_shape=jax.ShapeDtypeStruct((t, n_slab * LANES), BF16),
        scratch_shapes=[full] * 3 + [by_class] * 10,
        compiler_params=pltpu.CompilerParams(dimension_semantics=("arbitrary", "arbitrary"),
                                             vmem_limit_bytes=VMEM_LIMIT),
        name="attn_a",
    )(qa, ka, va, t1, t4, t16)


def _attn_b_kernel(q_ref, k_ref, v_ref, tab_ref, sink_ref, o_ref):
    lo = _low_lanes()
    n_slab, seq, _ = q_ref.shape
    win = tab_ref.shape[-1]
    nb = seq // BQ
    zero = jnp.zeros((BQ, LANES), q_ref.dtype)

    for i in range(nb):
        q_rows, kv_rows, tidx = _block_geometry(i, nb, seq, win)
        q = [q_ref[sl, q_rows, :] for sl in range(n_slab)]
        q_all = jnp.concatenate([part for qs in q for part in (jnp.where(lo, qs, zero), jnp.where(lo, zero, qs))],
                                axis=0)
        tab = jnp.concatenate([tab_ref[sl, tidx] for sl in range(n_slab)], axis=0)
        s = lax.dot_general(q_all, k_ref[kv_rows, :], (((1,), (1,)), ((), ())), preferred_element_type=F32) + tab
        m = jnp.max(s, axis=-1, keepdims=True)
        p = jnp.exp2(s - m)
        vw = v_ref[kv_rows, :]
        o2 = _dot(p.astype(BF16), jnp.concatenate([vw, jnp.ones(vw.shape, BF16)], axis=1))
        for sl in range(n_slab):
            rows = slice(sl * 2 * BQ, (sl + 1) * 2 * BQ)
            o = _merge_heads(o2[rows, :LANES], lo)
            mm = _merge_heads(m[rows], lo)
            l = _merge_heads(o2[rows, LANES:], lo)
            sink = sink_ref[sl]
            m_new = jnp.maximum(mm, sink)
            scale = jnp.exp2(mm - m_new)
            den = l * scale + jnp.exp2(sink - m_new)
            o_ref[q_rows, sl * LANES:(sl + 1) * LANES] = (o * (scale / den)).astype(o_ref.dtype)


def _attn_b_call(qb, kb, vb, tab, sink, batch, seq):
    n_slab, t, _ = qb.shape
    n_kv = kb.shape[0]
    group = n_slab // n_kv
    kv_spec = pl.BlockSpec((None, seq, LANES), lambda kv, b: (kv, b, 0))
    return pl.pallas_call(
        _attn_b_kernel,
        grid=(n_kv, batch),
        in_specs=[pl.BlockSpec((group, seq, LANES), lambda kv, b: (kv, b, 0)), kv_spec, kv_spec,
                  pl.BlockSpec((group,) + tab.shape[1:], lambda kv, b: (kv, 0, 0, 0)),
                  pl.BlockSpec((group,) + sink.shape[1:], lambda kv, b: (kv, 0, 0))],
        out_specs=pl.BlockSpec((seq, group * LANES), lambda kv, b: (b, kv)),
        out_shape=jax.ShapeDtypeStruct((t, n_slab * LANES), BF16),
        compiler_params=pltpu.CompilerParams(dimension_semantics=("arbitrary", "arbitrary"),
                                             vmem_limit_bytes=VMEM_LIMIT),
        name="attn_b",
    )(qb, kb, vb, tab, sink)


def _row_call(kernel, name, row_inputs, layer_inputs, layer, out_dim, scratch):
    t = row_inputs[0].shape[0]

    def row_spec(a):
        if a.ndim == 2:
            return pl.BlockSpec((TM, a.shape[1]), lambda i: (i, 0))
        return pl.BlockSpec((None, TM, a.shape[2]), lambda i: (layer, i, 0))

    return pl.pallas_call(
        kernel,
        grid=(t // TM,),
        in_specs=[row_spec(a) for a in row_inputs] + [_layer_spec(a, layer) for a in layer_inputs],
        out_specs=pl.BlockSpec((TM, out_dim), lambda i: (i, 0)),
        out_shape=jax.ShapeDtypeStruct((t, out_dim), F32),
        scratch_shapes=scratch,
        compiler_params=pltpu.CompilerParams(dimension_semantics=("arbitrary",),
                                             vmem_limit_bytes=VMEM_LIMIT),
        name=name,
    )(*row_inputs, *layer_inputs)


def _post_kernel(x_ref, ya_ref, yb_ref, p_ref, gmix_ref, gffn_ref, gple_ref, wgates_ref, wa_ref, wb_ref, wo_ref,
                 wgate_ref, wup_ref, wdown_ref, wpg_ref, wpp_ref, o_ref, merged, hid, xs):
    d = x_ref.shape[1]
    chunks = [slice(c * MXU_COLS, (c + 1) * MXU_COLS) for c in range(d // MXU_COLS)]

    h = _rms(x_ref[...], gmix_ref[...]).astype(BF16)
    ya = ya_ref[...]
    yb = yb_ref[...]
    for c, cs in enumerate(chunks):
        ga = _dot(h, wgates_ref[:, cs])
        gb = _dot(h, wgates_ref[:, d + c * MXU_COLS:d + (c + 1) * MXU_COLS])
        pa = _dot(ya, wa_ref[:, cs])
        pb = _dot(yb, wb_ref[:, cs])
        merged[:, cs] = (jax.nn.sigmoid(ga) * pa + jax.nn.sigmoid(gb) * pb).astype(BF16)
    mg = merged[...]
    for cs in chunks:
        xs[:, cs] = x_ref[:, cs] + _dot(mg, wo_ref[:, cs])

    h = _rms(xs[...], gffn_ref[...]).astype(BF16)
    for c in range(wgate_ref.shape[1] // MXU_COLS):
        cs = slice(c * MXU_COLS, (c + 1) * MXU_COLS)
        a = _dot(h, wgate_ref[:, cs])
        u = _dot(h, wup_ref[:, cs])
        hid[:, cs] = (a * jax.nn.sigmoid(a) * u).astype(BF16)
    hv = hid[...]
    for cs in chunks:
        xs[:, cs] = xs[:, cs] + _dot(hv, wdown_ref[:, cs])

    h = _rms(xs[...], gple_ref[...]).astype(BF16)
    pv = p_ref[...].astype(BF16)
    for cs in chunks:
        o_ref[:, cs] = xs[:, cs] + jax.nn.sigmoid(_dot(h, wpg_ref[:, cs])) * _dot(pv, wpp_ref[:, cs])


def _t5_bucket(rel):
    half_b = NUM_BUCKETS // 2
    max_exact = half_b // 2
    sign = jnp.where(rel > 0, half_b, 0)
    n = jnp.abs(rel)
    nf = jnp.maximum(n, 1).astype(F32)
    large = max_exact + (jnp.log(nf / max_exact) / math.log(MAX_DISTANCE / max_exact)
                         * (half_b - max_exact)).astype(jnp.int32)
    large = jnp.minimum(large, half_b - 1)
    return sign + jnp.where(n < max_exact, n, large)


def _band_tables(table, seq_len, win, half, dilation):
    nb = seq_len // BQ
    blocks = (0, min(1, nb - 1), nb - 1) if nb > 1 else (0,)
    starts = [max(0, min(i * BQ - (win - BQ) // 2, seq_len - win)) - i * BQ for i in blocks]
    n = BQ + win - 1
    tabs = []
    for s in starts:
        rel = s - (BQ - 1) + jnp.arange(n, dtype=jnp.int32)
        bias = table[_t5_bucket(rel * dilation)].astype(F32) * LOG2_E
        vals = jnp.where((jnp.abs(rel) <= half)[:, None], bias, NEG_INF)
        vpad = jnp.pad(vals.T, ((0, 0), (0, 1)))
        skew = jnp.tile(vpad, (1, BQ))[:, :BQ * n].reshape(-1, BQ, n)
        tabs.append(skew[:, :, BQ - 1:BQ - 1 + win])
    t = jnp.stack(tabs, axis=1)
    h = t.shape[0]
    return t.reshape(h // 2, 2, len(starts), BQ, win).transpose(0, 2, 1, 3, 4).reshape(
        h // 2, len(starts), 2 * BQ, win)


def _pair_gain(g, scale=1.0):
    return jnp.tile(g.astype(F32) * scale, 2)


def kernel(x, p, rel_table, norm_mix_g, w_in, qnorm_a_g, knorm_a_g, qnorm_b_g, knorm_b_g, sink_b,
           w_branch_a, w_branch_b, w_out, norm_ffn_g, w_ffn_gate, w_ffn_up, w_ffn_down,
           norm_ple_g, w_ple_gate, w_ple_proj):
    batch, seq, d = x.shape
    depth = p.shape[0]
    t = batch * seq
    q_scale = HEAD_DIM ** -0.5 * LOG2_E

    table_a = rel_table[:, :N_HEADS_A]
    table_b = rel_table[:, N_HEADS_A:]
    tabs_a = []
    for w, dil in DILATED_PATTERNS:
        sub = seq // dil
        tabs_a.append(_band_tables(table_a, sub, min(2 * BQ, sub), w // (2 * dil), dil))
    tab_b = _band_tables(table_b, seq, BQ + 2 * WINDOW_B, WINDOW_B, 1)

    bf = lambda w: w.astype(BF16)
    rows = lambda g: g.astype(F32).reshape(depth, 1, -1)
    w_qkv_b, w_gates_b = bf(w_in[:, :, :GATE_COL]), bf(w_in[:, :, GATE_COL:])
    w_a_b, w_b_b, w_o_b = bf(w_branch_a), bf(w_branch_b), bf(w_out)
    w_gate_b, w_up_b, w_down_b = bf(w_ffn_gate), bf(w_ffn_up), bf(w_ffn_down)
    w_pg_b, w_pp_b = bf(w_ple_gate), bf(w_ple_proj)
    g_mix, g_ffn, g_ple = rows(norm_mix_g), rows(norm_ffn_g), rows(norm_ple_g)
    p2 = p.reshape(depth, t, -1)

    x2 = x.reshape(t, d)
    for l in range(depth):
        gains = jnp.stack([_pair_gain(qnorm_a_g[l], q_scale), _pair_gain(knorm_a_g[l]),
                           _pair_gain(qnorm_b_g[l], q_scale), _pair_gain(knorm_b_g[l])])
        qa, ka, va, qb, kb, vb = _qkv_call(x2, g_mix, w_qkv_b, gains, l)
        ya = _attn_a_call(qa, ka, va, *tabs_a, batch, seq)
        sink = jnp.repeat(sink_b[l].astype(F32) * LOG2_E, HEAD_DIM).reshape(N_HEADS_B // 2, 1, LANES)
        yb = _attn_b_call(qb, kb, vb, tab_b, sink, batch, seq)
        x2 = _row_call(_post_kernel, "post", [x2, ya, yb, p2],
                       [g_mix, g_ffn, g_ple, w_gates_b, w_a_b, w_b_b, w_o_b, w_gate_b, w_up_b, w_down_b,
                        w_pg_b, w_pp_b], l, d,
                       [pltpu.VMEM((TM, d), BF16), pltpu.VMEM((TM, w_ffn_gate.shape[2]), BF16),
                        pltpu.VMEM((TM, d), F32)])
    return x2.reshape(batch, seq, d)
```

```python
import math

import jax
import jax.numpy as jnp
from jax import lax
from jax.experimental import pallas as pl
from jax.experimental.pallas import tpu as pltpu

F32 = jnp.float32
BF16 = jnp.bfloat16

HEAD_DIM = 64
N_HEADS_A = 8
N_HEADS_B = 8
N_KV_B = 2
DILATED_PATTERNS = ((128, 1), (512, 4), (2048, 16))
WINDOW_B = 128
NUM_BUCKETS = 32
MAX_DISTANCE = 1024
RMS_EPS = 1e-6
NEG_INF = -1e30
LOG2_E = math.log2(math.e)

WIDTH_A = N_HEADS_A * HEAD_DIM
WIDTH_BQ = N_HEADS_B * HEAD_DIM
WIDTH_BKV = N_KV_B * HEAD_DIM
GATE_COL = 3 * WIDTH_A + WIDTH_BQ + 2 * WIDTH_BKV

LANES = 128
MXU_COLS = 256
BQ = 128
TM = 512
VMEM_LIMIT = 56 * 1024 * 1024


def _dot(a, b):
    return jnp.dot(a, b, preferred_element_type=F32)


def _rms(x, g):
    ms = jnp.mean(x * x, axis=-1, keepdims=True)
    return x * lax.rsqrt(ms + RMS_EPS) * g


def _low_lanes():
    return lax.broadcasted_iota(jnp.int32, (1, LANES), 1) < HEAD_DIM


def _head_rms(y, gain, lo):
    sq = y * y
    s0 = jnp.sum(jnp.where(lo, sq, 0.0), axis=-1, keepdims=True)
    s1 = jnp.sum(jnp.where(lo, 0.0, sq), axis=-1, keepdims=True)
    ms = jnp.where(lo, s0, s1) * (1.0 / HEAD_DIM)
    return y * lax.rsqrt(ms + RMS_EPS) * gain


def _const_spec(shape):
    nd = len(shape)
    return pl.BlockSpec(shape, lambda *_: (0,) * nd, pipeline_mode=pl.Buffered(1))


def _layer_spec(stacked, layer):
    nd = stacked.ndim - 1
    return pl.BlockSpec((None,) + stacked.shape[1:], lambda *_: (layer,) + (0,) * nd,
                        pipeline_mode=pl.Buffered(1))


def _qkv_kernel(x_ref, g_ref, w_ref, gains_ref, qa_ref, ka_ref, va_ref, qb_ref, kb_ref, vb_ref):
    lo = _low_lanes()
    h = _rms(x_ref[...], g_ref[...]).astype(BF16)
    n_pair = WIDTH_A // MXU_COLS
    order = sorted(range(GATE_COL // MXU_COLS), key=lambda c: 2 * n_pair <= c < 3 * n_pair)
    for c in order:
        y = _dot(h, w_ref[:, c * MXU_COLS:(c + 1) * MXU_COLS])
        y0, y1 = y[:, :LANES], y[:, LANES:]
        if c < n_pair:
            qa_ref[2 * c] = _head_rms(y0, gains_ref[0:1, :], lo)
            qa_ref[2 * c + 1] = _head_rms(y1, gains_ref[0:1, :], lo)
        elif c < 2 * n_pair:
            ka_ref[2 * (c - n_pair)] = _head_rms(y0, gains_ref[1:2, :], lo)
            ka_ref[2 * (c - n_pair) + 1] = _head_rms(y1, gains_ref[1:2, :], lo)
        elif c < 3 * n_pair:
            va_ref[2 * (c - 2 * n_pair)] = y0
            va_ref[2 * (c - 2 * n_pair) + 1] = y1
        elif c < 4 * n_pair:
            qb_ref[2 * (c - 3 * n_pair)] = _head_rms(y0, gains_ref[2:3, :], lo).astype(BF16)
            qb_ref[2 * (c - 3 * n_pair) + 1] = _head_rms(y1, gains_ref[2:3, :], lo).astype(BF16)
        else:
            kn = _head_rms(y0, gains_ref[3:4, :], lo)
            ks = pltpu.roll(kn, HEAD_DIM, axis=1)
            vs = pltpu.roll(y1, HEAD_DIM, axis=1)
            kb_ref[0] = jnp.where(lo, kn, ks).astype(BF16)
            kb_ref[1] = jnp.where(lo, ks, kn).astype(BF16)
            vb_ref[0] = jnp.where(lo, y1, vs).astype(BF16)
            vb_ref[1] = jnp.where(lo, vs, y1).astype(BF16)


def _qkv_call(x2, g, w_in, gains, layer):
    t, d = x2.shape
    n_q = WIDTH_A // LANES
    n_kv = N_KV_B
    assert WIDTH_BQ == WIDTH_A and 2 * WIDTH_BKV == MXU_COLS
    slab = lambda n, dt: jax.ShapeDtypeStruct((n, t, LANES), dt)
    slab_spec = lambda n: pl.BlockSpec((n, TM, LANES), lambda i: (0, i, 0))
    return pl.pallas_call(
        _qkv_kernel,
        grid=(t // TM,),
        in_specs=[pl.BlockSpec((TM, d), lambda i: (i, 0)), _layer_spec(g, layer),
                  _layer_spec(w_in, layer), _const_spec(gains.shape)],
        out_specs=[slab_spec(n_q), slab_spec(n_q), slab_spec(n_q), slab_spec(n_q),
                   slab_spec(n_kv), slab_spec(n_kv)],
        out_shape=[slab(n_q, F32), slab(n_q, F32), slab(n_q, F32), slab(n_q, BF16),
                   slab(n_kv, BF16), slab(n_kv, BF16)],
        compiler_params=pltpu.CompilerParams(dimension_semantics=("arbitrary",),
                                             vmem_limit_bytes=VMEM_LIMIT),
        name="qkv",
    )(x2, g, w_in, gains)


def _attn_block(q, kw, vw, tab, lo):
    zero = jnp.zeros_like(q)
    q2 = jnp.concatenate([jnp.where(lo, q, zero), jnp.where(lo, zero, q)], axis=0).astype(BF16)
    s = lax.dot_general(q2, kw.astype(BF16), (((1,), (1,)), ((), ())),
                        preferred_element_type=F32) + tab
    m = jnp.max(s, axis=-1, keepdims=True)
    p = jnp.exp2(s - m)
    l = jnp.sum(p, axis=-1, keepdims=True)
    o2 = _dot(p.astype(BF16), vw.astype(BF16))
    return o2, m, l


def _merge_heads(a2, lo):
    bq = a2.shape[0] // 2
    return jnp.where(lo, a2[:bq], a2[bq:])


def _block_geometry(i, n_blocks, seq_len, win):
    w0 = max(0, min(i * BQ - (win - BQ) // 2, seq_len - win))
    return pl.ds(i * BQ, BQ), pl.ds(w0, win), (0 if i == 0 else (2 if i == n_blocks - 1 else 1))


def _attn_a_kernel(q_ref, k_ref, v_ref, t1_ref, t4_ref, t16_ref, out_ref,
                   o1, lse1, outf, qs, ks, vs, o4, lse4, o16g, lse16g, o16, lse16, outc):
    lo = _low_lanes()
    seq = q_ref.shape[0]
    win = t1_ref.shape[-1]
    d4 = DILATED_PATTERNS[1][1]
    step = DILATED_PATTERNS[2][1] // d4
    sub4 = seq // d4

    def block(qr, kr, vr, q_rows, kv_rows, tab, o_out, lse_out, out_rows):
        o2, m, l = _attn_block(qr[q_rows, :], kr[kv_rows, :], vr[kv_rows, :], tab, lo)
        l = _merge_heads(l, lo)
        o_out[out_rows, :] = _merge_heads(o2, lo) / l
        lse_out[out_rows, :] = _merge_heads(m, lo) + jnp.log2(l)

    for r in range(d4):
        res = pl.ds(r, sub4, stride=d4)
        qs[r] = q_ref[res, :]
        ks[r] = k_ref[res, :]
        vs[r] = v_ref[res, :]

    nb = seq // BQ
    for i in range(nb):
        q_rows, kv_rows, tidx = _block_geometry(i, nb, seq, win)
        block(q_ref, k_ref, v_ref, q_rows, kv_rows, t1_ref[tidx], o1, lse1, q_rows)

    nb4 = sub4 // BQ
    for i in range(nb4):
        for r in range(d4):
            q_rows, kv_rows, tidx = _block_geometry(i, nb4, sub4, win)
            block(qs.at[r], ks.at[r], vs.at[r], q_rows, kv_rows, t4_ref[tidx], o4.at[r], lse4.at[r], q_rows)

    for b in range(step):
        for r in range(d4):
            rows = pl.ds(b, BQ, stride=step)
            block(qs.at[r], ks.at[r], vs.at[r], rows, rows, t16_ref[0], o16g.at[r], lse16g.at[r],
                  pl.ds(b * BQ, BQ))

    def merge_class(r, carry):
        for grouped, ordered in ((o16g, o16), (lse16g, lse16)):
            for b in range(step):
                ordered[r, pl.ds(b, BQ, stride=step), :] = grouped[r, pl.ds(b * BQ, BQ), :]
        for i in range(nb4):
            rows = pl.ds(i * BQ, BQ)
            nat = pl.ds(r + d4 * i * BQ, BQ, stride=d4)
            la, lb, lc = lse1[nat, :], lse4[r, rows, :], lse16[r, rows, :]
            mx = jnp.maximum(jnp.maximum(la, lb), lc)
            wa, wb, wc = jnp.exp2(la - mx), jnp.exp2(lb - mx), jnp.exp2(lc - mx)
            num = wa * o1[nat, :] + wb * o4[r, rows, :] + wc * o16[r, rows, :]
            outc[r, rows, :] = num / (wa + wb + wc)
        outf[pl.ds(r, sub4, stride=d4), :] = outc[r]
        return carry

    lax.fori_loop(0, d4, merge_class, 0)
    out_ref[...] = outf[...].astype(out_ref.dtype)


def _attn_a_call(qa, ka, va, t1, t4, t16, batch, seq):
    n_slab, t, _ = qa.shape
    assert seq // DILATED_PATTERNS[2][1] == BQ
    qkv_spec = pl.BlockSpec((None, seq, LANES), lambda hp, b: (hp, b, 0))
    tab_spec = lambda tab: pl.BlockSpec((None,) + tab.shape[1:], lambda hp, b: (hp, 0, 0, 0))
    d4 = DILATED_PATTERNS[1][1]
    full = pltpu.VMEM((seq, LANES), F32)
    by_class = pltpu.VMEM((d4, seq // d4, LANES), F32)
    return pl.pallas_call(
        _attn_a_kernel,
        grid=(n_slab, batch),
        in_specs=[qkv_spec, qkv_spec, qkv_spec, tab_spec(t1), tab_spec(t4), tab_spec(t16)],
        out_specs=pl.BlockSpec((seq, LANES), lambda hp, b: (b, hp)),
        out_shape=jax.ShapeDtypeStruct((t, n_slab * LANES), BF16),
        scratch_shapes=[full] * 3 + [by_class] * 10,
        compiler_params=pltpu.CompilerParams(dimension_semantics=("arbitrary", "arbitrary"),
                                             vmem_limit_bytes=VMEM_LIMIT),
        name="attn_a",
    )(qa, ka, va, t1, t4, t16)


def _attn_b_kernel(q_ref, k_ref, v_ref, tab_ref, sink_ref, o_ref):
    lo = _low_lanes()
    n_slab, seq, _ = q_ref.shape
    win = tab_ref.shape[-1]
    nb = seq // BQ
    zero = jnp.zeros((BQ, LANES), q_ref.dtype)

    for i in range(nb):
        q_rows, kv_rows, tidx = _block_geometry(i, nb, seq, win)
        q = [q_ref[sl, q_rows, :] for sl in range(n_slab)]
        q_all = jnp.concatenate([part for qs in q for part in (jnp.where(lo, qs, zero), jnp.where(lo, zero, qs))],
                                axis=0)
        tab = jnp.concatenate([tab_ref[sl, tidx] for sl in range(n_slab)], axis=0)
        s = lax.dot_general(q_all, k_ref[kv_rows, :], (((1,), (1,)), ((), ())), preferred_element_type=F32) + tab
        m = jnp.max(s, axis=-1, keepdims=True)
        p = jnp.exp2(s - m)
        lsum = jnp.sum(p, axis=-1, keepdims=True)
        o2 = _dot(p.astype(BF16), v_ref[kv_rows, :])
        for sl in range(n_slab):
            rows = slice(sl * 2 * BQ, (sl + 1) * 2 * BQ)
            o = _merge_heads(o2[rows], lo)
            mm = _merge_heads(m[rows], lo)
            l = _merge_heads(lsum[rows], lo)
            sink = sink_ref[sl]
            m_new = jnp.maximum(mm, sink)
            scale = jnp.exp2(mm - m_new)
            den = l * scale + jnp.exp2(sink - m_new)
            o_ref[q_rows, sl * LANES:(sl + 1) * LANES] = (o * (scale / den)).astype(o_ref.dtype)


def _attn_b_call(qb, kb, vb, tab, sink, batch, seq):
    n_slab, t, _ = qb.shape
    n_kv = kb.shape[0]
    group = n_slab // n_kv
    kv_spec = pl.BlockSpec((None, seq, LANES), lambda kv, b: (kv, b, 0))
    return pl.pallas_call(
        _attn_b_kernel,
        grid=(n_kv, batch),
        in_specs=[pl.BlockSpec((group, seq, LANES), lambda kv, b: (kv, b, 0)), kv_spec, kv_spec,
                  pl.BlockSpec((group,) + tab.shape[1:], lambda kv, b: (kv, 0, 0, 0)),
                  pl.BlockSpec((group,) + sink.shape[1:], lambda kv, b: (kv, 0, 0))],
        out_specs=pl.BlockSpec((seq, group * LANES), lambda kv, b: (b, kv)),
        out_shape=jax.ShapeDtypeStruct((t, n_slab * LANES), BF16),
        compiler_params=pltpu.CompilerParams(dimension_semantics=("arbitrary", "arbitrary"),
                                             vmem_limit_bytes=VMEM_LIMIT),
        name="attn_b",
    )(qb, kb, vb, tab, sink)


def _row_call(kernel, name, row_inputs, layer_inputs, layer, out_dim, scratch):
    t = row_inputs[0].shape[0]

    def row_spec(a):
        if a.ndim == 2:
            return pl.BlockSpec((TM, a.shape[1]), lambda i: (i, 0))
        return pl.BlockSpec((None, TM, a.shape[2]), lambda i: (layer, i, 0))

    return pl.pallas_call(
        kernel,
        grid=(t // TM,),
        in_specs=[row_spec(a) for a in row_inputs] + [_layer_spec(a, layer) for a in layer_inputs],
        out_specs=pl.BlockSpec((TM, out_dim), lambda i: (i, 0)),
        out_shape=jax.ShapeDtypeStruct((t, out_dim), F32),
        scratch_shapes=scratch,
        compiler_params=pltpu.CompilerParams(dimension_semantics=("arbitrary",),
                                             vmem_limit_bytes=VMEM_LIMIT),
        name=name,
    )(*row_inputs, *layer_inputs)


def _post_kernel(x_ref, ya_ref, yb_ref, p_ref, gmix_ref, gffn_ref, gple_ref, wgates_ref, wa_ref, wb_ref, wo_ref,
                 wgate_ref, wup_ref, wdown_ref, wpg_ref, wpp_ref, o_ref, merged, hid, xs):
    d = x_ref.shape[1]
    chunks = [slice(c * MXU_COLS, (c + 1) * MXU_COLS) for c in range(d // MXU_COLS)]

    h = _rms(x_ref[...], gmix_ref[...]).astype(BF16)
    ya = ya_ref[...]
    yb = yb_ref[...]
    for c, cs in enumerate(chunks):
        ga = _dot(h, wgates_ref[:, cs])
        gb = _dot(h, wgates_ref[:, d + c * MXU_COLS:d + (c + 1) * MXU_COLS])
        pa = _dot(ya, wa_ref[:, cs])
        pb = _dot(yb, wb_ref[:, cs])
        merged[:, cs] = (jax.nn.sigmoid(ga) * pa + jax.nn.sigmoid(gb) * pb).astype(BF16)
    mg = merged[...]
    for cs in chunks:
        xs[:, cs] = x_ref[:, cs] + _dot(mg, wo_ref[:, cs])

    h = _rms(xs[...], gffn_ref[...]).astype(BF16)
    for c in range(wgate_ref.shape[1] // MXU_COLS):
        cs = slice(c * MXU_COLS, (c + 1) * MXU_COLS)
        a = _dot(h, wgate_ref[:, cs])
        u = _dot(h, wup_ref[:, cs])
        hid[:, cs] = (a * jax.nn.sigmoid(a) * u).astype(BF16)
    hv = hid[...]
    for cs in chunks:
        xs[:, cs] = xs[:, cs] + _dot(hv, wdown_ref[:, cs])

    h = _rms(xs[...], gple_ref[...]).astype(BF16)
    pv = p_ref[...].astype(BF16)
    for cs in chunks:
        o_ref[:, cs] = xs[:, cs] + jax.nn.sigmoid(_dot(h, wpg_ref[:, cs])) * _dot(pv, wpp_ref[:, cs])


def _t5_bucket(rel):
    half_b = NUM_BUCKETS // 2
    max_exact = half_b // 2
    sign = jnp.where(rel > 0, half_b, 0)
    n = jnp.abs(rel)
    nf = jnp.maximum(n, 1).astype(F32)
    large = max_exact + (jnp.log(nf / max_exact) / math.log(MAX_DISTANCE / max_exact)
                         * (half_b - max_exact)).astype(jnp.int32)
    large = jnp.minimum(large, half_b - 1)
    return sign + jnp.where(n < max_exact, n, large)


def _band_tables(table, seq_len, win, half, dilation):
    nb = seq_len // BQ
    blocks = (0, min(1, nb - 1), nb - 1) if nb > 1 else (0,)
    starts = [max(0, min(i * BQ - (win - BQ) // 2, seq_len - win)) - i * BQ for i in blocks]
    n = BQ + win - 1
    tabs = []
    for s in starts:
        rel = s - (BQ - 1) + jnp.arange(n, dtype=jnp.int32)
        bias = table[_t5_bucket(rel * dilation)].astype(F32) * LOG2_E
        vals = jnp.where((jnp.abs(rel) <= half)[:, None], bias, NEG_INF)
        vpad = jnp.pad(vals.T, ((0, 0), (0, 1)))
        skew = jnp.tile(vpad, (1, BQ))[:, :BQ * n].reshape(-1, BQ, n)
        tabs.append(skew[:, :, BQ - 1:BQ - 1 + win])
    t = jnp.stack(tabs, axis=1)
    h = t.shape[0]
    return t.reshape(h // 2, 2, len(starts), BQ, win).transpose(0, 2, 1, 3, 4).reshape(
        h // 2, len(starts), 2 * BQ, win)


def _pair_gain(g, scale=1.0):
    return jnp.tile(g.astype(F32) * scale, 2)


def kernel(x, p, rel_table, norm_mix_g, w_in, qnorm_a_g, knorm_a_g, qnorm_b_g, knorm_b_g, sink_b,
           w_branch_a, w_branch_b, w_out, norm_ffn_g, w_ffn_gate, w_ffn_up, w_ffn_down,
           norm_ple_g, w_ple_gate, w_ple_proj):
    batch, seq, d = x.shape
    depth = p.shape[0]
    t = batch * seq
    q_scale = HEAD_DIM ** -0.5 * LOG2_E

    table_a = rel_table[:, :N_HEADS_A]
    table_b = rel_table[:, N_HEADS_A:]
    tabs_a = []
    for w, dil in DILATED_PATTERNS:
        sub = seq // dil
        tabs_a.append(_band_tables(table_a, sub, min(2 * BQ, sub), w // (2 * dil), dil))
    tab_b = _band_tables(table_b, seq, BQ + 2 * WINDOW_B, WINDOW_B, 1)

    bf = lambda w: w.astype(BF16)
    rows = lambda g: g.astype(F32).reshape(depth, 1, -1)
    w_qkv_b, w_gates_b = bf(w_in[:, :, :GATE_COL]), bf(w_in[:, :, GATE_COL:])
    w_a_b, w_b_b, w_o_b = bf(w_branch_a), bf(w_branch_b), bf(w_out)
    w_gate_b, w_up_b, w_down_b = bf(w_ffn_gate), bf(w_ffn_up), bf(w_ffn_down)
    w_pg_b, w_pp_b = bf(w_ple_gate), bf(w_ple_proj)
    g_mix, g_ffn, g_ple = rows(norm_mix_g), rows(norm_ffn_g), rows(norm_ple_g)
    p2 = p.reshape(depth, t, -1)

    x2 = x.reshape(t, d)
    for l in range(depth):
        gains = jnp.stack([_pair_gain(qnorm_a_g[l], q_scale), _pair_gain(knorm_a_g[l]),
                           _pair_gain(qnorm_b_g[l], q_scale), _pair_gain(knorm_b_g[l])])
        qa, ka, va, qb, kb, vb = _qkv_call(x2, g_mix, w_qkv_b, gains, l)
        ya = _attn_a_call(qa, ka, va, *tabs_a, batch, seq)
        sink = jnp.repeat(sink_b[l].astype(F32) * LOG2_E, HEAD_DIM).reshape(N_HEADS_B // 2, 1, LANES)
        yb = _attn_b_call(qb, kb, vb, tab_b, sink, batch, seq)
        x2 = _row_call(_post_kernel, "post", [x2, ya, yb, p2],
                       [g_mix, g_ffn, g_ple, w_gates_b, w_a_b, w_b_b, w_o_b, w_gate_b, w_up_b, w_down_b,
                        w_pg_b, w_pp_b], l, d,
                       [pltpu.VMEM((TM, d), BF16), pltpu.VMEM((TM, w_ffn_gate.shape[2]), BF16),
                        pltpu.VMEM((TM, d), F32)])
    return x2.reshape(batch, seq, d)
```

```python
import math

import jax
import jax.numpy as jnp
from jax import lax
from jax.experimental import pallas as pl
from jax.experimental.pallas import tpu as pltpu

F32 = jnp.float32
BF16 = jnp.bfloat16

HEAD_DIM = 64
N_HEADS_A = 8
N_HEADS_B = 8
N_KV_B = 2
DILATED_PATTERNS = ((128, 1), (512, 4), (2048, 16))
WINDOW_B = 128
NUM_BUCKETS = 32
MAX_DISTANCE = 1024
RMS_EPS = 1e-6
NEG_INF = -1e30
LOG2_E = math.log2(math.e)

WIDTH_A = N_HEADS_A * HEAD_DIM
WIDTH_BQ = N_HEADS_B * HEAD_DIM
WIDTH_BKV = N_KV_B * HEAD_DIM
GATE_COL = 3 * WIDTH_A + WIDTH_BQ + 2 * WIDTH_BKV

LANES = 128
MXU_COLS = 256
BQ = 128
TM = 512
VMEM_LIMIT = 56 * 1024 * 1024


def _dot(a, b):
    return jnp.dot(a, b, preferred_element_type=F32)


def _rms(x, g):
    ms = jnp.mean(x * x, axis=-1, keepdims=True)
    return x * lax.rsqrt(ms + RMS_EPS) * g


def _low_lanes():
    return lax.broadcasted_iota(jnp.int32, (1, LANES), 1) < HEAD_DIM


def _head_rms(y, gain, lo):
    sq = y * y
    s0 = jnp.sum(jnp.where(lo, sq, 0.0), axis=-1, keepdims=True)
    s1 = jnp.sum(jnp.where(lo, 0.0, sq), axis=-1, keepdims=True)
    ms = jnp.where(lo, s0, s1) * (1.0 / HEAD_DIM)
    return y * lax.rsqrt(ms + RMS_EPS) * gain


def _const_spec(shape):
    nd = len(shape)
    return pl.BlockSpec(shape, lambda *_: (0,) * nd, pipeline_mode=pl.Buffered(1))


def _layer_spec(stacked, layer):
    nd = stacked.ndim - 1
    return pl.BlockSpec((None,) + stacked.shape[1:], lambda *_: (layer,) + (0,) * nd,
                        pipeline_mode=pl.Buffered(1))


def _qkv_kernel(x_ref, g_ref, w_ref, gains_ref, qa_ref, ka_ref, va_ref, qb_ref, kb_ref, vb_ref):
    lo = _low_lanes()
    h = _rms(x_ref[...], g_ref[...]).astype(BF16)
    n_pair = WIDTH_A // MXU_COLS
    order = sorted(range(GATE_COL // MXU_COLS), key=lambda c: 2 * n_pair <= c < 3 * n_pair)
    for c in order:
        y = _dot(h, w_ref[:, c * MXU_COLS:(c + 1) * MXU_COLS])
        y0, y1 = y[:, :LANES], y[:, LANES:]
        if c < n_pair:
            qa_ref[2 * c] = _head_rms(y0, gains_ref[0:1, :], lo)
            qa_ref[2 * c + 1] = _head_rms(y1, gains_ref[0:1, :], lo)
        elif c < 2 * n_pair:
            ka_ref[2 * (c - n_pair)] = _head_rms(y0, gains_ref[1:2, :], lo)
            ka_ref[2 * (c - n_pair) + 1] = _head_rms(y1, gains_ref[1:2, :], lo)
        elif c < 3 * n_pair:
            va_ref[2 * (c - 2 * n_pair)] = y0
            va_ref[2 * (c - 2 * n_pair) + 1] = y1
        elif c < 4 * n_pair:
            qb_ref[2 * (c - 3 * n_pair)] = _head_rms(y0, gains_ref[2:3, :], lo).astype(BF16)
            qb_ref[2 * (c - 3 * n_pair) + 1] = _head_rms(y1, gains_ref[2:3, :], lo).astype(BF16)
        else:
            kn = _head_rms(y0, gains_ref[3:4, :], lo)
            ks = pltpu.roll(kn, HEAD_DIM, axis=1)
            vs = pltpu.roll(y1, HEAD_DIM, axis=1)
            kb_ref[0] = jnp.where(lo, kn, ks).astype(BF16)
            kb_ref[1] = jnp.where(lo, ks, kn).astype(BF16)
            vb_ref[0] = jnp.where(lo, y1, vs).astype(BF16)
            vb_ref[1] = jnp.where(lo, vs, y1).astype(BF16)


def _qkv_call(x2, g, w_in, gains, layer):
    t, d = x2.shape
    n_q = WIDTH_A // LANES
    n_kv = N_KV_B
    assert WIDTH_BQ == WIDTH_A and 2 * WIDTH_BKV == MXU_COLS
    slab = lambda n, dt: jax.ShapeDtypeStruct((n, t, LANES), dt)
    slab_spec = lambda n: pl.BlockSpec((n, TM, LANES), lambda i: (0, i, 0))
    return pl.pallas_call(
        _qkv_kernel,
        grid=(t // TM,),
        in_specs=[pl.BlockSpec((TM, d), lambda i: (i, 0)), _layer_spec(g, layer),
                  _layer_spec(w_in, layer), _const_spec(gains.shape)],
        out_specs=[slab_spec(n_q), slab_spec(n_q), slab_spec(n_q), slab_spec(n_q),
                   slab_spec(n_kv), slab_spec(n_kv)],
        out_shape=[slab(n_q, F32), slab(n_q, F32), slab(n_q, F32), slab(n_q, BF16),
                   slab(n_kv, BF16), slab(n_kv, BF16)],
        compiler_params=pltpu.CompilerParams(dimension_semantics=("arbitrary",),
                                             vmem_limit_bytes=VMEM_LIMIT),
        name="qkv",
    )(x2, g, w_in, gains)


def _attn_block(q, kw, vw, tab, lo):
    zero = jnp.zeros_like(q)
    q2 = jnp.concatenate([jnp.where(lo, q, zero), jnp.where(lo, zero, q)], axis=0).astype(BF16)
    s = lax.dot_general(q2, kw.astype(BF16), (((1,), (1,)), ((), ())),
                        preferred_element_type=F32) + tab
    m = jnp.max(s, axis=-1, keepdims=True)
    p = jnp.exp2(s - m)
    l = jnp.sum(p, axis=-1, keepdims=True)
    o2 = _dot(p.astype(BF16), vw.astype(BF16))
    return o2, m, l


def _merge_heads(a2, lo):
    bq = a2.shape[0] // 2
    return jnp.where(lo, a2[:bq], a2[bq:])


def _block_geometry(i, n_blocks, seq_len, win):
    w0 = max(0, min(i * BQ - (win - BQ) // 2, seq_len - win))
    return pl.ds(i * BQ, BQ), pl.ds(w0, win), (0 if i == 0 else (2 if i == n_blocks - 1 else 1))


def _attn_a_kernel(q_ref, k_ref, v_ref, t1_ref, t4_ref, t16_ref, out_ref,
                   o1, lse1, outf, qs, ks, vs, o4, lse4, o16g, lse16g, o16, lse16, outc):
    lo = _low_lanes()
    seq = q_ref.shape[0]
    win = t1_ref.shape[-1]
    d4 = DILATED_PATTERNS[1][1]
    step = DILATED_PATTERNS[2][1] // d4
    sub4 = seq // d4

    def block(qr, kr, vr, q_rows, kv_rows, tab, o_out, lse_out, out_rows):
        o2, m, l = _attn_block(qr[q_rows, :], kr[kv_rows, :], vr[kv_rows, :], tab, lo)
        l = _merge_heads(l, lo)
        o_out[out_rows, :] = _merge_heads(o2, lo) / l
        lse_out[out_rows, :] = _merge_heads(m, lo) + jnp.log2(l)

    for r in range(d4):
        res = pl.ds(r, sub4, stride=d4)
        qs[r] = q_ref[res, :]
        ks[r] = k_ref[res, :]
        vs[r] = v_ref[res, :]

    nb = seq // BQ
    for i in range(nb):
        q_rows, kv_rows, tidx = _block_geometry(i, nb, seq, win)
        block(q_ref, k_ref, v_ref, q_rows, kv_rows, t1_ref[tidx], o1, lse1, q_rows)

    nb4 = sub4 // BQ
    for i in range(nb4):
        for r in range(d4):
            q_rows, kv_rows, tidx = _block_geometry(i, nb4, sub4, win)
            block(qs.at[r], ks.at[r], vs.at[r], q_rows, kv_rows, t4_ref[tidx], o4.at[r], lse4.at[r], q_rows)

    for b in range(step):
        for r in range(d4):
            rows = pl.ds(b, BQ, stride=step)
            block(qs.at[r], ks.at[r], vs.at[r], rows, rows, t16_ref[0], o16g.at[r], lse16g.at[r],
                  pl.ds(b * BQ, BQ))

    def merge_class(r, carry):
        for grouped, ordered in ((o16g, o16), (lse16g, lse16)):
            for b in range(step):
                ordered[r, pl.ds(b, BQ, stride=step), :] = grouped[r, pl.ds(b * BQ, BQ), :]
        for i in range(nb4):
            rows = pl.ds(i * BQ, BQ)
            nat = pl.ds(r + d4 * i * BQ, BQ, stride=d4)
            la, lb, lc = lse1[nat, :], lse4[r, rows, :], lse16[r, rows, :]
            mx = jnp.maximum(jnp.maximum(la, lb), lc)
            wa, wb, wc = jnp.exp2(la - mx), jnp.exp2(lb - mx), jnp.exp2(lc - mx)
            num = wa * o1[nat, :] + wb * o4[r, rows, :] + wc * o16[r, rows, :]
            outc[r, rows, :] = num / (wa + wb + wc)
        outf[pl.ds(r, sub4, stride=d4), :] = outc[r]
        return carry

    lax.fori_loop(0, d4, merge_class, 0)
    out_ref[...] = outf[...].astype(out_ref.dtype)


def _attn_a_call(qa, ka, va, t1, t4, t16, batch, seq):
    n_slab, t, _ = qa.shape
    assert seq // DILATED_PATTERNS[2][1] == BQ
    qkv_spec = pl.BlockSpec((None, seq, LANES), lambda hp, b: (hp, b, 0))
    tab_spec = lambda tab: pl.BlockSpec((None,) + tab.shape[1:], lambda hp, b: (hp, 0, 0, 0))
    d4 = DILATED_PATTERNS[1][1]
    full = pltpu.VMEM((seq, LANES), F32)
    by_class = pltpu.VMEM((d4, seq // d4, LANES), F32)
    return pl.pallas_call(
        _attn_a_kernel,
        grid=(n_slab, batch),
        in_specs=[qkv_spec, qkv_spec, qkv_spec, tab_spec(t1), tab_spec(t4), tab_spec(t16)],
        out_specs=pl.BlockSpec((seq, LANES), lambda hp, b: (b, hp)),
        out_shape=jax.ShapeDtypeStruct((t, n_slab * LANES), BF16),
        scratch_shapes=[full] * 3 + [by_class] * 10,
        compiler_params=pltpu.CompilerParams(dimension_semantics=("arbitrary", "arbitrary"),
                                             vmem_limit_bytes=VMEM_LIMIT),
        name="attn_a",
    )(qa, ka, va, t1, t4, t16)


def _attn_b_kernel(q_ref, k_ref, v_ref, tab_ref, sink_ref, o_ref):
    lo = _low_lanes()
    n_slab, seq, _ = q_ref.shape
    win = tab_ref.shape[-1]
    nb = seq // BQ
    zero = jnp.zeros((BQ, LANES), q_ref.dtype)

    for i in range(nb):
        q_rows, kv_rows, tidx = _block_geometry(i, nb, seq, win)
        q = [q_ref[sl, q_rows, :] for sl in range(n_slab)]
        q_all = jnp.concatenate([part for qs in q for part in (jnp.where(lo, qs, zero), jnp.where(lo, zero, qs))],
                                axis=0)
        tab = jnp.concatenate([tab_ref[sl, tidx] for sl in range(n_slab)], axis=0)
        s = lax.dot_general(q_all, k_ref[kv_rows, :], (((1,), (1,)), ((), ())), preferred_element_type=F32) + tab
        m = jnp.max(s, axis=-1, keepdims=True)
        p = jnp.exp2(s - m)
        vw = v_ref[kv_rows, :]
        o2 = _dot(p.astype(BF16), jnp.concatenate([vw, jnp.ones(vw.shape, BF16)], axis=1))
        for sl in range(n_slab):
            rows = slice(sl * 2 * BQ, (sl + 1) * 2 * BQ)
            o = _merge_heads(o2[rows, :LANES], lo)
            mm = _merge_heads(m[rows], lo)
            l = _merge_heads(o2[rows, LANES:], lo)
            sink = sink_ref[sl]
            m_new = jnp.maximum(mm, sink)
            scale = jnp.exp2(mm - m_new)
            den = l * scale + jnp.exp2(sink - m_new)
            o_ref[q_rows, sl * LANES:(sl + 1) * LANES] = (o * (scale / den)).astype(o_ref.dtype)


def _attn_b_call(qb, kb, vb, tab, sink, batch, seq):
    n_slab, t, _ = qb.shape
    n_kv = kb.shape[0]
    group = n_slab // n_kv
    kv_spec = pl.BlockSpec((None, seq, LANES), lambda kv, b: (kv, b, 0))
    return pl.pallas_call(
        _attn_b_kernel,
        grid=(n_kv, batch),
        in_specs=[pl.BlockSpec((group, seq, LANES), lambda kv, b: (kv, b, 0)), kv_spec, kv_spec,
                  pl.BlockSpec((group,) + tab.shape[1:], lambda kv, b: (kv, 0, 0, 0)),
                  pl.BlockSpec((group,) + sink.shape[1:], lambda kv, b: (kv, 0, 0))],
        out_specs=pl.BlockSpec((seq, group * LANES), lambda kv, b: (b, kv)),
        out_shape=jax.ShapeDtypeStruct((t, n_slab * LANES), BF16),
        compiler_params=pltpu.CompilerParams(dimension_semantics=("arbitrary", "arbitrary"),
                                             vmem_limit_bytes=VMEM_LIMIT),
        name="attn_b",
    )(qb, kb, vb, tab, sink)


def _row_call(kernel, name, row_inputs, layer_inputs, layer, out_dim, scratch):
    t = row_inputs[0].shape[0]

    def row_spec(a):
        if a.ndim == 2:
            return pl.BlockSpec((TM, a.shape[1]), lambda i: (i, 0))
        return pl.BlockSpec((None, TM, a.shape[2]), lambda i: (layer, i, 0))

    return pl.pallas_call(
        kernel,
        grid=(t // TM,),
        in_specs=[row_spec(a) for a in row_inputs] + [_layer_spec(a, layer) for a in layer_inputs],
        out_specs=pl.BlockSpec((TM, out_dim), lambda i: (i, 0)),
        out_shape=jax.ShapeDtypeStruct((t, out_dim), F32),
        scratch_shapes=scratch,
        compiler_params=pltpu.CompilerParams(dimension_semantics=("arbitrary",),
                                             vmem_limit_bytes=VMEM_LIMIT),
        name=name,
    )(*row_inputs, *layer_inputs)


def _post_kernel(x_ref, ya_ref, yb_ref, p_ref, gmix_ref, gffn_ref, gple_ref, wgates_ref, wa_ref, wb_ref, wo_ref,
                 wgate_ref, wup_ref, wdown_ref, wpg_ref, wpp_ref, o_ref, merged, hid, xs):
    d = x_ref.shape[1]
    chunks = [slice(c * MXU_COLS, (c + 1) * MXU_COLS) for c in range(d // MXU_COLS)]

    h = _rms(x_ref[...], gmix_ref[...]).astype(BF16)
    ya = ya_ref[...]
    yb = yb_ref[...]
    for c, cs in enumerate(chunks):
        ga = _dot(h, wgates_ref[:, cs])
        gb = _dot(h, wgates_ref[:, d + c * MXU_COLS:d + (c + 1) * MXU_COLS])
        pa = _dot(ya, wa_ref[:, cs])
        pb = _dot(yb, wb_ref[:, cs])
        merged[:, cs] = (jax.nn.sigmoid(ga) * pa + jax.nn.sigmoid(gb) * pb).astype(BF16)
    mg = merged[...]
    for cs in chunks:
        xs[:, cs] = x_ref[:, cs] + _dot(mg, wo_ref[:, cs])

    h = _rms(xs[...], gffn_ref[...]).astype(BF16)
    for c in range(wgate_ref.shape[1] // MXU_COLS):
        cs = slice(c * MXU_COLS, (c + 1) * MXU_COLS)
        a = _dot(h, wgate_ref[:, cs])
        u = _dot(h, wup_ref[:, cs])
        hid[:, cs] = (a * jax.nn.sigmoid(a) * u).astype(BF16)
    hv = hid[...]
    for cs in chunks:
        xs[:, cs] = xs[:, cs] + _dot(hv, wdown_ref[:, cs])

    h = _rms(xs[...], gple_ref[...]).astype(BF16)
    pv = p_ref[...].astype(BF16)
    for cs in chunks:
        o_ref[:, cs] = xs[:, cs] + jax.nn.sigmoid(_dot(h, wpg_ref[:, cs])) * _dot(pv, wpp_ref[:, cs])


def _t5_bucket(rel):
    half_b = NUM_BUCKETS // 2
    max_exact = half_b // 2
    sign = jnp.where(rel > 0, half_b, 0)
    n = jnp.abs(rel)
    nf = jnp.maximum(n, 1).astype(F32)
    large = max_exact + (jnp.log(nf / max_exact) / math.log(MAX_DISTANCE / max_exact)
                         * (half_b - max_exact)).astype(jnp.int32)
    large = jnp.minimum(large, half_b - 1)
    return sign + jnp.where(n < max_exact, n, large)


def _band_tables(table, seq_len, win, half, dilation):
    nb = seq_len // BQ
    blocks = (0, min(1, nb - 1), nb - 1) if nb > 1 else (0,)
    starts = [max(0, min(i * BQ - (win - BQ) // 2, seq_len - win)) - i * BQ for i in blocks]
    n = BQ + win - 1
    n_tab = len(starts)
    h = table.shape[1]
    rel = jnp.asarray(starts, jnp.int32)[:, None] - (BQ - 1) + jnp.arange(n, dtype=jnp.int32)[None, :]
    bias = table[_t5_bucket(rel * dilation)].astype(F32) * LOG2_E
    vals = jnp.where((jnp.abs(rel) <= half)[:, :, None], bias, NEG_INF)
    vpad = jnp.pad(jnp.transpose(vals, (2, 0, 1)), ((0, 0), (0, 0), (0, 1)))
    skew = jnp.tile(vpad, (1, 1, BQ))[:, :, :BQ * n].reshape(h, n_tab, BQ, n)
    t = skew[:, :, :, BQ - 1:BQ - 1 + win]
    return t.reshape(h // 2, 2, n_tab, BQ, win).transpose(0, 2, 1, 3, 4).reshape(h // 2, n_tab, 2 * BQ, win)


def _pair_gain(g, scale=1.0):
    return jnp.tile(g.astype(F32) * scale, 2)


def kernel(x, p, rel_table, norm_mix_g, w_in, qnorm_a_g, knorm_a_g, qnorm_b_g, knorm_b_g, sink_b,
           w_branch_a, w_branch_b, w_out, norm_ffn_g, w_ffn_gate, w_ffn_up, w_ffn_down,
           norm_ple_g, w_ple_gate, w_ple_proj):
    batch, seq, d = x.shape
    depth = p.shape[0]
    t = batch * seq
    q_scale = HEAD_DIM ** -0.5 * LOG2_E

    table_a = rel_table[:, :N_HEADS_A]
    table_b = rel_table[:, N_HEADS_A:]
    tabs_a = []
    for w, dil in DILATED_PATTERNS:
        sub = seq // dil
        tabs_a.append(_band_tables(table_a, sub, min(2 * BQ, sub), w // (2 * dil), dil))
    tab_b = _band_tables(table_b, seq, BQ + 2 * WINDOW_B, WINDOW_B, 1)

    bf = lambda w: w.astype(BF16)
    rows = lambda g: g.astype(F32).reshape(depth, 1, -1)
    w_qkv_b, w_gates_b = bf(w_in[:, :, :GATE_COL]), bf(w_in[:, :, GATE_COL:])
    w_a_b, w_b_b, w_o_b = bf(w_branch_a), bf(w_branch_b), bf(w_out)
    w_gate_b, w_up_b, w_down_b = bf(w_ffn_gate), bf(w_ffn_up), bf(w_ffn_down)
    w_pg_b, w_pp_b = bf(w_ple_gate), bf(w_ple_proj)
    g_mix, g_ffn, g_ple = rows(norm_mix_g), rows(norm_ffn_g), rows(norm_ple_g)
    p2 = p.reshape(depth, t, -1)

    x2 = x.reshape(t, d)
    for l in range(depth):
        gains = jnp.stack([_pair_gain(qnorm_a_g[l], q_scale), _pair_gain(knorm_a_g[l]),
                           _pair_gain(qnorm_b_g[l], q_scale), _pair_gain(knorm_b_g[l])])
        qa, ka, va, qb, kb, vb = _qkv_call(x2, g_mix, w_qkv_b, gains, l)
        ya = _attn_a_call(qa, ka, va, *tabs_a, batch, seq)
        sink = jnp.repeat(sink_b[l].astype(F32) * LOG2_E, HEAD_DIM).reshape(N_HEADS_B // 2, 1, LANES)
        yb = _attn_b_call(qb, kb, vb, tab_b, sink, batch, seq)
        x2 = _row_call(_post_kernel, "post", [x2, ya, yb, p2],
                       [g_mix, g_ffn, g_ple, w_gates_b, w_a_b, w_b_b, w_o_b, w_gate_b, w_up_b, w_down_b,
                        w_pg_b, w_pp_b], l, d,
                       [pltpu.VMEM((TM, d), BF16), pltpu.VMEM((TM, w_ffn_gate.shape[2]), BF16),
                        pltpu.VMEM((TM, d), F32)])
    return x2.reshape(batch, seq, d)
```

```python
import math

import jax
import jax.numpy as jnp
from jax import lax
from jax.experimental import pallas as pl
from jax.experimental.pallas import tpu as pltpu

F32 = jnp.float32
BF16 = jnp.bfloat16

HEAD_DIM = 64
N_HEADS_A = 8
N_HEADS_B = 8
N_KV_B = 2
DILATED_PATTERNS = ((128, 1), (512, 4), (2048, 16))
WINDOW_B = 128
NUM_BUCKETS = 32
MAX_DISTANCE = 1024
RMS_EPS = 1e-6
NEG_INF = -1e30
LOG2_E = math.log2(math.e)

WIDTH_A = N_HEADS_A * HEAD_DIM
WIDTH_BQ = N_HEADS_B * HEAD_DIM
WIDTH_BKV = N_KV_B * HEAD_DIM
GATE_COL = 3 * WIDTH_A + WIDTH_BQ + 2 * WIDTH_BKV

LANES = 128
MXU_COLS = 256
BQ = 128
TM = 512
VMEM_LIMIT = 56 * 1024 * 1024


def _dot(a, b):
    return jnp.dot(a, b, preferred_element_type=F32)


def _rms(x, g):
    ms = jnp.mean(x * x, axis=-1, keepdims=True)
    return x * lax.rsqrt(ms + RMS_EPS) * g


def _low_lanes():
    return lax.broadcasted_iota(jnp.int32, (1, LANES), 1) < HEAD_DIM


def _head_rms(y, gain, lo):
    sq = y * y
    s0 = jnp.sum(jnp.where(lo, sq, 0.0), axis=-1, keepdims=True)
    s1 = jnp.sum(jnp.where(lo, 0.0, sq), axis=-1, keepdims=True)
    ms = jnp.where(lo, s0, s1) * (1.0 / HEAD_DIM)
    return y * lax.rsqrt(ms + RMS_EPS) * gain


def _const_spec(shape):
    nd = len(shape)
    return pl.BlockSpec(shape, lambda *_: (0,) * nd, pipeline_mode=pl.Buffered(1))


def _layer_spec(stacked, layer):
    nd = stacked.ndim - 1
    return pl.BlockSpec((None,) + stacked.shape[1:], lambda *_: (layer,) + (0,) * nd,
                        pipeline_mode=pl.Buffered(1))


def _qkv_kernel(x_ref, g_ref, w_ref, gains_ref, qa_ref, ka_ref, va_ref, qb_ref, kb_ref, vb_ref):
    lo = _low_lanes()
    h = _rms(x_ref[...], g_ref[...]).astype(BF16)
    n_pair = WIDTH_A // MXU_COLS
    order = sorted(range(GATE_COL // MXU_COLS), key=lambda c: 2 * n_pair <= c < 3 * n_pair)
    for c in order:
        y = _dot(h, w_ref[:, c * MXU_COLS:(c + 1) * MXU_COLS])
        y0, y1 = y[:, :LANES], y[:, LANES:]
        if c < n_pair:
            qa_ref[2 * c] = _head_rms(y0, gains_ref[0:1, :], lo)
            qa_ref[2 * c + 1] = _head_rms(y1, gains_ref[0:1, :], lo)
        elif c < 2 * n_pair:
            ka_ref[2 * (c - n_pair)] = _head_rms(y0, gains_ref[1:2, :], lo)
            ka_ref[2 * (c - n_pair) + 1] = _head_rms(y1, gains_ref[1:2, :], lo)
        elif c < 3 * n_pair:
            va_ref[2 * (c - 2 * n_pair)] = y0
            va_ref[2 * (c - 2 * n_pair) + 1] = y1
        elif c < 4 * n_pair:
            qb_ref[2 * (c - 3 * n_pair)] = _head_rms(y0, gains_ref[2:3, :], lo).astype(BF16)
            qb_ref[2 * (c - 3 * n_pair) + 1] = _head_rms(y1, gains_ref[2:3, :], lo).astype(BF16)
        else:
            kn = _head_rms(y0, gains_ref[3:4, :], lo)
            ks = pltpu.roll(kn, HEAD_DIM, axis=1)
            vs = pltpu.roll(y1, HEAD_DIM, axis=1)
            kb_ref[0] = jnp.where(lo, kn, ks).astype(BF16)
            kb_ref[1] = jnp.where(lo, ks, kn).astype(BF16)
            vb_ref[0] = jnp.where(lo, y1, vs).astype(BF16)
            vb_ref[1] = jnp.where(lo, vs, y1).astype(BF16)


def _qkv_call(x2, g, w_in, gains, layer):
    t, d = x2.shape
    n_q = WIDTH_A // LANES
    n_kv = N_KV_B
    assert WIDTH_BQ == WIDTH_A and 2 * WIDTH_BKV == MXU_COLS
    slab = lambda n, dt: jax.ShapeDtypeStruct((n, t, LANES), dt)
    slab_spec = lambda n: pl.BlockSpec((n, TM, LANES), lambda i: (0, i, 0))
    return pl.pallas_call(
        _qkv_kernel,
        grid=(t // TM,),
        in_specs=[pl.BlockSpec((TM, d), lambda i: (i, 0)), _layer_spec(g, layer),
                  _layer_spec(w_in, layer), _const_spec(gains.shape)],
        out_specs=[slab_spec(n_q), slab_spec(n_q), slab_spec(n_q), slab_spec(n_q),
                   slab_spec(n_kv), slab_spec(n_kv)],
        out_shape=[slab(n_q, F32), slab(n_q, F32), slab(n_q, F32), slab(n_q, BF16),
                   slab(n_kv, BF16), slab(n_kv, BF16)],
        compiler_params=pltpu.CompilerParams(dimension_semantics=("arbitrary",),
                                             vmem_limit_bytes=VMEM_LIMIT),
        name="qkv",
    )(x2, g, w_in, gains)


def _attn_block(q, kw, vw, tab, lo):
    zero = jnp.zeros_like(q)
    q2 = jnp.concatenate([jnp.where(lo, q, zero), jnp.where(lo, zero, q)], axis=0).astype(BF16)
    s = lax.dot_general(q2, kw.astype(BF16), (((1,), (1,)), ((), ())),
                        preferred_element_type=F32) + tab
    m = jnp.max(s, axis=-1, keepdims=True)
    p = jnp.exp2(s - m)
    l = jnp.sum(p, axis=-1, keepdims=True)
    o2 = _dot(p.astype(BF16), vw.astype(BF16))
    return o2, m, l


def _merge_heads(a2, lo):
    bq = a2.shape[0] // 2
    return jnp.where(lo, a2[:bq], a2[bq:])


def _block_geometry(i, n_blocks, seq_len, win):
    w0 = max(0, min(i * BQ - (win - BQ) // 2, seq_len - win))
    return pl.ds(i * BQ, BQ), pl.ds(w0, win), (0 if i == 0 else (2 if i == n_blocks - 1 else 1))


def _attn_a_kernel(q_ref, k_ref, v_ref, t1_ref, t4_ref, t16_ref, out_ref,
                   o1, lse1, outf, qs, ks, vs, o4, lse4, o16g, lse16g, o16, lse16, outc):
    lo = _low_lanes()
    seq = q_ref.shape[0]
    win = t1_ref.shape[-1]
    d4 = DILATED_PATTERNS[1][1]
    step = DILATED_PATTERNS[2][1] // d4
    sub4 = seq // d4

    def block(qr, kr, vr, q_rows, kv_rows, tab, o_out, lse_out, out_rows):
        o2, m, l = _attn_block(qr[q_rows, :], kr[kv_rows, :], vr[kv_rows, :], tab, lo)
        l = _merge_heads(l, lo)
        o_out[out_rows, :] = _merge_heads(o2, lo) / l
        lse_out[out_rows, :] = _merge_heads(m, lo) + jnp.log2(l)

    for r in range(d4):
        res = pl.ds(r, sub4, stride=d4)
        qs[r] = q_ref[res, :]
        ks[r] = k_ref[res, :]
        vs[r] = v_ref[res, :]

    nb = seq // BQ
    for i in range(nb):
        q_rows, kv_rows, tidx = _block_geometry(i, nb, seq, win)
        block(q_ref, k_ref, v_ref, q_rows, kv_rows, t1_ref[tidx], o1, lse1, q_rows)

    nb4 = sub4 // BQ
    for i in range(nb4):
        for r in range(d4):
            q_rows, kv_rows, tidx = _block_geometry(i, nb4, sub4, win)
            block(qs.at[r], ks.at[r], vs.at[r], q_rows, kv_rows, t4_ref[tidx], o4.at[r], lse4.at[r], q_rows)

    for b in range(step):
        for r in range(d4):
            rows = pl.ds(b, BQ, stride=step)
            block(qs.at[r], ks.at[r], vs.at[r], rows, rows, t16_ref[0], o16g.at[r], lse16g.at[r],
                  pl.ds(b * BQ, BQ))

    def merge_class(r, carry):
        for grouped, ordered in ((o16g, o16), (lse16g, lse16)):
            for b in range(step):
                ordered[r, pl.ds(b, BQ, stride=step), :] = grouped[r, pl.ds(b * BQ, BQ), :]
        for i in range(nb4):
            rows = pl.ds(i * BQ, BQ)
            nat = pl.ds(r + d4 * i * BQ, BQ, stride=d4)
            la, lb, lc = lse1[nat, :], lse4[r, rows, :], lse16[r, rows, :]
            mx = jnp.maximum(jnp.maximum(la, lb), lc)
            wa, wb, wc = jnp.exp2(la - mx), jnp.exp2(lb - mx), jnp.exp2(lc - mx)
            num = wa * o1[nat, :] + wb * o4[r, rows, :] + wc * o16[r, rows, :]
            outc[r, rows, :] = num / (wa + wb + wc)
        outf[pl.ds(r, sub4, stride=d4), :] = outc[r]
        return carry

    lax.fori_loop(0, d4, merge_class, 0)
    out_ref[...] = outf[...].astype(out_ref.dtype)


def _attn_a_call(qa, ka, va, t1, t4, t16, batch, seq):
    n_slab, t, _ = qa.shape
    assert seq // DILATED_PATTERNS[2][1] == BQ
    qkv_spec = pl.BlockSpec((None, seq, LANES), lambda hp, b: (hp, b, 0))
    tab_spec = lambda tab: pl.BlockSpec((None,) + tab.shape[1:], lambda hp, b: (hp, 0, 0, 0))
    d4 = DILATED_PATTERNS[1][1]
    full = pltpu.VMEM((seq, LANES), F32)
    by_class = pltpu.VMEM((d4, seq // d4, LANES), F32)
    return pl.pallas_call(
        _attn_a_kernel,
        grid=(n_slab, batch),
        in_specs=[qkv_spec, qkv_spec, qkv_spec, tab_spec(t1), tab_spec(t4), tab_spec(t16)],
        out_specs=pl.BlockSpec((seq, LANES), lambda hp, b: (b, hp)),
        out_shape=jax.ShapeDtypeStruct((t, n_slab * LANES), BF16),
        scratch_shapes=[full] * 3 + [by_class] * 10,
        compiler_params=pltpu.CompilerParams(dimension_semantics=("arbitrary", "arbitrary"),
                                             vmem_limit_bytes=VMEM_LIMIT),
        name="attn_a",
    )(qa, ka, va, t1, t4, t16)


def _attn_b_kernel(q_ref, k_ref, v_ref, tab_ref, sink_ref, o_ref):
    lo = _low_lanes()
    n_slab, seq, _ = q_ref.shape
    win = tab_ref.shape[-1]
    nb = seq // BQ
    zero = jnp.zeros((BQ, LANES), q_ref.dtype)

    for i in range(nb):
        q_rows, kv_rows, tidx = _block_geometry(i, nb, seq, win)
        q = [q_ref[sl, q_rows, :] for sl in range(n_slab)]
        q_all = jnp.concatenate([part for qs in q for part in (jnp.where(lo, qs, zero), jnp.where(lo, zero, qs))],
                                axis=0)
        tab = jnp.concatenate([tab_ref[sl, tidx] for sl in range(n_slab)], axis=0)
        s = lax.dot_general(q_all, k_ref[kv_rows, :], (((1,), (1,)), ((), ())), preferred_element_type=F32) + tab
        m = jnp.max(s, axis=-1, keepdims=True)
        p = jnp.exp2(s - m)
        vw = v_ref[kv_rows, :]
        o2 = _dot(p.astype(BF16), jnp.concatenate([vw, jnp.ones(vw.shape, BF16)], axis=1))
        for sl in range(n_slab):
            rows = slice(sl * 2 * BQ, (sl + 1) * 2 * BQ)
            o = _merge_heads(o2[rows, :LANES], lo)
            mm = _merge_heads(m[rows], lo)
            l = _merge_heads(o2[rows, LANES:], lo)
            sink = sink_ref[sl]
            m_new = jnp.maximum(mm, sink)
            scale = jnp.exp2(mm - m_new)
            den = l * scale + jnp.exp2(sink - m_new)
            o_ref[q_rows, sl * LANES:(sl + 1) * LANES] = (o * (scale / den)).astype(o_ref.dtype)


def _attn_b_call(qb, kb, vb, tab, sink, batch, seq):
    n_slab, t, _ = qb.shape
    n_kv = kb.shape[0]
    group = n_slab // n_kv
    kv_spec = pl.BlockSpec((None, seq, LANES), lambda kv, b: (kv, b, 0))
    return pl.pallas_call(
        _attn_b_kernel,
        grid=(n_kv, batch),
        in_specs=[pl.BlockSpec((group, seq, LANES), lambda kv, b: (kv, b, 0)), kv_spec, kv_spec,
                  pl.BlockSpec((group,) + tab.shape[1:], lambda kv, b: (kv, 0, 0, 0)),
                  pl.BlockSpec((group,) + sink.shape[1:], lambda kv, b: (kv, 0, 0))],
        out_specs=pl.BlockSpec((seq, group * LANES), lambda kv, b: (b, kv)),
        out_shape=jax.ShapeDtypeStruct((t, n_slab * LANES), BF16),
        compiler_params=pltpu.CompilerParams(dimension_semantics=("arbitrary", "arbitrary"),
                                             vmem_limit_bytes=VMEM_LIMIT),
        name="attn_b",
    )(qb, kb, vb, tab, sink)


def _row_call(kernel, name, row_inputs, layer_inputs, layer, out_dim, scratch):
    t = row_inputs[0].shape[0]

    def row_spec(a):
        if a.ndim == 2:
            return pl.BlockSpec((TM, a.shape[1]), lambda i: (i, 0))
        return pl.BlockSpec((None, TM, a.shape[2]), lambda i: (layer, i, 0))

    return pl.pallas_call(
        kernel,
        grid=(t // TM,),
        in_specs=[row_spec(a) for a in row_inputs] + [_layer_spec(a, layer) for a in layer_inputs],
        out_specs=pl.BlockSpec((TM, out_dim), lambda i: (i, 0)),
        out_shape=jax.ShapeDtypeStruct((t, out_dim), F32),
        scratch_shapes=scratch,
        compiler_params=pltpu.CompilerParams(dimension_semantics=("arbitrary",),
                                             vmem_limit_bytes=VMEM_LIMIT),
        name=name,
    )(*row_inputs, *layer_inputs)


def _post_kernel(x_ref, ya_ref, yb_ref, p_ref, gmix_ref, gffn_ref, gple_ref, win_ref, wa_ref, wb_ref, wo_ref,
                 wgate_ref, wup_ref, wdown_ref, wpg_ref, wpp_ref, o_ref, merged, hid, xs):
    d = x_ref.shape[1]
    chunks = [slice(c * MXU_COLS, (c + 1) * MXU_COLS) for c in range(d // MXU_COLS)]

    h = _rms(x_ref[...], gmix_ref[...]).astype(BF16)
    ya = ya_ref[...]
    yb = yb_ref[...]
    for c, cs in enumerate(chunks):
        ga = _dot(h, win_ref[:, GATE_COL + c * MXU_COLS:GATE_COL + (c + 1) * MXU_COLS])
        gb = _dot(h, win_ref[:, GATE_COL + d + c * MXU_COLS:GATE_COL + d + (c + 1) * MXU_COLS])
        pa = _dot(ya, wa_ref[:, cs])
        pb = _dot(yb, wb_ref[:, cs])
        merged[:, cs] = (jax.nn.sigmoid(ga) * pa + jax.nn.sigmoid(gb) * pb).astype(BF16)
    mg = merged[...]
    for cs in chunks:
        xs[:, cs] = x_ref[:, cs] + _dot(mg, wo_ref[:, cs])

    h = _rms(xs[...], gffn_ref[...]).astype(BF16)
    for c in range(wgate_ref.shape[1] // MXU_COLS):
        cs = slice(c * MXU_COLS, (c + 1) * MXU_COLS)
        a = _dot(h, wgate_ref[:, cs])
        u = _dot(h, wup_ref[:, cs])
        hid[:, cs] = (a * jax.nn.sigmoid(a) * u).astype(BF16)
    hv = hid[...]
    for cs in chunks:
        xs[:, cs] = xs[:, cs] + _dot(hv, wdown_ref[:, cs])

    h = _rms(xs[...], gple_ref[...]).astype(BF16)
    pv = p_ref[...].astype(BF16)
    for cs in chunks:
        o_ref[:, cs] = xs[:, cs] + jax.nn.sigmoid(_dot(h, wpg_ref[:, cs])) * _dot(pv, wpp_ref[:, cs])


def _t5_bucket(rel):
    half_b = NUM_BUCKETS // 2
    max_exact = half_b // 2
    sign = jnp.where(rel > 0, half_b, 0)
    n = jnp.abs(rel)
    nf = jnp.maximum(n, 1).astype(F32)
    large = max_exact + (jnp.log(nf / max_exact) / math.log(MAX_DISTANCE / max_exact)
                         * (half_b - max_exact)).astype(jnp.int32)
    large = jnp.minimum(large, half_b - 1)
    return sign + jnp.where(n < max_exact, n, large)


def _band_tables(table, seq_len, win, half, dilation):
    nb = seq_len // BQ
    blocks = (0, min(1, nb - 1), nb - 1) if nb > 1 else (0,)
    starts = [max(0, min(i * BQ - (win - BQ) // 2, seq_len - win)) - i * BQ for i in blocks]
    n = BQ + win - 1
    tabs = []
    for s in starts:
        rel = s - (BQ - 1) + jnp.arange(n, dtype=jnp.int32)
        bias = table[_t5_bucket(rel * dilation)].astype(F32) * LOG2_E
        vals = jnp.where((jnp.abs(rel) <= half)[:, None], bias, NEG_INF)
        vpad = jnp.pad(vals.T, ((0, 0), (0, 1)))
        skew = jnp.tile(vpad, (1, BQ))[:, :BQ * n].reshape(-1, BQ, n)
        tabs.append(skew[:, :, BQ - 1:BQ - 1 + win])
    t = jnp.stack(tabs, axis=1)
    h = t.shape[0]
    return t.reshape(h // 2, 2, len(starts), BQ, win).transpose(0, 2, 1, 3, 4).reshape(
        h // 2, len(starts), 2 * BQ, win)


def _pair_gain(g, scale=1.0):
    return jnp.tile(g.astype(F32) * scale, 2)


def kernel(x, p, rel_table, norm_mix_g, w_in, qnorm_a_g, knorm_a_g, qnorm_b_g, knorm_b_g, sink_b,
           w_branch_a, w_branch_b, w_out, norm_ffn_g, w_ffn_gate, w_ffn_up, w_ffn_down,
           norm_ple_g, w_ple_gate, w_ple_proj):
    batch, seq, d = x.shape
    depth = p.shape[0]
    t = batch * seq
    q_scale = HEAD_DIM ** -0.5 * LOG2_E

    table_a = rel_table[:, :N_HEADS_A]
    table_b = rel_table[:, N_HEADS_A:]
    tabs_a = []
    for w, dil in DILATED_PATTERNS:
        sub = seq // dil
        tabs_a.append(_band_tables(table_a, sub, min(2 * BQ, sub), w // (2 * dil), dil))
    tab_b = _band_tables(table_b, seq, BQ + 2 * WINDOW_B, WINDOW_B, 1)

    bf = lambda w: w.astype(BF16)
    rows = lambda g: g.astype(F32).reshape(depth, 1, -1)
    w_in_b, w_a_b, w_b_b, w_o_b = bf(w_in), bf(w_branch_a), bf(w_branch_b), bf(w_out)
    w_gate_b, w_up_b, w_down_b = bf(w_ffn_gate), bf(w_ffn_up), bf(w_ffn_down)
    w_pg_b, w_pp_b = bf(w_ple_gate), bf(w_ple_proj)
    g_mix, g_ffn, g_ple = rows(norm_mix_g), rows(norm_ffn_g), rows(norm_ple_g)
    p2 = p.reshape(depth, t, -1)

    x2 = x.reshape(t, d)
    for l in range(depth):
        gains = jnp.stack([_pair_gain(qnorm_a_g[l], q_scale), _pair_gain(knorm_a_g[l]),
                           _pair_gain(qnorm_b_g[l], q_scale), _pair_gain(knorm_b_g[l])])
        qa, ka, va, qb, kb, vb = _qkv_call(x2, g_mix, w_in_b, gains, l)
        ya = _attn_a_call(qa, ka, va, *tabs_a, batch, seq)
        sink = jnp.repeat(sink_b[l].astype(F32) * LOG2_E, HEAD_DIM).reshape(N_HEADS_B // 2, 1, LANES)
        yb = _attn_b_call(qb, kb, vb, tab_b, sink, batch, seq)
        x2 = _row_call(_post_kernel, "post", [x2, ya, yb, p2],
                       [g_mix, g_ffn, g_ple, w_in_b, w_a_b, w_b_b, w_o_b, w_gate_b, w_up_b, w_down_b,
                        w_pg_b, w_pp_b], l, d,
                       [pltpu.VMEM((TM, d), BF16), pltpu.VMEM((TM, w_ffn_gate.shape[2]), BF16),
                        pltpu.VMEM((TM, d), F32)])
    return x2.reshape(batch, seq, d)
```

```python
import math

import jax
import jax.numpy as jnp
from jax import lax
from jax.experimental import pallas as pl
from jax.experimental.pallas import tpu as pltpu

F32 = jnp.float32
BF16 = jnp.bfloat16

HEAD_DIM = 64
N_HEADS_A = 8
N_HEADS_B = 8
N_KV_B = 2
DILATED_PATTERNS = ((128, 1), (512, 4), (2048, 16))
WINDOW_B = 128
NUM_BUCKETS = 32
MAX_DISTANCE = 1024
RMS_EPS = 1e-6
NEG_INF = -1e30
LOG2_E = math.log2(math.e)

WIDTH_A = N_HEADS_A * HEAD_DIM
WIDTH_BQ = N_HEADS_B * HEAD_DIM
WIDTH_BKV = N_KV_B * HEAD_DIM
GATE_COL = 3 * WIDTH_A + WIDTH_BQ + 2 * WIDTH_BKV

LANES = 128
MXU_COLS = 256
BQ = 128
TM = 512
VMEM_LIMIT = 56 * 1024 * 1024


def _dot(a, b):
    return jnp.dot(a, b, preferred_element_type=F32)


def _rms(x, g):
    ms = jnp.mean(x * x, axis=-1, keepdims=True)
    return x * lax.rsqrt(ms + RMS_EPS) * g


def _low_lanes():
    return lax.broadcasted_iota(jnp.int32, (1, LANES), 1) < HEAD_DIM


def _head_rms(y, gain, lo):
    sq = y * y
    s0 = jnp.sum(jnp.where(lo, sq, 0.0), axis=-1, keepdims=True)
    s1 = jnp.sum(jnp.where(lo, 0.0, sq), axis=-1, keepdims=True)
    ms = jnp.where(lo, s0, s1) * (1.0 / HEAD_DIM)
    return y * lax.rsqrt(ms + RMS_EPS) * gain


def _const_spec(shape):
    nd = len(shape)
    return pl.BlockSpec(shape, lambda *_: (0,) * nd, pipeline_mode=pl.Buffered(1))


def _layer_spec(stacked, layer):
    nd = stacked.ndim - 1
    return pl.BlockSpec((None,) + stacked.shape[1:], lambda *_: (layer,) + (0,) * nd,
                        pipeline_mode=pl.Buffered(1))


def _qkv_kernel(x_ref, g_ref, w_ref, gains_ref, qa_ref, ka_ref, va_ref, qb_ref, kb_ref, vb_ref):
    lo = _low_lanes()
    h = _rms(x_ref[...], g_ref[...]).astype(BF16)
    n_pair = WIDTH_A // MXU_COLS
    order = sorted(range(GATE_COL // MXU_COLS), key=lambda c: 2 * n_pair <= c < 3 * n_pair)
    for c in order:
        y = _dot(h, w_ref[:, c * MXU_COLS:(c + 1) * MXU_COLS])
        y0, y1 = y[:, :LANES], y[:, LANES:]
        if c < n_pair:
            qa_ref[2 * c] = _head_rms(y0, gains_ref[0:1, :], lo)
            qa_ref[2 * c + 1] = _head_rms(y1, gains_ref[0:1, :], lo)
        elif c < 2 * n_pair:
            ka_ref[2 * (c - n_pair)] = _head_rms(y0, gains_ref[1:2, :], lo)
            ka_ref[2 * (c - n_pair) + 1] = _head_rms(y1, gains_ref[1:2, :], lo)
        elif c < 3 * n_pair:
            va_ref[2 * (c - 2 * n_pair)] = y0
            va_ref[2 * (c - 2 * n_pair) + 1] = y1
        elif c < 4 * n_pair:
            qb_ref[2 * (c - 3 * n_pair)] = _head_rms(y0, gains_ref[2:3, :], lo).astype(BF16)
            qb_ref[2 * (c - 3 * n_pair) + 1] = _head_rms(y1, gains_ref[2:3, :], lo).astype(BF16)
        else:
            kn = _head_rms(y0, gains_ref[3:4, :], lo)
            ks = pltpu.roll(kn, HEAD_DIM, axis=1)
            vs = pltpu.roll(y1, HEAD_DIM, axis=1)
            kb_ref[0] = jnp.where(lo, kn, ks).astype(BF16)
            kb_ref[1] = jnp.where(lo, ks, kn).astype(BF16)
            vb_ref[0] = jnp.where(lo, y1, vs).astype(BF16)
            vb_ref[1] = jnp.where(lo, vs, y1).astype(BF16)


def _qkv_call(x2, g, w_in, gains, layer):
    t, d = x2.shape
    n_q = WIDTH_A // LANES
    n_kv = N_KV_B
    assert WIDTH_BQ == WIDTH_A and 2 * WIDTH_BKV == MXU_COLS
    slab = lambda n, dt: jax.ShapeDtypeStruct((n, t, LANES), dt)
    slab_spec = lambda n: pl.BlockSpec((n, TM, LANES), lambda i: (0, i, 0))
    return pl.pallas_call(
        _qkv_kernel,
        grid=(t // TM,),
        in_specs=[pl.BlockSpec((TM, d), lambda i: (i, 0)), _layer_spec(g, layer),
                  _layer_spec(w_in, layer), _const_spec(gains.shape)],
        out_specs=[slab_spec(n_q), slab_spec(n_q), slab_spec(n_q), slab_spec(n_q),
                   slab_spec(n_kv), slab_spec(n_kv)],
        out_shape=[slab(n_q, F32), slab(n_q, F32), slab(n_q, F32), slab(n_q, BF16),
                   slab(n_kv, BF16), slab(n_kv, BF16)],
        compiler_params=pltpu.CompilerParams(dimension_semantics=("arbitrary",),
                                             vmem_limit_bytes=VMEM_LIMIT),
        name="qkv",
    )(x2, g, w_in, gains)


def _attn_block(q, kw, vw, tab, lo):
    zero = jnp.zeros_like(q)
    q2 = jnp.concatenate([jnp.where(lo, q, zero), jnp.where(lo, zero, q)], axis=0).astype(BF16)
    s = lax.dot_general(q2, kw.astype(BF16), (((1,), (1,)), ((), ())),
                        preferred_element_type=F32) + tab
    m = jnp.max(s, axis=-1, keepdims=True)
    p = jnp.exp2(s - m)
    l = jnp.sum(p, axis=-1, keepdims=True)
    o2 = _dot(p.astype(BF16), vw.astype(BF16))
    return o2, m, l


def _merge_heads(a2, lo):
    bq = a2.shape[0] // 2
    return jnp.where(lo, a2[:bq], a2[bq:])


def _block_geometry(i, n_blocks, seq_len, win):
    w0 = max(0, min(i * BQ - (win - BQ) // 2, seq_len - win))
    return pl.ds(i * BQ, BQ), pl.ds(w0, win), (0 if i == 0 else (2 if i == n_blocks - 1 else 1))


def _attn_a_kernel(q_ref, k_ref, v_ref, t1_ref, t4_ref, t16_ref, out_ref,
                   o1, lse1, outf, qs, ks, vs, o4, lse4, o16g, lse16g, o16, lse16, outc):
    lo = _low_lanes()
    seq = q_ref.shape[0]
    win = t1_ref.shape[-1]
    d4 = DILATED_PATTERNS[1][1]
    step = DILATED_PATTERNS[2][1] // d4
    sub4 = seq // d4

    def block(qr, kr, vr, q_rows, kv_rows, tab, o_out, lse_out, out_rows):
        o2, m, l = _attn_block(qr[q_rows, :], kr[kv_rows, :], vr[kv_rows, :], tab, lo)
        l = _merge_heads(l, lo)
        o_out[out_rows, :] = _merge_heads(o2, lo) / l
        lse_out[out_rows, :] = _merge_heads(m, lo) + jnp.log2(l)

    for r in range(d4):
        res = pl.ds(r, sub4, stride=d4)
        qs[r] = q_ref[res, :]
        ks[r] = k_ref[res, :]
        vs[r] = v_ref[res, :]

    nb = seq // BQ
    for i in range(nb):
        q_rows, kv_rows, tidx = _block_geometry(i, nb, seq, win)
        block(q_ref, k_ref, v_ref, q_rows, kv_rows, t1_ref[tidx], o1, lse1, q_rows)

    nb4 = sub4 // BQ
    for i in range(nb4):
        for r in range(d4):
            q_rows, kv_rows, tidx = _block_geometry(i, nb4, sub4, win)
            block(qs.at[r], ks.at[r], vs.at[r], q_rows, kv_rows, t4_ref[tidx], o4.at[r], lse4.at[r], q_rows)

    for b in range(step):
        for r in range(d4):
            rows = pl.ds(b, BQ, stride=step)
            block(qs.at[r], ks.at[r], vs.at[r], rows, rows, t16_ref[0], o16g.at[r], lse16g.at[r],
                  pl.ds(b * BQ, BQ))

    def merge_class(r, carry):
        for grouped, ordered in ((o16g, o16), (lse16g, lse16)):
            for b in range(step):
                ordered[r, pl.ds(b, BQ, stride=step), :] = grouped[r, pl.ds(b * BQ, BQ), :]
        for i in range(nb4):
            rows = pl.ds(i * BQ, BQ)
            nat = pl.ds(r + d4 * i * BQ, BQ, stride=d4)
            la, lb, lc = lse1[nat, :], lse4[r, rows, :], lse16[r, rows, :]
            mx = jnp.maximum(jnp.maximum(la, lb), lc)
            wa, wb, wc = jnp.exp2(la - mx), jnp.exp2(lb - mx), jnp.exp2(lc - mx)
            num = wa * o1[nat, :] + wb * o4[r, rows, :] + wc * o16[r, rows, :]
            outc[r, rows, :] = num / (wa + wb + wc)
        outf[pl.ds(r, sub4, stride=d4), :] = outc[r]
        return carry

    lax.fori_loop(0, d4, merge_class, 0)
    out_ref[...] = outf[...].astype(out_ref.dtype)


def _attn_a_call(qa, ka, va, t1, t4, t16, batch, seq):
    n_slab, t, _ = qa.shape
    assert seq // DILATED_PATTERNS[2][1] == BQ
    qkv_spec = pl.BlockSpec((None, seq, LANES), lambda hp, b: (hp, b, 0))
    tab_spec = lambda tab: pl.BlockSpec((None,) + tab.shape[1:], lambda hp, b: (hp, 0, 0, 0))
    d4 = DILATED_PATTERNS[1][1]
    full = pltpu.VMEM((seq, LANES), F32)
    by_class = pltpu.VMEM((d4, seq // d4, LANES), F32)
    return pl.pallas_call(
        _attn_a_kernel,
        grid=(n_slab, batch),
        in_specs=[qkv_spec, qkv_spec, qkv_spec, tab_spec(t1), tab_spec(t4), tab_spec(t16)],
        out_specs=pl.BlockSpec((seq, LANES), lambda hp, b: (b, hp)),
        out_shape=jax.ShapeDtypeStruct((t, n_slab * LANES), BF16),
        scratch_shapes=[full] * 3 + [by_class] * 10,
        compiler_params=pltpu.CompilerParams(dimension_semantics=("arbitrary", "arbitrary"),
                                             vmem_limit_bytes=VMEM_LIMIT),
        name="attn_a",
    )(qa, ka, va, t1, t4, t16)


def _attn_b_kernel(q_ref, k_ref, v_ref, tab_ref, sink_ref, o_ref):
    lo = _low_lanes()
    n_slab, seq, _ = q_ref.shape
    win = tab_ref.shape[-1]
    nb = seq // BQ
    zero = jnp.zeros((BQ, LANES), q_ref.dtype)

    for i in range(nb):
        q_rows, kv_rows, tidx = _block_geometry(i, nb, seq, win)
        q = [q_ref[sl, q_rows, :] for sl in range(n_slab)]
        q_all = jnp.concatenate([part for qs in q for part in (jnp.where(lo, qs, zero), jnp.where(lo, zero, qs))],
                                axis=0)
        tab = jnp.concatenate([tab_ref[sl, tidx] for sl in range(n_slab)], axis=0)
        s = lax.dot_general(q_all, k_ref[kv_rows, :], (((1,), (1,)), ((), ())), preferred_element_type=F32) + tab
        m = jnp.max(s, axis=-1, keepdims=True)
        p = jnp.exp2(s - m)
        vw = v_ref[kv_rows, :]
        o2 = _dot(p.astype(BF16), jnp.concatenate([vw, jnp.ones(vw.shape, BF16)], axis=1))
        for sl in range(n_slab):
            rows = slice(sl * 2 * BQ, (sl + 1) * 2 * BQ)
            o = _merge_heads(o2[rows, :LANES], lo)
            mm = _merge_heads(m[rows], lo)
            l = _merge_heads(o2[rows, LANES:], lo)
            sink = sink_ref[sl]
            m_new = jnp.maximum(mm, sink)
            scale = jnp.exp2(mm - m_new)
            den = l * scale + jnp.exp2(sink - m_new)
            o_ref[q_rows, sl * LANES:(sl + 1) * LANES] = (o * (scale / den)).astype(o_ref.dtype)


def _attn_b_call(qb, kb, vb, tab, sink, batch, seq):
    n_slab, t, _ = qb.shape
    n_kv = kb.shape[0]
    group = n_slab // n_kv
    kv_spec = pl.BlockSpec((None, seq, LANES), lambda kv, b: (kv, b, 0))
    return pl.pallas_call(
        _attn_b_kernel,
        grid=(n_kv, batch),
        in_specs=[pl.BlockSpec((group, seq, LANES), lambda kv, b: (kv, b, 0)), kv_spec, kv_spec,
                  pl.BlockSpec((group,) + tab.shape[1:], lambda kv, b: (kv, 0, 0, 0)),
                  pl.BlockSpec((group,) + sink.shape[1:], lambda kv, b: (kv, 0, 0))],
        out_specs=pl.BlockSpec((seq, group * LANES), lambda kv, b: (b, kv)),
        out_shape=jax.ShapeDtypeStruct((t, n_slab * LANES), BF16),
        compiler_params=pltpu.CompilerParams(dimension_semantics=("arbitrary", "arbitrary"),
                                             vmem_limit_bytes=VMEM_LIMIT),
        name="attn_b",
    )(qb, kb, vb, tab, sink)


def _row_call(kernel, name, row_inputs, layer_inputs, layer, out_dim, scratch):
    t = row_inputs[0].shape[0]

    def row_spec(a):
        if a.ndim == 2:
            return pl.BlockSpec((TM, a.shape[1]), lambda i: (i, 0))
        return pl.BlockSpec((None, TM, a.shape[2]), lambda i: (layer, i, 0))

    return pl.pallas_call(
        kernel,
        grid=(t // TM,),
        in_specs=[row_spec(a) for a in row_inputs] + [_layer_spec(a, layer) for a in layer_inputs],
        out_specs=pl.BlockSpec((TM, out_dim), lambda i: (i, 0)),
        out_shape=jax.ShapeDtypeStruct((t, out_dim), F32),
        scratch_shapes=scratch,
        compiler_params=pltpu.CompilerParams(dimension_semantics=("arbitrary",),
                                             vmem_limit_bytes=VMEM_LIMIT),
        name=name,
    )(*row_inputs, *layer_inputs)


def _post_kernel(x_ref, ya_ref, yb_ref, p_ref, gmix_ref, gffn_ref, gple_ref, win_ref, wa_ref, wb_ref, wo_ref,
                 wgate_ref, wup_ref, wdown_ref, wpg_ref, wpp_ref, o_ref, merged, hid, xs):
    d = x_ref.shape[1]
    chunks = [slice(c * MXU_COLS, (c + 1) * MXU_COLS) for c in range(d // MXU_COLS)]

    h = _rms(x_ref[...], gmix_ref[...]).astype(BF16)
    ya = ya_ref[...]
    yb = yb_ref[...]
    for c, cs in enumerate(chunks):
        ga = _dot(h, win_ref[:, GATE_COL + c * MXU_COLS:GATE_COL + (c + 1) * MXU_COLS])
        gb = _dot(h, win_ref[:, GATE_COL + d + c * MXU_COLS:GATE_COL + d + (c + 1) * MXU_COLS])
        pa = _dot(ya, wa_ref[:, cs])
        pb = _dot(yb, wb_ref[:, cs])
        merged[:, cs] = (jax.nn.sigmoid(ga) * pa + jax.nn.sigmoid(gb) * pb).astype(BF16)
    mg = merged[...]
    for cs in chunks:
        xs[:, cs] = x_ref[:, cs] + _dot(mg, wo_ref[:, cs])

    h = _rms(xs[...], gffn_ref[...]).astype(BF16)
    for c in range(wgate_ref.shape[1] // MXU_COLS):
        cs = slice(c * MXU_COLS, (c + 1) * MXU_COLS)
        a = _dot(h, wgate_ref[:, cs])
        u = _dot(h, wup_ref[:, cs])
        hid[:, cs] = (a * jax.nn.sigmoid(a) * u).astype(BF16)
    hv = hid[...]
    for cs in chunks:
        xs[:, cs] = xs[:, cs] + _dot(hv, wdown_ref[:, cs])

    h = _rms(xs[...], gple_ref[...]).astype(BF16)
    pv = p_ref[...].astype(BF16)
    for cs in chunks:
        o_ref[:, cs] = xs[:, cs] + jax.nn.sigmoid(_dot(h, wpg_ref[:, cs])) * _dot(pv, wpp_ref[:, cs])


def _t5_bucket(rel):
    half_b = NUM_BUCKETS // 2
    max_exact = half_b // 2
    sign = jnp.where(rel > 0, half_b, 0)
    n = jnp.abs(rel)
    nf = jnp.maximum(n, 1).astype(F32)
    large = max_exact + (jnp.log(nf / max_exact) / math.log(MAX_DISTANCE / max_exact)
                         * (half_b - max_exact)).astype(jnp.int32)
    large = jnp.minimum(large, half_b - 1)
    return sign + jnp.where(n < max_exact, n, large)


TABLE_PERIOD = 512


def _block_starts(seq_len, win):
    nb = seq_len // BQ
    blocks = (0, min(1, nb - 1), nb - 1) if nb > 1 else (0,)
    return [max(0, min(i * BQ - (win - BQ) // 2, seq_len - win)) - i * BQ for i in blocks]


def _rel_rows(table, starts, win, half, dilation):
    assert BQ + win - 1 <= TABLE_PERIOD
    c = jnp.arange(TABLE_PERIOD, dtype=jnp.int32)
    d = jnp.where(c < win, c, c - TABLE_PERIOD)
    rel = jnp.asarray(starts, jnp.int32)[:, None] + d[None, :]
    bias = table[_t5_bucket(rel * dilation)].astype(F32) * LOG2_E
    return jnp.transpose(jnp.where((jnp.abs(rel) <= half)[:, :, None], bias, NEG_INF), (0, 2, 1))


def _tables_kernel(rows_ref, *out_refs):
    v = 0
    for out_ref in out_refs:
        n_pair, n_var, _, win = out_ref.shape
        for t in range(n_var):
            for h in range(2 * n_pair):
                row = jnp.broadcast_to(rows_ref[v + t, h:h + 1, :], (BQ, TABLE_PERIOD))
                skew = pltpu.roll(row, 0, 1, stride=1, stride_axis=0)
                out_ref[h // 2, t, (h % 2) * BQ:(h % 2 + 1) * BQ, :] = skew[:, :win]
        v += n_var


def _band_tables(rows_and_wins, n_heads):
    rows = jnp.concatenate([r for r, _ in rows_and_wins], axis=0)
    shapes = [jax.ShapeDtypeStruct((n_heads // 2, r.shape[0], 2 * BQ, win), F32) for r, win in rows_and_wins]
    return pl.pallas_call(_tables_kernel, out_shape=shapes, name="tables",
                          compiler_params=pltpu.CompilerParams(vmem_limit_bytes=VMEM_LIMIT))(rows)


def _pair_gain(g, scale=1.0):
    return jnp.tile(g.astype(F32) * scale, 2)


def kernel(x, p, rel_table, norm_mix_g, w_in, qnorm_a_g, knorm_a_g, qnorm_b_g, knorm_b_g, sink_b,
           w_branch_a, w_branch_b, w_out, norm_ffn_g, w_ffn_gate, w_ffn_up, w_ffn_down,
           norm_ple_g, w_ple_gate, w_ple_proj):
    batch, seq, d = x.shape
    depth = p.shape[0]
    t = batch * seq
    q_scale = HEAD_DIM ** -0.5 * LOG2_E

    table_a = rel_table[:, :N_HEADS_A]
    table_b = rel_table[:, N_HEADS_A:]
    patterns = [(seq // dil, min(2 * BQ, seq // dil), w // (2 * dil), dil, table_a) for w, dil in DILATED_PATTERNS]
    patterns.append((seq, BQ + 2 * WINDOW_B, WINDOW_B, 1, table_b))
    rows = [(_rel_rows(tab, _block_starts(sub, win), win, half, dil), win) for sub, win, half, dil, tab in patterns]
    *tabs_a, tab_b = _band_tables(rows, N_HEADS_A)

    bf = lambda w: w.astype(BF16)
    rows = lambda g: g.astype(F32).reshape(depth, 1, -1)
    w_in_b, w_a_b, w_b_b, w_o_b = bf(w_in), bf(w_branch_a), bf(w_branch_b), bf(w_out)
    w_gate_b, w_up_b, w_down_b = bf(w_ffn_gate), bf(w_ffn_up), bf(w_ffn_down)
    w_pg_b, w_pp_b = bf(w_ple_gate), bf(w_ple_proj)
    g_mix, g_ffn, g_ple = rows(norm_mix_g), rows(norm_ffn_g), rows(norm_ple_g)
    p2 = p.reshape(depth, t, -1)

    x2 = x.reshape(t, d)
    for l in range(depth):
        gains = jnp.stack([_pair_gain(qnorm_a_g[l], q_scale), _pair_gain(knorm_a_g[l]),
                           _pair_gain(qnorm_b_g[l], q_scale), _pair_gain(knorm_b_g[l])])
        qa, ka, va, qb, kb, vb = _qkv_call(x2, g_mix, w_in_b, gains, l)
        ya = _attn_a_call(qa, ka, va, *tabs_a, batch, seq)
        sink = jnp.repeat(sink_b[l].astype(F32) * LOG2_E, HEAD_DIM).reshape(N_HEADS_B // 2, 1, LANES)
        yb = _attn_b_call(qb, kb, vb, tab_b, sink, batch, seq)
        x2 = _row_call(_post_kernel, "post", [x2, ya, yb, p2],
                       [g_mix, g_ffn, g_ple, w_in_b, w_a_b, w_b_b, w_o_b, w_gate_b, w_up_b, w_down_b,
                        w_pg_b, w_pp_b], l, d,
                       [pltpu.VMEM((TM, d), BF16), pltpu.VMEM((TM, w_ffn_gate.shape[2]), BF16),
                        pltpu.VMEM((TM, d), F32)])
    return x2.reshape(batch, seq, d)
```

```python
import math

import jax
import jax.numpy as jnp
from jax import lax
from jax.experimental import pallas as pl
from jax.experimental.pallas import tpu as pltpu

F32 = jnp.float32
BF16 = jnp.bfloat16

HEAD_DIM = 64
N_HEADS_A = 8
N_HEADS_B = 8
N_KV_B = 2
DILATED_PATTERNS = ((128, 1), (512, 4), (2048, 16))
WINDOW_B = 128
NUM_BUCKETS = 32
MAX_DISTANCE = 1024
RMS_EPS = 1e-6
NEG_INF = -1e30
LOG2_E = math.log2(math.e)

WIDTH_A = N_HEADS_A * HEAD_DIM
WIDTH_BQ = N_HEADS_B * HEAD_DIM
WIDTH_BKV = N_KV_B * HEAD_DIM
GATE_COL = 3 * WIDTH_A + WIDTH_BQ + 2 * WIDTH_BKV

LANES = 128
MXU_COLS = 256
BQ = 128
TM = 512
TM_QKV = 1024
VMEM_LIMIT = 56 * 1024 * 1024


def _dot(a, b):
    return jnp.dot(a, b, preferred_element_type=F32)


def _rms(x, g):
    ms = jnp.mean(x * x, axis=-1, keepdims=True)
    return x * lax.rsqrt(ms + RMS_EPS) * g


def _low_lanes():
    return lax.broadcasted_iota(jnp.int32, (1, LANES), 1) < HEAD_DIM


def _head_rms(y, gain, lo):
    sq = y * y
    s0 = jnp.sum(jnp.where(lo, sq, 0.0), axis=-1, keepdims=True)
    s1 = jnp.sum(jnp.where(lo, 0.0, sq), axis=-1, keepdims=True)
    ms = jnp.where(lo, s0, s1) * (1.0 / HEAD_DIM)
    return y * lax.rsqrt(ms + RMS_EPS) * gain


def _const_spec(shape):
    nd = len(shape)
    return pl.BlockSpec(shape, lambda *_: (0,) * nd, pipeline_mode=pl.Buffered(1))


def _layer_spec(stacked, layer):
    nd = stacked.ndim - 1
    return pl.BlockSpec((None,) + stacked.shape[1:], lambda *_: (layer,) + (0,) * nd,
                        pipeline_mode=pl.Buffered(1))


def _qkv_kernel(x_ref, g_ref, w_ref, gains_ref, qa_ref, ka_ref, va_ref, qb_ref, kb_ref, vb_ref):
    lo = _low_lanes()
    h = _rms(x_ref[...], g_ref[...]).astype(BF16)
    n_pair = WIDTH_A // MXU_COLS
    order = sorted(range(GATE_COL // MXU_COLS), key=lambda c: 2 * n_pair <= c < 3 * n_pair)
    for c in order:
        y = _dot(h, w_ref[:, c * MXU_COLS:(c + 1) * MXU_COLS])
        y0, y1 = y[:, :LANES], y[:, LANES:]
        if c < n_pair:
            qa_ref[2 * c] = _head_rms(y0, gains_ref[0:1, :], lo)
            qa_ref[2 * c + 1] = _head_rms(y1, gains_ref[0:1, :], lo)
        elif c < 2 * n_pair:
            ka_ref[2 * (c - n_pair)] = _head_rms(y0, gains_ref[1:2, :], lo)
            ka_ref[2 * (c - n_pair) + 1] = _head_rms(y1, gains_ref[1:2, :], lo)
        elif c < 3 * n_pair:
            va_ref[2 * (c - 2 * n_pair)] = y0
            va_ref[2 * (c - 2 * n_pair) + 1] = y1
        elif c < 4 * n_pair:
            qb_ref[2 * (c - 3 * n_pair)] = _head_rms(y0, gains_ref[2:3, :], lo).astype(BF16)
            qb_ref[2 * (c - 3 * n_pair) + 1] = _head_rms(y1, gains_ref[2:3, :], lo).astype(BF16)
        else:
            kn = _head_rms(y0, gains_ref[3:4, :], lo)
            ks = pltpu.roll(kn, HEAD_DIM, axis=1)
            vs = pltpu.roll(y1, HEAD_DIM, axis=1)
            kb_ref[0] = jnp.where(lo, kn, ks).astype(BF16)
            kb_ref[1] = jnp.where(lo, ks, kn).astype(BF16)
            vb_ref[0] = jnp.where(lo, y1, vs).astype(BF16)
            vb_ref[1] = jnp.where(lo, vs, y1).astype(BF16)


def _qkv_call(x2, g, w_in, gains, layer):
    t, d = x2.shape
    n_q = WIDTH_A // LANES
    n_kv = N_KV_B
    assert WIDTH_BQ == WIDTH_A and 2 * WIDTH_BKV == MXU_COLS
    slab = lambda n, dt: jax.ShapeDtypeStruct((n, t, LANES), dt)
    slab_spec = lambda n: pl.BlockSpec((n, TM_QKV, LANES), lambda i: (0, i, 0))
    return pl.pallas_call(
        _qkv_kernel,
        grid=(t // TM_QKV,),
        in_specs=[pl.BlockSpec((TM_QKV, d), lambda i: (i, 0)), _layer_spec(g, layer),
                  _layer_spec(w_in, layer), _const_spec(gains.shape)],
        out_specs=[slab_spec(n_q), slab_spec(n_q), slab_spec(n_q), slab_spec(n_q),
                   slab_spec(n_kv), slab_spec(n_kv)],
        out_shape=[slab(n_q, F32), slab(n_q, F32), slab(n_q, F32), slab(n_q, BF16),
                   slab(n_kv, BF16), slab(n_kv, BF16)],
        compiler_params=pltpu.CompilerParams(dimension_semantics=("arbitrary",),
                                             vmem_limit_bytes=VMEM_LIMIT),
        name="qkv",
    )(x2, g, w_in, gains)


def _attn_block(q, kw, vw, tab, lo):
    zero = jnp.zeros_like(q)
    q2 = jnp.concatenate([jnp.where(lo, q, zero), jnp.where(lo, zero, q)], axis=0).astype(BF16)
    s = lax.dot_general(q2, kw.astype(BF16), (((1,), (1,)), ((), ())),
                        preferred_element_type=F32) + tab
    m = jnp.max(s, axis=-1, keepdims=True)
    p = jnp.exp2(s - m)
    l = jnp.sum(p, axis=-1, keepdims=True)
    o2 = _dot(p.astype(BF16), vw.astype(BF16))
    return o2, m, l


def _merge_heads(a2, lo):
    bq = a2.shape[0] // 2
    return jnp.where(lo, a2[:bq], a2[bq:])


def _block_geometry(i, n_blocks, seq_len, win):
    w0 = max(0, min(i * BQ - (win - BQ) // 2, seq_len - win))
    return pl.ds(i * BQ, BQ), pl.ds(w0, win), (0 if i == 0 else (2 if i == n_blocks - 1 else 1))


def _attn_a_kernel(q_ref, k_ref, v_ref, t1_ref, t4_ref, t16_ref, out_ref,
                   o1, lse1, outf, qs, ks, vs, o4, lse4, o16g, lse16g, o16, lse16, outc):
    lo = _low_lanes()
    seq = q_ref.shape[0]
    win = t1_ref.shape[-1]
    d4 = DILATED_PATTERNS[1][1]
    step = DILATED_PATTERNS[2][1] // d4
    sub4 = seq // d4

    def block(qr, kr, vr, q_rows, kv_rows, tab, o_out, lse_out, out_rows):
        o2, m, l = _attn_block(qr[q_rows, :], kr[kv_rows, :], vr[kv_rows, :], tab, lo)
        l = _merge_heads(l, lo)
        o_out[out_rows, :] = _merge_heads(o2, lo) / l
        lse_out[out_rows, :] = _merge_heads(m, lo) + jnp.log2(l)

    for r in range(d4):
        res = pl.ds(r, sub4, stride=d4)
        qs[r] = q_ref[res, :]
        ks[r] = k_ref[res, :]
        vs[r] = v_ref[res, :]

    nb = seq // BQ
    for i in range(nb):
        q_rows, kv_rows, tidx = _block_geometry(i, nb, seq, win)
        block(q_ref, k_ref, v_ref, q_rows, kv_rows, t1_ref[tidx], o1, lse1, q_rows)

    nb4 = sub4 // BQ
    for i in range(nb4):
        for r in range(d4):
            q_rows, kv_rows, tidx = _block_geometry(i, nb4, sub4, win)
            block(qs.at[r], ks.at[r], vs.at[r], q_rows, kv_rows, t4_ref[tidx], o4.at[r], lse4.at[r], q_rows)

    for b in range(step):
        for r in range(d4):
            rows = pl.ds(b, BQ, stride=step)
            block(qs.at[r], ks.at[r], vs.at[r], rows, rows, t16_ref[0], o16g.at[r], lse16g.at[r],
                  pl.ds(b * BQ, BQ))

    def merge_class(r, carry):
        for grouped, ordered in ((o16g, o16), (lse16g, lse16)):
            for b in range(step):
                ordered[r, pl.ds(b, BQ, stride=step), :] = grouped[r, pl.ds(b * BQ, BQ), :]
        for i in range(nb4):
            rows = pl.ds(i * BQ, BQ)
            nat = pl.ds(r + d4 * i * BQ, BQ, stride=d4)
            la, lb, lc = lse1[nat, :], lse4[r, rows, :], lse16[r, rows, :]
            mx = jnp.maximum(jnp.maximum(la, lb), lc)
            wa, wb, wc = jnp.exp2(la - mx), jnp.exp2(lb - mx), jnp.exp2(lc - mx)
            num = wa * o1[nat, :] + wb * o4[r, rows, :] + wc * o16[r, rows, :]
            outc[r, rows, :] = num / (wa + wb + wc)
        outf[pl.ds(r, sub4, stride=d4), :] = outc[r]
        return carry

    lax.fori_loop(0, d4, merge_class, 0)
    out_ref[...] = outf[...].astype(out_ref.dtype)


def _attn_a_call(qa, ka, va, t1, t4, t16, batch, seq):
    n_slab, t, _ = qa.shape
    assert seq // DILATED_PATTERNS[2][1] == BQ
    qkv_spec = pl.BlockSpec((None, seq, LANES), lambda hp, b: (hp, b, 0))
    tab_spec = lambda tab: pl.BlockSpec((None,) + tab.shape[1:], lambda hp, b: (hp, 0, 0, 0))
    d4 = DILATED_PATTERNS[1][1]
    full = pltpu.VMEM((seq, LANES), F32)
    by_class = pltpu.VMEM((d4, seq // d4, LANES), F32)
    return pl.pallas_call(
        _attn_a_kernel,
        grid=(n_slab, batch),
        in_specs=[qkv_spec, qkv_spec, qkv_spec, tab_spec(t1), tab_spec(t4), tab_spec(t16)],
        out_specs=pl.BlockSpec((seq, LANES), lambda hp, b: (b, hp)),
        out_shape=jax.ShapeDtypeStruct((t, n_slab * LANES), BF16),
        scratch_shapes=[full] * 3 + [by_class] * 10,
        compiler_params=pltpu.CompilerParams(dimension_semantics=("arbitrary", "arbitrary"),
                                             vmem_limit_bytes=VMEM_LIMIT),
        name="attn_a",
    )(qa, ka, va, t1, t4, t16)


def _attn_b_kernel(q_ref, k_ref, v_ref, tab_ref, sink_ref, o_ref):
    lo = _low_lanes()
    n_slab, seq, _ = q_ref.shape
    win = tab_ref.shape[-1]
    nb = seq // BQ
    zero = jnp.zeros((BQ, LANES), q_ref.dtype)

    for i in range(nb):
        q_rows, kv_rows, tidx = _block_geometry(i, nb, seq, win)
        q = [q_ref[sl, q_rows, :] for sl in range(n_slab)]
        q_all = jnp.concatenate([part for qs in q for part in (jnp.where(lo, qs, zero), jnp.where(lo, zero, qs))],
                                axis=0)
        tab = jnp.concatenate([tab_ref[sl, tidx] for sl in range(n_slab)], axis=0)
        s = lax.dot_general(q_all, k_ref[kv_rows, :], (((1,), (1,)), ((), ())), preferred_element_type=F32) + tab
        m = jnp.max(s, axis=-1, keepdims=True)
        p = jnp.exp2(s - m)
        vw = v_ref[kv_rows, :]
        o2 = _dot(p.astype(BF16), jnp.concatenate([vw, jnp.ones(vw.shape, BF16)], axis=1))
        for sl in range(n_slab):
            rows = slice(sl * 2 * BQ, (sl + 1) * 2 * BQ)
            o = _merge_heads(o2[rows, :LANES], lo)
            mm = _merge_heads(m[rows], lo)
            l = _merge_heads(o2[rows, LANES:], lo)
            sink = sink_ref[sl]
            m_new = jnp.maximum(mm, sink)
            scale = jnp.exp2(mm - m_new)
            den = l * scale + jnp.exp2(sink - m_new)
            o_ref[q_rows, sl * LANES:(sl + 1) * LANES] = (o * (scale / den)).astype(o_ref.dtype)


def _attn_b_call(qb, kb, vb, tab, sink, batch, seq):
    n_slab, t, _ = qb.shape
    n_kv = kb.shape[0]
    group = n_slab // n_kv
    kv_spec = pl.BlockSpec((None, seq, LANES), lambda kv, b: (kv, b, 0))
    return pl.pallas_call(
        _attn_b_kernel,
        grid=(n_kv, batch),
        in_specs=[pl.BlockSpec((group, seq, LANES), lambda kv, b: (kv, b, 0)), kv_spec, kv_spec,
                  pl.BlockSpec((group,) + tab.shape[1:], lambda kv, b: (kv, 0, 0, 0)),
                  pl.BlockSpec((group,) + sink.shape[1:], lambda kv, b: (kv, 0, 0))],
        out_specs=pl.BlockSpec((seq, group * LANES), lambda kv, b: (b, kv)),
        out_shape=jax.ShapeDtypeStruct((t, n_slab * LANES), BF16),
        compiler_params=pltpu.CompilerParams(dimension_semantics=("arbitrary", "arbitrary"),
                                             vmem_limit_bytes=VMEM_LIMIT),
        name="attn_b",
    )(qb, kb, vb, tab, sink)


def _row_call(kernel, name, row_inputs, layer_inputs, layer, out_dim, scratch):
    t = row_inputs[0].shape[0]

    def row_spec(a):
        if a.ndim == 2:
            return pl.BlockSpec((TM, a.shape[1]), lambda i: (i, 0))
        return pl.BlockSpec((None, TM, a.shape[2]), lambda i: (layer, i, 0))

    return pl.pallas_call(
        kernel,
        grid=(t // TM,),
        in_specs=[row_spec(a) for a in row_inputs] + [_layer_spec(a, layer) for a in layer_inputs],
        out_specs=pl.BlockSpec((TM, out_dim), lambda i: (i, 0)),
        out_shape=jax.ShapeDtypeStruct((t, out_dim), F32),
        scratch_shapes=scratch,
        compiler_params=pltpu.CompilerParams(dimension_semantics=("arbitrary",),
                                             vmem_limit_bytes=VMEM_LIMIT),
        name=name,
    )(*row_inputs, *layer_inputs)


def _post_kernel(x_ref, ya_ref, yb_ref, p_ref, gmix_ref, gffn_ref, gple_ref, win_ref, wa_ref, wb_ref, wo_ref,
                 wgate_ref, wup_ref, wdown_ref, wpg_ref, wpp_ref, o_ref, merged, hid, xs):
    d = x_ref.shape[1]
    chunks = [slice(c * MXU_COLS, (c + 1) * MXU_COLS) for c in range(d // MXU_COLS)]

    h = _rms(x_ref[...], gmix_ref[...]).astype(BF16)
    ya = ya_ref[...]
    yb = yb_ref[...]
    for c, cs in enumerate(chunks):
        ga = _dot(h, win_ref[:, GATE_COL + c * MXU_COLS:GATE_COL + (c + 1) * MXU_COLS])
        gb = _dot(h, win_ref[:, GATE_COL + d + c * MXU_COLS:GATE_COL + d + (c + 1) * MXU_COLS])
        pa = _dot(ya, wa_ref[:, cs])
        pb = _dot(yb, wb_ref[:, cs])
        merged[:, cs] = (jax.nn.sigmoid(ga) * pa + jax.nn.sigmoid(gb) * pb).astype(BF16)
    mg = merged[...]
    for cs in chunks:
        xs[:, cs] = x_ref[:, cs] + _dot(mg, wo_ref[:, cs])

    h = _rms(xs[...], gffn_ref[...]).astype(BF16)
    for c in range(wgate_ref.shape[1] // MXU_COLS):
        cs = slice(c * MXU_COLS, (c + 1) * MXU_COLS)
        a = _dot(h, wgate_ref[:, cs])
        u = _dot(h, wup_ref[:, cs])
        hid[:, cs] = (a * jax.nn.sigmoid(a) * u).astype(BF16)
    hv = hid[...]
    for cs in chunks:
        xs[:, cs] = xs[:, cs] + _dot(hv, wdown_ref[:, cs])

    h = _rms(xs[...], gple_ref[...]).astype(BF16)
    pv = p_ref[...].astype(BF16)
    for cs in chunks:
        o_ref[:, cs] = xs[:, cs] + jax.nn.sigmoid(_dot(h, wpg_ref[:, cs])) * _dot(pv, wpp_ref[:, cs])


def _t5_bucket(rel):
    half_b = NUM_BUCKETS // 2
    max_exact = half_b // 2
    sign = jnp.where(rel > 0, half_b, 0)
    n = jnp.abs(rel)
    nf = jnp.maximum(n, 1).astype(F32)
    large = max_exact + (jnp.log(nf / max_exact) / math.log(MAX_DISTANCE / max_exact)
                         * (half_b - max_exact)).astype(jnp.int32)
    large = jnp.minimum(large, half_b - 1)
    return sign + jnp.where(n < max_exact, n, large)


TABLE_PERIOD = 512


def _block_starts(seq_len, win):
    nb = seq_len // BQ
    blocks = (0, min(1, nb - 1), nb - 1) if nb > 1 else (0,)
    return [max(0, min(i * BQ - (win - BQ) // 2, seq_len - win)) - i * BQ for i in blocks]


def _rel_rows(table, starts, win, half, dilation):
    assert BQ + win - 1 <= TABLE_PERIOD
    c = jnp.arange(TABLE_PERIOD, dtype=jnp.int32)
    d = jnp.where(c < win, c, c - TABLE_PERIOD)
    rel = jnp.asarray(starts, jnp.int32)[:, None] + d[None, :]
    bias = table[_t5_bucket(rel * dilation)].astype(F32) * LOG2_E
    return jnp.transpose(jnp.where((jnp.abs(rel) <= half)[:, :, None], bias, NEG_INF), (0, 2, 1))


def _tables_kernel(rows_ref, *out_refs):
    v = 0
    for out_ref in out_refs:
        n_pair, n_var, _, win = out_ref.shape
        for t in range(n_var):
            for h in range(2 * n_pair):
                row = jnp.broadcast_to(rows_ref[v + t, h:h + 1, :], (BQ, TABLE_PERIOD))
                skew = pltpu.roll(row, 0, 1, stride=1, stride_axis=0)
                out_ref[h // 2, t, (h % 2) * BQ:(h % 2 + 1) * BQ, :] = skew[:, :win]
        v += n_var


def _band_tables(rows_and_wins, n_heads):
    rows = jnp.concatenate([r for r, _ in rows_and_wins], axis=0)
    shapes = [jax.ShapeDtypeStruct((n_heads // 2, r.shape[0], 2 * BQ, win), F32) for r, win in rows_and_wins]
    return pl.pallas_call(_tables_kernel, out_shape=shapes, name="tables",
                          compiler_params=pltpu.CompilerParams(vmem_limit_bytes=VMEM_LIMIT))(rows)


def _pair_gain(g, scale=1.0):
    return jnp.tile(g.astype(F32) * scale, 2)


def kernel(x, p, rel_table, norm_mix_g, w_in, qnorm_a_g, knorm_a_g, qnorm_b_g, knorm_b_g, sink_b,
           w_branch_a, w_branch_b, w_out, norm_ffn_g, w_ffn_gate, w_ffn_up, w_ffn_down,
           norm_ple_g, w_ple_gate, w_ple_proj):
    batch, seq, d = x.shape
    depth = p.shape[0]
    t = batch * seq
    q_scale = HEAD_DIM ** -0.5 * LOG2_E

    table_a = rel_table[:, :N_HEADS_A]
    table_b = rel_table[:, N_HEADS_A:]
    patterns = [(seq // dil, min(2 * BQ, seq // dil), w // (2 * dil), dil, table_a) for w, dil in DILATED_PATTERNS]
    patterns.append((seq, BQ + 2 * WINDOW_B, WINDOW_B, 1, table_b))
    rows = [(_rel_rows(tab, _block_starts(sub, win), win, half, dil), win) for sub, win, half, dil, tab in patterns]
    *tabs_a, tab_b = _band_tables(rows, N_HEADS_A)

    bf = lambda w: w.astype(BF16)
    rows = lambda g: g.astype(F32).reshape(depth, 1, -1)
    w_in_b, w_a_b, w_b_b, w_o_b = bf(w_in), bf(w_branch_a), bf(w_branch_b), bf(w_out)
    w_gate_b, w_up_b, w_down_b = bf(w_ffn_gate), bf(w_ffn_up), bf(w_ffn_down)
    w_pg_b, w_pp_b = bf(w_ple_gate), bf(w_ple_proj)
    g_mix, g_ffn, g_ple = rows(norm_mix_g), rows(norm_ffn_g), rows(norm_ple_g)
    p2 = p.reshape(depth, t, -1)

    x2 = x.reshape(t, d)
    for l in range(depth):
        gains = jnp.stack([_pair_gain(qnorm_a_g[l], q_scale), _pair_gain(knorm_a_g[l]),
                           _pair_gain(qnorm_b_g[l], q_scale), _pair_gain(knorm_b_g[l])])
        qa, ka, va, qb, kb, vb = _qkv_call(x2, g_mix, w_in_b, gains, l)
        ya = _attn_a_call(qa, ka, va, *tabs_a, batch, seq)
        sink = jnp.repeat(sink_b[l].astype(F32) * LOG2_E, HEAD_DIM).reshape(N_HEADS_B // 2, 1, LANES)
        yb = _attn_b_call(qb, kb, vb, tab_b, sink, batch, seq)
        x2 = _row_call(_post_kernel, "post", [x2, ya, yb, p2],
                       [g_mix, g_ffn, g_ple, w_in_b, w_a_b, w_b_b, w_o_b, w_gate_b, w_up_b, w_down_b,
                        w_pg_b, w_pp_b], l, d,
                       [pltpu.VMEM((TM, d), BF16), pltpu.VMEM((TM, w_ffn_gate.shape[2]), BF16),
                        pltpu.VMEM((TM, d), F32)])
    return x2.reshape(batch, seq, d)
```

```python
import math

import jax
import jax.numpy as jnp
from jax import lax
from jax.experimental import pallas as pl
from jax.experimental.pallas import tpu as pltpu

F32 = jnp.float32
BF16 = jnp.bfloat16

HEAD_DIM = 64
N_HEADS_A = 8
N_HEADS_B = 8
N_KV_B = 2
DILATED_PATTERNS = ((128, 1), (512, 4), (2048, 16))
WINDOW_B = 128
NUM_BUCKETS = 32
MAX_DISTANCE = 1024
RMS_EPS = 1e-6
NEG_INF = -1e30
LOG2_E = math.log2(math.e)

WIDTH_A = N_HEADS_A * HEAD_DIM
WIDTH_BQ = N_HEADS_B * HEAD_DIM
WIDTH_BKV = N_KV_B * HEAD_DIM
GATE_COL = 3 * WIDTH_A + WIDTH_BQ + 2 * WIDTH_BKV

LANES = 128
MXU_COLS = 256
BF16_SUBLANES = 16
BQ = 128
TM = 512
VMEM_LIMIT = 56 * 1024 * 1024


def _dot(a, b):
    return jnp.dot(a, b, preferred_element_type=F32)


def _rms(x, g):
    ms = jnp.mean(x * x, axis=-1, keepdims=True)
    return x * lax.rsqrt(ms + RMS_EPS) * g


def _low_lanes():
    return lax.broadcasted_iota(jnp.int32, (1, LANES), 1) < HEAD_DIM


def _head_rms(y, gain, lo):
    sq = y * y
    s0 = jnp.sum(jnp.where(lo, sq, 0.0), axis=-1, keepdims=True)
    s1 = jnp.sum(jnp.where(lo, 0.0, sq), axis=-1, keepdims=True)
    ms = jnp.where(lo, s0, s1) * (1.0 / HEAD_DIM)
    return y * lax.rsqrt(ms + RMS_EPS) * gain


def _const_spec(shape):
    nd = len(shape)
    return pl.BlockSpec(shape, lambda *_: (0,) * nd, pipeline_mode=pl.Buffered(1))


def _layer_spec(stacked, layer):
    nd = stacked.ndim - 1
    return pl.BlockSpec((None,) + stacked.shape[1:], lambda *_: (layer,) + (0,) * nd,
                        pipeline_mode=pl.Buffered(1))


def _qkv_kernel(x_ref, g_ref, w_ref, gains_ref, *refs):
    n_cast = (len(refs) - 6) // 2
    cast_src, cast_dst = refs[:n_cast], refs[n_cast + 6:]
    qa_ref, ka_ref, va_ref, qb_ref, kb_ref, vb_ref = refs[n_cast:n_cast + 6]
    for src, dst in zip(cast_src, cast_dst):
        dst[...] = src[...].astype(BF16)
    lo = _low_lanes()
    h = _rms(x_ref[...], g_ref[...]).astype(BF16)
    n_pair = WIDTH_A // MXU_COLS
    order = sorted(range(GATE_COL // MXU_COLS), key=lambda c: 2 * n_pair <= c < 3 * n_pair)
    for c in order:
        y = _dot(h, w_ref[:, c * MXU_COLS:(c + 1) * MXU_COLS])
        y0, y1 = y[:, :LANES], y[:, LANES:]
        if c < n_pair:
            qa_ref[2 * c] = _head_rms(y0, gains_ref[0:1, :], lo)
            qa_ref[2 * c + 1] = _head_rms(y1, gains_ref[0:1, :], lo)
        elif c < 2 * n_pair:
            ka_ref[2 * (c - n_pair)] = _head_rms(y0, gains_ref[1:2, :], lo)
            ka_ref[2 * (c - n_pair) + 1] = _head_rms(y1, gains_ref[1:2, :], lo)
        elif c < 3 * n_pair:
            va_ref[2 * (c - 2 * n_pair)] = y0
            va_ref[2 * (c - 2 * n_pair) + 1] = y1
        elif c < 4 * n_pair:
            qb_ref[2 * (c - 3 * n_pair)] = _head_rms(y0, gains_ref[2:3, :], lo).astype(BF16)
            qb_ref[2 * (c - 3 * n_pair) + 1] = _head_rms(y1, gains_ref[2:3, :], lo).astype(BF16)
        else:
            kn = _head_rms(y0, gains_ref[3:4, :], lo)
            ks = pltpu.roll(kn, HEAD_DIM, axis=1)
            vs = pltpu.roll(y1, HEAD_DIM, axis=1)
            kb_ref[0] = jnp.where(lo, kn, ks).astype(BF16)
            kb_ref[1] = jnp.where(lo, ks, kn).astype(BF16)
            vb_ref[0] = jnp.where(lo, y1, vs).astype(BF16)
            vb_ref[1] = jnp.where(lo, vs, y1).astype(BF16)


def _cast_blocks(n_rows, n_steps):
    for n_blocks in range(n_steps, 0, -1):
        if n_steps % n_blocks == 0 and n_rows % (n_blocks * BF16_SUBLANES) == 0:
            return n_rows // n_blocks, n_steps // n_blocks
    raise ValueError(f"cannot split {n_rows} rows into bf16 row blocks over {n_steps} steps")


def _qkv_call(x2, g, w_in_layer, gains, layer, to_cast):
    t, d = x2.shape
    n_q = WIDTH_A // LANES
    n_kv = N_KV_B
    n_steps = t // TM
    assert WIDTH_BQ == WIDTH_A and 2 * WIDTH_BKV == MXU_COLS
    slab = lambda n, dt: jax.ShapeDtypeStruct((n, t, LANES), dt)
    slab_spec = lambda n: pl.BlockSpec((n, TM, LANES), lambda i: (0, i, 0))
    cast_in_specs, cast_out_specs, cast_shapes = [], [], []
    for w, w_layer in to_cast:
        _, n_rows, n_cols = w.shape
        rb, hold = _cast_blocks(n_rows, n_steps)
        cast_in_specs.append(pl.BlockSpec((None, rb, n_cols), lambda i, w_layer=w_layer, hold=hold: (w_layer, i // hold, 0)))
        cast_out_specs.append(pl.BlockSpec((rb, n_cols), lambda i, hold=hold: (i // hold, 0)))
        cast_shapes.append(jax.ShapeDtypeStruct((n_rows, n_cols), BF16))
    outs = pl.pallas_call(
        _qkv_kernel,
        grid=(n_steps,),
        in_specs=[pl.BlockSpec((TM, d), lambda i: (i, 0)), _layer_spec(g, layer),
                  _const_spec(w_in_layer.shape), _const_spec(gains.shape)] + cast_in_specs,
        out_specs=[slab_spec(n_q), slab_spec(n_q), slab_spec(n_q), slab_spec(n_q),
                   slab_spec(n_kv), slab_spec(n_kv)] + cast_out_specs,
        out_shape=[slab(n_q, F32), slab(n_q, F32), slab(n_q, F32), slab(n_q, BF16),
                   slab(n_kv, BF16), slab(n_kv, BF16)] + cast_shapes,
        compiler_params=pltpu.CompilerParams(dimension_semantics=("arbitrary",),
                                             vmem_limit_bytes=VMEM_LIMIT),
        name="qkv",
    )(x2, g, w_in_layer, gains, *[w for w, _ in to_cast])
    return outs[:6], outs[6:]


def _attn_block(q, kw, vw, tab, lo):
    zero = jnp.zeros_like(q)
    q2 = jnp.concatenate([jnp.where(lo, q, zero), jnp.where(lo, zero, q)], axis=0).astype(BF16)
    s = lax.dot_general(q2, kw.astype(BF16), (((1,), (1,)), ((), ())),
                        preferred_element_type=F32) + tab
    m = jnp.max(s, axis=-1, keepdims=True)
    p = jnp.exp2(s - m)
    l = jnp.sum(p, axis=-1, keepdims=True)
    o2 = _dot(p.astype(BF16), vw.astype(BF16))
    return o2, m, l


def _merge_heads(a2, lo):
    bq = a2.shape[0] // 2
    return jnp.where(lo, a2[:bq], a2[bq:])


def _block_geometry(i, n_blocks, seq_len, win):
    w0 = max(0, min(i * BQ - (win - BQ) // 2, seq_len - win))
    return pl.ds(i * BQ, BQ), pl.ds(w0, win), (0 if i == 0 else (2 if i == n_blocks - 1 else 1))


def _attn_a_kernel(q_ref, k_ref, v_ref, t1_ref, t4_ref, t16_ref, out_ref,
                   o1, lse1, outf, qs, ks, vs, o4, lse4, o16g, lse16g, o16, lse16, outc):
    lo = _low_lanes()
    seq = q_ref.shape[0]
    win = t1_ref.shape[-1]
    d4 = DILATED_PATTERNS[1][1]
    step = DILATED_PATTERNS[2][1] // d4
    sub4 = seq // d4

    def block(qr, kr, vr, q_rows, kv_rows, tab, o_out, lse_out, out_rows):
        o2, m, l = _attn_block(qr[q_rows, :], kr[kv_rows, :], vr[kv_rows, :], tab, lo)
        l = _merge_heads(l, lo)
        o_out[out_rows, :] = _merge_heads(o2, lo) / l
        lse_out[out_rows, :] = _merge_heads(m, lo) + jnp.log2(l)

    for r in range(d4):
        res = pl.ds(r, sub4, stride=d4)
        qs[r] = q_ref[res, :]
        ks[r] = k_ref[res, :]
        vs[r] = v_ref[res, :]

    nb = seq // BQ
    for i in range(nb):
        q_rows, kv_rows, tidx = _block_geometry(i, nb, seq, win)
        block(q_ref, k_ref, v_ref, q_rows, kv_rows, t1_ref[tidx], o1, lse1, q_rows)

    nb4 = sub4 // BQ
    for i in range(nb4):
        for r in range(d4):
            q_rows, kv_rows, tidx = _block_geometry(i, nb4, sub4, win)
            block(qs.at[r], ks.at[r], vs.at[r], q_rows, kv_rows, t4_ref[tidx], o4.at[r], lse4.at[r], q_rows)

    for b in range(step):
        for r in range(d4):
            rows = pl.ds(b, BQ, stride=step)
            block(qs.at[r], ks.at[r], vs.at[r], rows, rows, t16_ref[0], o16g.at[r], lse16g.at[r],
                  pl.ds(b * BQ, BQ))

    def merge_class(r, carry):
        for grouped, ordered in ((o16g, o16), (lse16g, lse16)):
            for b in range(step):
                ordered[r, pl.ds(b, BQ, stride=step), :] = grouped[r, pl.ds(b * BQ, BQ), :]
        for i in range(nb4):
            rows = pl.ds(i * BQ, BQ)
            nat = pl.ds(r + d4 * i * BQ, BQ, stride=d4)
            la, lb, lc = lse1[nat, :], lse4[r, rows, :], lse16[r, rows, :]
            mx = jnp.maximum(jnp.maximum(la, lb), lc)
            wa, wb, wc = jnp.exp2(la - mx), jnp.exp2(lb - mx), jnp.exp2(lc - mx)
            num = wa * o1[nat, :] + wb * o4[r, rows, :] + wc * o16[r, rows, :]
            outc[r, rows, :] = num / (wa + wb + wc)
        outf[pl.ds(r, sub4, stride=d4), :] = outc[r]
        return carry

    lax.fori_loop(0, d4, merge_class, 0)
    out_ref[...] = outf[...].astype(out_ref.dtype)


def _attn_a_call(qa, ka, va, t1, t4, t16, batch, seq):
    n_slab, t, _ = qa.shape
    assert seq // DILATED_PATTERNS[2][1] == BQ
    qkv_spec = pl.BlockSpec((None, seq, LANES), lambda hp, b: (hp, b, 0))
    tab_spec = lambda tab: pl.BlockSpec((None,) + tab.shape[1:], lambda hp, b: (hp, 0, 0, 0))
    d4 = DILATED_PATTERNS[1][1]
    full = pltpu.VMEM((seq, LANES), F32)
    by_class = pltpu.VMEM((d4, seq // d4, LANES), F32)
    return pl.pallas_call(
        _attn_a_kernel,
        grid=(n_slab, batch),
        in_specs=[qkv_spec, qkv_spec, qkv_spec, tab_spec(t1), tab_spec(t4), tab_spec(t16)],
        out_specs=pl.BlockSpec((seq, LANES), lambda hp, b: (b, hp)),
        out_shape=jax.ShapeDtypeStruct((t, n_slab * LANES), BF16),
        scratch_shapes=[full] * 3 + [by_class] * 10,
        compiler_params=pltpu.CompilerParams(dimension_semantics=("arbitrary", "arbitrary"),
                                             vmem_limit_bytes=VMEM_LIMIT),
        name="attn_a",
    )(qa, ka, va, t1, t4, t16)


def _attn_b_kernel(q_ref, k_ref, v_ref, tab_ref, sink_ref, o_ref):
    lo = _low_lanes()
    n_slab, seq, _ = q_ref.shape
    win = tab_ref.shape[-1]
    nb = seq // BQ
    zero = jnp.zeros((BQ, LANES), q_ref.dtype)

    for i in range(nb):
        q_rows, kv_rows, tidx = _block_geometry(i, nb, seq, win)
        q = [q_ref[sl, q_rows, :] for sl in range(n_slab)]
        q_all = jnp.concatenate([part for qs in q for part in (jnp.where(lo, qs, zero), jnp.where(lo, zero, qs))],
                                axis=0)
        tab = jnp.concatenate([tab_ref[sl, tidx] for sl in range(n_slab)], axis=0)
        s = lax.dot_general(q_all, k_ref[kv_rows, :], (((1,), (1,)), ((), ())), preferred_element_type=F32) + tab
        m = jnp.max(s, axis=-1, keepdims=True)
        p = jnp.exp2(s - m)
        vw = v_ref[kv_rows, :]
        o2 = _dot(p.astype(BF16), jnp.concatenate([vw, jnp.ones(vw.shape, BF16)], axis=1))
        for sl in range(n_slab):
            rows = slice(sl * 2 * BQ, (sl + 1) * 2 * BQ)
            o = _merge_heads(o2[rows, :LANES], lo)
            mm = _merge_heads(m[rows], lo)
            l = _merge_heads(o2[rows, LANES:], lo)
            sink = sink_ref[sl]
            m_new = jnp.maximum(mm, sink)
            scale = jnp.exp2(mm - m_new)
            den = l * scale + jnp.exp2(sink - m_new)
            o_ref[q_rows, sl * LANES:(sl + 1) * LANES] = (o * (scale / den)).astype(o_ref.dtype)


def _attn_b_call(qb, kb, vb, tab, sink, batch, seq):
    n_slab, t, _ = qb.shape
    n_kv = kb.shape[0]
    group = n_slab // n_kv
    kv_spec = pl.BlockSpec((None, seq, LANES), lambda kv, b: (kv, b, 0))
    return pl.pallas_call(
        _attn_b_kernel,
        grid=(n_kv, batch),
        in_specs=[pl.BlockSpec((group, seq, LANES), lambda kv, b: (kv, b, 0)), kv_spec, kv_spec,
                  pl.BlockSpec((group,) + tab.shape[1:], lambda kv, b: (kv, 0, 0, 0)),
                  pl.BlockSpec((group,) + sink.shape[1:], lambda kv, b: (kv, 0, 0))],
        out_specs=pl.BlockSpec((seq, group * LANES), lambda kv, b: (b, kv)),
        out_shape=jax.ShapeDtypeStruct((t, n_slab * LANES), BF16),
        compiler_params=pltpu.CompilerParams(dimension_semantics=("arbitrary", "arbitrary"),
                                             vmem_limit_bytes=VMEM_LIMIT),
        name="attn_b",
    )(qb, kb, vb, tab, sink)


def _row_call(kernel, name, row_inputs, layer_inputs, layer, out_dim, scratch):
    t = row_inputs[0].shape[0]

    def row_spec(a):
        if a.ndim == 2:
            return pl.BlockSpec((TM, a.shape[1]), lambda i: (i, 0))
        return pl.BlockSpec((None, TM, a.shape[2]), lambda i: (layer, i, 0))

    return pl.pallas_call(
        kernel,
        grid=(t // TM,),
        in_specs=[row_spec(a) for a in row_inputs]
        + [_layer_spec(a, layer) if a.ndim == 3 else _const_spec(a.shape) for a in layer_inputs],
        out_specs=pl.BlockSpec((TM, out_dim), lambda i: (i, 0)),
        out_shape=jax.ShapeDtypeStruct((t, out_dim), F32),
        scratch_shapes=scratch,
        compiler_params=pltpu.CompilerParams(dimension_semantics=("arbitrary",),
                                             vmem_limit_bytes=VMEM_LIMIT),
        name=name,
    )(*row_inputs, *layer_inputs)


def _post_kernel(x_ref, ya_ref, yb_ref, p_ref, gmix_ref, gffn_ref, gple_ref, win_ref, wa_ref, wb_ref, wo_ref,
                 wgate_ref, wup_ref, wdown_ref, wpg_ref, wpp_ref, o_ref, merged, hid, xs):
    d = x_ref.shape[1]
    chunks = [slice(c * MXU_COLS, (c + 1) * MXU_COLS) for c in range(d // MXU_COLS)]

    h = _rms(x_ref[...], gmix_ref[...]).astype(BF16)
    ya = ya_ref[...]
    yb = yb_ref[...]
    for c, cs in enumerate(chunks):
        ga = _dot(h, win_ref[:, GATE_COL + c * MXU_COLS:GATE_COL + (c + 1) * MXU_COLS])
        gb = _dot(h, win_ref[:, GATE_COL + d + c * MXU_COLS:GATE_COL + d + (c + 1) * MXU_COLS])
        pa = _dot(ya, wa_ref[:, cs])
        pb = _dot(yb, wb_ref[:, cs])
        merged[:, cs] = (jax.nn.sigmoid(ga) * pa + jax.nn.sigmoid(gb) * pb).astype(BF16)
    mg = merged[...]
    for cs in chunks:
        xs[:, cs] = x_ref[:, cs] + _dot(mg, wo_ref[:, cs])

    h = _rms(xs[...], gffn_ref[...]).astype(BF16)
    for c in range(wgate_ref.shape[1] // MXU_COLS):
        cs = slice(c * MXU_COLS, (c + 1) * MXU_COLS)
        a = _dot(h, wgate_ref[:, cs])
        u = _dot(h, wup_ref[:, cs])
        hid[:, cs] = (a * jax.nn.sigmoid(a) * u).astype(BF16)
    hv = hid[...]
    for cs in chunks:
        xs[:, cs] = xs[:, cs] + _dot(hv, wdown_ref[:, cs])

    h = _rms(xs[...], gple_ref[...]).astype(BF16)
    pv = p_ref[...].astype(BF16)
    for cs in chunks:
        o_ref[:, cs] = xs[:, cs] + jax.nn.sigmoid(_dot(h, wpg_ref[:, cs])) * _dot(pv, wpp_ref[:, cs])


def _t5_bucket(rel):
    half_b = NUM_BUCKETS // 2
    max_exact = half_b // 2
    sign = jnp.where(rel > 0, half_b, 0)
    n = jnp.abs(rel)
    nf = jnp.maximum(n, 1).astype(F32)
    large = max_exact + (jnp.log(nf / max_exact) / math.log(MAX_DISTANCE / max_exact)
                         * (half_b - max_exact)).astype(jnp.int32)
    large = jnp.minimum(large, half_b - 1)
    return sign + jnp.where(n < max_exact, n, large)


TABLE_PERIOD = 512


def _block_starts(seq_len, win):
    nb = seq_len // BQ
    blocks = (0, min(1, nb - 1), nb - 1) if nb > 1 else (0,)
    return [max(0, min(i * BQ - (win - BQ) // 2, seq_len - win)) - i * BQ for i in blocks]


def _rel_rows(table, starts, win, half, dilation):
    assert BQ + win - 1 <= TABLE_PERIOD
    c = jnp.arange(TABLE_PERIOD, dtype=jnp.int32)
    d = jnp.where(c < win, c, c - TABLE_PERIOD)
    rel = jnp.asarray(starts, jnp.int32)[:, None] + d[None, :]
    bias = table[_t5_bucket(rel * dilation)].astype(F32) * LOG2_E
    return jnp.transpose(jnp.where((jnp.abs(rel) <= half)[:, :, None], bias, NEG_INF), (0, 2, 1))


def _tables_kernel(rows_ref, *out_refs):
    v = 0
    for out_ref in out_refs:
        n_pair, n_var, _, win = out_ref.shape
        for t in range(n_var):
            for h in range(2 * n_pair):
                row = jnp.broadcast_to(rows_ref[v + t, h:h + 1, :], (BQ, TABLE_PERIOD))
                skew = pltpu.roll(row, 0, 1, stride=1, stride_axis=0)
                out_ref[h // 2, t, (h % 2) * BQ:(h % 2 + 1) * BQ, :] = skew[:, :win]
        v += n_var


def _band_tables(rows_and_wins, n_heads):
    rows = jnp.concatenate([r for r, _ in rows_and_wins], axis=0)
    shapes = [jax.ShapeDtypeStruct((n_heads // 2, r.shape[0], 2 * BQ, win), F32) for r, win in rows_and_wins]
    return pl.pallas_call(_tables_kernel, out_shape=shapes, name="tables",
                          compiler_params=pltpu.CompilerParams(vmem_limit_bytes=VMEM_LIMIT))(rows)


def _pair_gain(g, scale=1.0):
    return jnp.tile(g.astype(F32) * scale, 2)


def kernel(x, p, rel_table, norm_mix_g, w_in, qnorm_a_g, knorm_a_g, qnorm_b_g, knorm_b_g, sink_b,
           w_branch_a, w_branch_b, w_out, norm_ffn_g, w_ffn_gate, w_ffn_up, w_ffn_down,
           norm_ple_g, w_ple_gate, w_ple_proj):
    batch, seq, d = x.shape
    depth = p.shape[0]
    t = batch * seq
    q_scale = HEAD_DIM ** -0.5 * LOG2_E

    table_a = rel_table[:, :N_HEADS_A]
    table_b = rel_table[:, N_HEADS_A:]
    patterns = [(seq // dil, min(2 * BQ, seq // dil), w // (2 * dil), dil, table_a) for w, dil in DILATED_PATTERNS]
    patterns.append((seq, BQ + 2 * WINDOW_B, WINDOW_B, 1, table_b))
    rows = [(_rel_rows(tab, _block_starts(sub, win), win, half, dil), win) for sub, win, half, dil, tab in patterns]
    *tabs_a, tab_b = _band_tables(rows, N_HEADS_A)

    rows = lambda g: g.astype(F32).reshape(depth, 1, -1)
    g_mix, g_ffn, g_ple = rows(norm_mix_g), rows(norm_ffn_g), rows(norm_ple_g)
    p2 = p.reshape(depth, t, -1)
    post_weights = (w_branch_a, w_branch_b, w_out, w_ffn_gate, w_ffn_up, w_ffn_down, w_ple_gate, w_ple_proj)

    x2 = x.reshape(t, d)
    w_in_b = w_in[0].astype(BF16)
    for l in range(depth):
        gains = jnp.stack([_pair_gain(qnorm_a_g[l], q_scale), _pair_gain(knorm_a_g[l]),
                           _pair_gain(qnorm_b_g[l], q_scale), _pair_gain(knorm_b_g[l])])
        to_cast = [(w, l) for w in post_weights] + ([(w_in, l + 1)] if l + 1 < depth else [])
        (qa, ka, va, qb, kb, vb), cast = _qkv_call(x2, g_mix, w_in_b, gains, l, to_cast)
        ya = _attn_a_call(qa, ka, va, *tabs_a, batch, seq)
        sink = jnp.repeat(sink_b[l].astype(F32) * LOG2_E, HEAD_DIM).reshape(N_HEADS_B // 2, 1, LANES)
        yb = _attn_b_call(qb, kb, vb, tab_b, sink, batch, seq)
        x2 = _row_call(_post_kernel, "post", [x2, ya, yb, p2],
                       [g_mix, g_ffn, g_ple, w_in_b, *cast[:len(post_weights)]], l, d,
                       [pltpu.VMEM((TM, d), BF16), pltpu.VMEM((TM, w_ffn_gate.shape[2]), BF16),
                        pltpu.VMEM((TM, d), F32)])
        if l + 1 < depth:
            w_in_b = cast[-1]
    return x2.reshape(batch, seq, d)
```

```python
import math

import jax
import jax.numpy as jnp
from jax import lax
from jax.experimental import pallas as pl
from jax.experimental.pallas import tpu as pltpu

F32 = jnp.float32
BF16 = jnp.bfloat16

HEAD_DIM = 64
N_HEADS_A = 8
N_HEADS_B = 8
N_KV_B = 2
DILATED_PATTERNS = ((128, 1), (512, 4), (2048, 16))
WINDOW_B = 128
NUM_BUCKETS = 32
MAX_DISTANCE = 1024
RMS_EPS = 1e-6
NEG_INF = -1e30
LOG2_E = math.log2(math.e)

WIDTH_A = N_HEADS_A * HEAD_DIM
WIDTH_BQ = N_HEADS_B * HEAD_DIM
WIDTH_BKV = N_KV_B * HEAD_DIM
GATE_COL = 3 * WIDTH_A + WIDTH_BQ + 2 * WIDTH_BKV

LANES = 128
MXU_COLS = 256
BF16_SUBLANES = 16
BQ = 128
TM = 512
VMEM_LIMIT = 56 * 1024 * 1024


def _dot(a, b):
    return jnp.dot(a, b, preferred_element_type=F32)


def _rms(x, g):
    ms = jnp.mean(x * x, axis=-1, keepdims=True)
    return x * lax.rsqrt(ms + RMS_EPS) * g


def _low_lanes():
    return lax.broadcasted_iota(jnp.int32, (1, LANES), 1) < HEAD_DIM


def _head_rms(y, gain, lo):
    sq = y * y
    s0 = jnp.sum(jnp.where(lo, sq, 0.0), axis=-1, keepdims=True)
    s1 = jnp.sum(jnp.where(lo, 0.0, sq), axis=-1, keepdims=True)
    ms = jnp.where(lo, s0, s1) * (1.0 / HEAD_DIM)
    return y * lax.rsqrt(ms + RMS_EPS) * gain


def _const_spec(shape):
    nd = len(shape)
    return pl.BlockSpec(shape, lambda *_: (0,) * nd, pipeline_mode=pl.Buffered(1))


def _layer_spec(stacked, layer):
    nd = stacked.ndim - 1
    return pl.BlockSpec((None,) + stacked.shape[1:], lambda *_: (layer,) + (0,) * nd,
                        pipeline_mode=pl.Buffered(1))


def _qkv_kernel(x_ref, g_ref, w_ref, gains_ref, *refs):
    n_cast = (len(refs) - 6) // 2
    cast_src, cast_dst = refs[:n_cast], refs[n_cast + 6:]
    qa_ref, ka_ref, va_ref, qb_ref, kb_ref, vb_ref = refs[n_cast:n_cast + 6]
    _cast_side_job(cast_src, cast_dst)
    lo = _low_lanes()
    h = _rms(x_ref[...], g_ref[...]).astype(BF16)
    n_pair = WIDTH_A // MXU_COLS
    order = sorted(range(GATE_COL // MXU_COLS), key=lambda c: 2 * n_pair <= c < 3 * n_pair)
    for c in order:
        y = _dot(h, w_ref[:, c * MXU_COLS:(c + 1) * MXU_COLS])
        y0, y1 = y[:, :LANES], y[:, LANES:]
        if c < n_pair:
            qa_ref[2 * c] = _head_rms(y0, gains_ref[0:1, :], lo)
            qa_ref[2 * c + 1] = _head_rms(y1, gains_ref[0:1, :], lo)
        elif c < 2 * n_pair:
            ka_ref[2 * (c - n_pair)] = _head_rms(y0, gains_ref[1:2, :], lo)
            ka_ref[2 * (c - n_pair) + 1] = _head_rms(y1, gains_ref[1:2, :], lo)
        elif c < 3 * n_pair:
            va_ref[2 * (c - 2 * n_pair)] = y0
            va_ref[2 * (c - 2 * n_pair) + 1] = y1
        elif c < 4 * n_pair:
            qb_ref[2 * (c - 3 * n_pair)] = _head_rms(y0, gains_ref[2:3, :], lo).astype(BF16)
            qb_ref[2 * (c - 3 * n_pair) + 1] = _head_rms(y1, gains_ref[2:3, :], lo).astype(BF16)
        else:
            kn = _head_rms(y0, gains_ref[3:4, :], lo)
            ks = pltpu.roll(kn, HEAD_DIM, axis=1)
            vs = pltpu.roll(y1, HEAD_DIM, axis=1)
            kb_ref[0] = jnp.where(lo, kn, ks).astype(BF16)
            kb_ref[1] = jnp.where(lo, ks, kn).astype(BF16)
            vb_ref[0] = jnp.where(lo, y1, vs).astype(BF16)
            vb_ref[1] = jnp.where(lo, vs, y1).astype(BF16)


def _cast_blocks(n_rows, n_steps):
    for n_blocks in range(n_steps, 0, -1):
        if n_steps % n_blocks == 0 and n_rows % (n_blocks * BF16_SUBLANES) == 0:
            return n_rows // n_blocks, n_steps // n_blocks
    raise ValueError(f"cannot split {n_rows} rows into bf16 row blocks over {n_steps} steps")


def _cast_specs(to_cast, n_steps):
    in_specs, out_specs, shapes = [], [], []
    for w, w_layer in to_cast:
        _, n_rows, n_cols = w.shape
        rb, hold = _cast_blocks(n_rows, n_steps)
        in_specs.append(pl.BlockSpec((None, rb, n_cols), lambda i, w_layer=w_layer, hold=hold: (w_layer, i // hold, 0)))
        out_specs.append(pl.BlockSpec((rb, n_cols), lambda i, hold=hold: (i // hold, 0)))
        shapes.append(jax.ShapeDtypeStruct((n_rows, n_cols), BF16))
    return in_specs, out_specs, shapes


def _cast_side_job(cast_src, cast_dst):
    for src, dst in zip(cast_src, cast_dst):
        dst[...] = src[...].astype(BF16)


def _qkv_call(x2, g, w_in_layer, gains, layer, to_cast):
    t, d = x2.shape
    n_q = WIDTH_A // LANES
    n_kv = N_KV_B
    n_steps = t // TM
    assert WIDTH_BQ == WIDTH_A and 2 * WIDTH_BKV == MXU_COLS
    slab = lambda n, dt: jax.ShapeDtypeStruct((n, t, LANES), dt)
    slab_spec = lambda n: pl.BlockSpec((n, TM, LANES), lambda i: (0, i, 0))
    cast_in_specs, cast_out_specs, cast_shapes = _cast_specs(to_cast, n_steps)
    outs = pl.pallas_call(
        _qkv_kernel,
        grid=(n_steps,),
        in_specs=[pl.BlockSpec((TM, d), lambda i: (i, 0)), _layer_spec(g, layer),
                  _const_spec(w_in_layer.shape), _const_spec(gains.shape)] + cast_in_specs,
        out_specs=[slab_spec(n_q), slab_spec(n_q), slab_spec(n_q), slab_spec(n_q),
                   slab_spec(n_kv), slab_spec(n_kv)] + cast_out_specs,
        out_shape=[slab(n_q, F32), slab(n_q, F32), slab(n_q, F32), slab(n_q, BF16),
                   slab(n_kv, BF16), slab(n_kv, BF16)] + cast_shapes,
        compiler_params=pltpu.CompilerParams(dimension_semantics=("arbitrary",),
                                             vmem_limit_bytes=VMEM_LIMIT),
        name="qkv",
    )(x2, g, w_in_layer, gains, *[w for w, _ in to_cast])
    return outs[:6], outs[6:]


def _attn_block(q, kw, vw, tab, lo):
    zero = jnp.zeros_like(q)
    q2 = jnp.concatenate([jnp.where(lo, q, zero), jnp.where(lo, zero, q)], axis=0).astype(BF16)
    s = lax.dot_general(q2, kw.astype(BF16), (((1,), (1,)), ((), ())),
                        preferred_element_type=F32) + tab
    m = jnp.max(s, axis=-1, keepdims=True)
    p = jnp.exp2(s - m)
    l = jnp.sum(p, axis=-1, keepdims=True)
    o2 = _dot(p.astype(BF16), vw.astype(BF16))
    return o2, m, l


def _merge_heads(a2, lo):
    bq = a2.shape[0] // 2
    return jnp.where(lo, a2[:bq], a2[bq:])


def _block_geometry(i, n_blocks, seq_len, win):
    w0 = max(0, min(i * BQ - (win - BQ) // 2, seq_len - win))
    return pl.ds(i * BQ, BQ), pl.ds(w0, win), (0 if i == 0 else (2 if i == n_blocks - 1 else 1))


def _attn_a_kernel(q_ref, k_ref, v_ref, t1_ref, t4_ref, t16_ref, out_ref,
                   o1, lse1, outf, qs, ks, vs, o4, lse4, o16g, lse16g, o16, lse16, outc):
    lo = _low_lanes()
    seq = q_ref.shape[0]
    win = t1_ref.shape[-1]
    d4 = DILATED_PATTERNS[1][1]
    step = DILATED_PATTERNS[2][1] // d4
    sub4 = seq // d4

    def block(qr, kr, vr, q_rows, kv_rows, tab, o_out, lse_out, out_rows):
        o2, m, l = _attn_block(qr[q_rows, :], kr[kv_rows, :], vr[kv_rows, :], tab, lo)
        l = _merge_heads(l, lo)
        o_out[out_rows, :] = _merge_heads(o2, lo) / l
        lse_out[out_rows, :] = _merge_heads(m, lo) + jnp.log2(l)

    for r in range(d4):
        res = pl.ds(r, sub4, stride=d4)
        qs[r] = q_ref[res, :]
        ks[r] = k_ref[res, :]
        vs[r] = v_ref[res, :]

    nb = seq // BQ
    for i in range(nb):
        q_rows, kv_rows, tidx = _block_geometry(i, nb, seq, win)
        block(q_ref, k_ref, v_ref, q_rows, kv_rows, t1_ref[tidx], o1, lse1, q_rows)

    nb4 = sub4 // BQ
    for i in range(nb4):
        for r in range(d4):
            q_rows, kv_rows, tidx = _block_geometry(i, nb4, sub4, win)
            block(qs.at[r], ks.at[r], vs.at[r], q_rows, kv_rows, t4_ref[tidx], o4.at[r], lse4.at[r], q_rows)

    for b in range(step):
        for r in range(d4):
            rows = pl.ds(b, BQ, stride=step)
            block(qs.at[r], ks.at[r], vs.at[r], rows, rows, t16_ref[0], o16g.at[r], lse16g.at[r],
                  pl.ds(b * BQ, BQ))

    def merge_class(r, carry):
        for grouped, ordered in ((o16g, o16), (lse16g, lse16)):
            for b in range(step):
                ordered[r, pl.ds(b, BQ, stride=step), :] = grouped[r, pl.ds(b * BQ, BQ), :]
        for i in range(nb4):
            rows = pl.ds(i * BQ, BQ)
            nat = pl.ds(r + d4 * i * BQ, BQ, stride=d4)
            la, lb, lc = lse1[nat, :], lse4[r, rows, :], lse16[r, rows, :]
            mx = jnp.maximum(jnp.maximum(la, lb), lc)
            wa, wb, wc = jnp.exp2(la - mx), jnp.exp2(lb - mx), jnp.exp2(lc - mx)
            num = wa * o1[nat, :] + wb * o4[r, rows, :] + wc * o16[r, rows, :]
            outc[r, rows, :] = num / (wa + wb + wc)
        outf[pl.ds(r, sub4, stride=d4), :] = outc[r]
        return carry

    lax.fori_loop(0, d4, merge_class, 0)
    out_ref[...] = outf[...].astype(out_ref.dtype)


def _attn_a_call(qa, ka, va, t1, t4, t16, batch, seq):
    n_slab, t, _ = qa.shape
    assert seq // DILATED_PATTERNS[2][1] == BQ
    qkv_spec = pl.BlockSpec((None, seq, LANES), lambda hp, b: (hp, b, 0))
    tab_spec = lambda tab: pl.BlockSpec((None,) + tab.shape[1:], lambda hp, b: (hp, 0, 0, 0))
    d4 = DILATED_PATTERNS[1][1]
    full = pltpu.VMEM((seq, LANES), F32)
    by_class = pltpu.VMEM((d4, seq // d4, LANES), F32)
    return pl.pallas_call(
        _attn_a_kernel,
        grid=(n_slab, batch),
        in_specs=[qkv_spec, qkv_spec, qkv_spec, tab_spec(t1), tab_spec(t4), tab_spec(t16)],
        out_specs=pl.BlockSpec((seq, LANES), lambda hp, b: (b, hp)),
        out_shape=jax.ShapeDtypeStruct((t, n_slab * LANES), BF16),
        scratch_shapes=[full] * 3 + [by_class] * 10,
        compiler_params=pltpu.CompilerParams(dimension_semantics=("arbitrary", "arbitrary"),
                                             vmem_limit_bytes=VMEM_LIMIT),
        name="attn_a",
    )(qa, ka, va, t1, t4, t16)


def _attn_b_kernel(q_ref, k_ref, v_ref, tab_ref, sink_ref, o_ref):
    lo = _low_lanes()
    n_slab, seq, _ = q_ref.shape
    win = tab_ref.shape[-1]
    nb = seq // BQ
    zero = jnp.zeros((BQ, LANES), q_ref.dtype)

    for i in range(nb):
        q_rows, kv_rows, tidx = _block_geometry(i, nb, seq, win)
        q = [q_ref[sl, q_rows, :] for sl in range(n_slab)]
        q_all = jnp.concatenate([part for qs in q for part in (jnp.where(lo, qs, zero), jnp.where(lo, zero, qs))],
                                axis=0)
        tab = jnp.concatenate([tab_ref[sl, tidx] for sl in range(n_slab)], axis=0)
        s = lax.dot_general(q_all, k_ref[kv_rows, :], (((1,), (1,)), ((), ())), preferred_element_type=F32) + tab
        m = jnp.max(s, axis=-1, keepdims=True)
        p = jnp.exp2(s - m)
        vw = v_ref[kv_rows, :]
        o2 = _dot(p.astype(BF16), jnp.concatenate([vw, jnp.ones(vw.shape, BF16)], axis=1))
        for sl in range(n_slab):
            rows = slice(sl * 2 * BQ, (sl + 1) * 2 * BQ)
            o = _merge_heads(o2[rows, :LANES], lo)
            mm = _merge_heads(m[rows], lo)
            l = _merge_heads(o2[rows, LANES:], lo)
            sink = sink_ref[sl]
            m_new = jnp.maximum(mm, sink)
            scale = jnp.exp2(mm - m_new)
            den = l * scale + jnp.exp2(sink - m_new)
            o_ref[q_rows, sl * LANES:(sl + 1) * LANES] = (o * (scale / den)).astype(o_ref.dtype)


def _attn_b_call(qb, kb, vb, tab, sink, batch, seq):
    n_slab, t, _ = qb.shape
    n_kv = kb.shape[0]
    group = n_slab // n_kv
    kv_spec = pl.BlockSpec((None, seq, LANES), lambda kv, b: (kv, b, 0))
    return pl.pallas_call(
        _attn_b_kernel,
        grid=(n_kv, batch),
        in_specs=[pl.BlockSpec((group, seq, LANES), lambda kv, b: (kv, b, 0)), kv_spec, kv_spec,
                  pl.BlockSpec((group,) + tab.shape[1:], lambda kv, b: (kv, 0, 0, 0)),
                  pl.BlockSpec((group,) + sink.shape[1:], lambda kv, b: (kv, 0, 0))],
        out_specs=pl.BlockSpec((seq, group * LANES), lambda kv, b: (b, kv)),
        out_shape=jax.ShapeDtypeStruct((t, n_slab * LANES), BF16),
        compiler_params=pltpu.CompilerParams(dimension_semantics=("arbitrary", "arbitrary"),
                                             vmem_limit_bytes=VMEM_LIMIT),
        name="attn_b",
    )(qb, kb, vb, tab, sink)


def _post_call(x2, ya, yb, p2, resident, layer, d_ff, to_cast):
    t, d = x2.shape
    n_steps = t // TM
    row_spec = lambda a: pl.BlockSpec((TM, a.shape[1]), lambda i: (i, 0))
    cast_in_specs, cast_out_specs, cast_shapes = _cast_specs(to_cast, n_steps)
    outs = pl.pallas_call(
        _post_kernel,
        grid=(n_steps,),
        in_specs=[row_spec(x2), row_spec(ya), row_spec(yb),
                  pl.BlockSpec((None, TM, p2.shape[2]), lambda i: (layer, i, 0))]
        + [_layer_spec(a, layer) if a.ndim == 3 else _const_spec(a.shape) for a in resident] + cast_in_specs,
        out_specs=[pl.BlockSpec((TM, d), lambda i: (i, 0))] + cast_out_specs,
        out_shape=[jax.ShapeDtypeStruct((t, d), F32)] + cast_shapes,
        scratch_shapes=[pltpu.VMEM((TM, d), BF16), pltpu.VMEM((TM, d_ff), BF16), pltpu.VMEM((TM, d), F32)],
        compiler_params=pltpu.CompilerParams(dimension_semantics=("arbitrary",),
                                             vmem_limit_bytes=VMEM_LIMIT),
        name="post",
    )(x2, ya, yb, p2, *resident, *[w for w, _ in to_cast])
    return outs[0], outs[1:]


def _post_kernel(x_ref, ya_ref, yb_ref, p_ref, gmix_ref, gffn_ref, gple_ref, win_ref, wa_ref, wb_ref, wo_ref,
                 wgate_ref, wup_ref, wdown_ref, wpg_ref, wpp_ref, *refs):
    n_cast = (len(refs) - 4) // 2
    o_ref = refs[n_cast]
    merged, hid, xs = refs[2 * n_cast + 1:]
    _cast_side_job(refs[:n_cast], refs[n_cast + 1:2 * n_cast + 1])
    d = x_ref.shape[1]
    chunks = [slice(c * MXU_COLS, (c + 1) * MXU_COLS) for c in range(d // MXU_COLS)]

    h = _rms(x_ref[...], gmix_ref[...]).astype(BF16)
    ya = ya_ref[...]
    yb = yb_ref[...]
    for c, cs in enumerate(chunks):
        ga = _dot(h, win_ref[:, GATE_COL + c * MXU_COLS:GATE_COL + (c + 1) * MXU_COLS])
        gb = _dot(h, win_ref[:, GATE_COL + d + c * MXU_COLS:GATE_COL + d + (c + 1) * MXU_COLS])
        pa = _dot(ya, wa_ref[:, cs])
        pb = _dot(yb, wb_ref[:, cs])
        merged[:, cs] = (jax.nn.sigmoid(ga) * pa + jax.nn.sigmoid(gb) * pb).astype(BF16)
    mg = merged[...]
    for cs in chunks:
        xs[:, cs] = x_ref[:, cs] + _dot(mg, wo_ref[:, cs])

    h = _rms(xs[...], gffn_ref[...]).astype(BF16)
    for c in range(wgate_ref.shape[1] // MXU_COLS):
        cs = slice(c * MXU_COLS, (c + 1) * MXU_COLS)
        a = _dot(h, wgate_ref[:, cs])
        u = _dot(h, wup_ref[:, cs])
        hid[:, cs] = (a * jax.nn.sigmoid(a) * u).astype(BF16)
    hv = hid[...]
    for cs in chunks:
        xs[:, cs] = xs[:, cs] + _dot(hv, wdown_ref[:, cs])

    h = _rms(xs[...], gple_ref[...]).astype(BF16)
    pv = p_ref[...].astype(BF16)
    for cs in chunks:
        o_ref[:, cs] = xs[:, cs] + jax.nn.sigmoid(_dot(h, wpg_ref[:, cs])) * _dot(pv, wpp_ref[:, cs])


def _t5_bucket(rel):
    half_b = NUM_BUCKETS // 2
    max_exact = half_b // 2
    sign = jnp.where(rel > 0, half_b, 0)
    n = jnp.abs(rel)
    nf = jnp.maximum(n, 1).astype(F32)
    large = max_exact + (jnp.log(nf / max_exact) / math.log(MAX_DISTANCE / max_exact)
                         * (half_b - max_exact)).astype(jnp.int32)
    large = jnp.minimum(large, half_b - 1)
    return sign + jnp.where(n < max_exact, n, large)


TABLE_PERIOD = 512


def _block_starts(seq_len, win):
    nb = seq_len // BQ
    blocks = (0, min(1, nb - 1), nb - 1) if nb > 1 else (0,)
    return [max(0, min(i * BQ - (win - BQ) // 2, seq_len - win)) - i * BQ for i in blocks]


def _rel_rows(table, starts, win, half, dilation):
    assert BQ + win - 1 <= TABLE_PERIOD
    c = jnp.arange(TABLE_PERIOD, dtype=jnp.int32)
    d = jnp.where(c < win, c, c - TABLE_PERIOD)
    rel = jnp.asarray(starts, jnp.int32)[:, None] + d[None, :]
    bias = table[_t5_bucket(rel * dilation)].astype(F32) * LOG2_E
    return jnp.transpose(jnp.where((jnp.abs(rel) <= half)[:, :, None], bias, NEG_INF), (0, 2, 1))


def _tables_kernel(rows_ref, *out_refs):
    v = 0
    for out_ref in out_refs:
        n_pair, n_var, _, win = out_ref.shape
        for t in range(n_var):
            for h in range(2 * n_pair):
                row = jnp.broadcast_to(rows_ref[v + t, h:h + 1, :], (BQ, TABLE_PERIOD))
                skew = pltpu.roll(row, 0, 1, stride=1, stride_axis=0)
                out_ref[h // 2, t, (h % 2) * BQ:(h % 2 + 1) * BQ, :] = skew[:, :win]
        v += n_var


def _band_tables(rows_and_wins, n_heads):
    rows = jnp.concatenate([r for r, _ in rows_and_wins], axis=0)
    shapes = [jax.ShapeDtypeStruct((n_heads // 2, r.shape[0], 2 * BQ, win), F32) for r, win in rows_and_wins]
    return pl.pallas_call(_tables_kernel, out_shape=shapes, name="tables",
                          compiler_params=pltpu.CompilerParams(vmem_limit_bytes=VMEM_LIMIT))(rows)


def _pair_gain(g, scale=1.0):
    return jnp.tile(g.astype(F32) * scale, 2)


def kernel(x, p, rel_table, norm_mix_g, w_in, qnorm_a_g, knorm_a_g, qnorm_b_g, knorm_b_g, sink_b,
           w_branch_a, w_branch_b, w_out, norm_ffn_g, w_ffn_gate, w_ffn_up, w_ffn_down,
           norm_ple_g, w_ple_gate, w_ple_proj):
    batch, seq, d = x.shape
    depth = p.shape[0]
    t = batch * seq
    q_scale = HEAD_DIM ** -0.5 * LOG2_E

    table_a = rel_table[:, :N_HEADS_A]
    table_b = rel_table[:, N_HEADS_A:]
    patterns = [(seq // dil, min(2 * BQ, seq // dil), w // (2 * dil), dil, table_a) for w, dil in DILATED_PATTERNS]
    patterns.append((seq, BQ + 2 * WINDOW_B, WINDOW_B, 1, table_b))
    rows = [(_rel_rows(tab, _block_starts(sub, win), win, half, dil), win) for sub, win, half, dil, tab in patterns]
    *tabs_a, tab_b = _band_tables(rows, N_HEADS_A)

    rows = lambda g: g.astype(F32).reshape(depth, 1, -1)
    g_mix, g_ffn, g_ple = rows(norm_mix_g), rows(norm_ffn_g), rows(norm_ple_g)
    p2 = p.reshape(depth, t, -1)
    post_weights = (w_branch_a, w_branch_b, w_out, w_ffn_gate, w_ffn_up, w_ffn_down, w_ple_gate, w_ple_proj)

    x2 = x.reshape(t, d)
    w_in_b = w_in[0].astype(BF16)
    post_b = []
    for l in range(depth):
        gains = jnp.stack([_pair_gain(qnorm_a_g[l], q_scale), _pair_gain(knorm_a_g[l]),
                           _pair_gain(qnorm_b_g[l], q_scale), _pair_gain(knorm_b_g[l])])
        (qa, ka, va, qb, kb, vb), cast = _qkv_call(x2, g_mix, w_in_b, gains, l,
                                                    [(w, 0) for w in post_weights] if l == 0 else [])
        if l == 0:
            post_b = cast
        ya = _attn_a_call(qa, ka, va, *tabs_a, batch, seq)
        sink = jnp.repeat(sink_b[l].astype(F32) * LOG2_E, HEAD_DIM).reshape(N_HEADS_B // 2, 1, LANES)
        yb = _attn_b_call(qb, kb, vb, tab_b, sink, batch, seq)
        to_cast = [(w, l + 1) for w in (w_in,) + post_weights] if l + 1 < depth else []
        x2, cast = _post_call(x2, ya, yb, p2, [g_mix, g_ffn, g_ple, w_in_b, *post_b], l, w_ffn_gate.shape[2],
                              to_cast)
        if to_cast:
            w_in_b, post_b = cast[0], cast[1:]
    return x2.reshape(batch, seq, d)
```

```python
import functools
import math

import jax
import jax.numpy as jnp
from jax import lax
from jax.experimental import pallas as pl
from jax.experimental.pallas import tpu as pltpu

F32 = jnp.float32
BF16 = jnp.bfloat16

HEAD_DIM = 64
N_HEADS_A = 8
N_HEADS_B = 8
N_KV_B = 2
DILATED_PATTERNS = ((128, 1), (512, 4), (2048, 16))
WINDOW_B = 128
NUM_BUCKETS = 32
MAX_DISTANCE = 1024
RMS_EPS = 1e-6
NEG_INF = -1e30
LOG2_E = math.log2(math.e)

WIDTH_A = N_HEADS_A * HEAD_DIM
WIDTH_BQ = N_HEADS_B * HEAD_DIM
WIDTH_BKV = N_KV_B * HEAD_DIM
GATE_COL = 3 * WIDTH_A + WIDTH_BQ + 2 * WIDTH_BKV

LANES = 128
MXU_COLS = 256
BF16_SUBLANES = 16
BQ = 128
TM = 512
VMEM_LIMIT = 56 * 1024 * 1024


def _dot(a, b):
    return jnp.dot(a, b, preferred_element_type=F32)


def _rms(x, g):
    ms = jnp.mean(x * x, axis=-1, keepdims=True)
    return x * lax.rsqrt(ms + RMS_EPS) * g


def _low_lanes():
    return lax.broadcasted_iota(jnp.int32, (1, LANES), 1) < HEAD_DIM


def _head_rms(y, gain, lo):
    sq = y * y
    s0 = jnp.sum(jnp.where(lo, sq, 0.0), axis=-1, keepdims=True)
    s1 = jnp.sum(jnp.where(lo, 0.0, sq), axis=-1, keepdims=True)
    ms = jnp.where(lo, s0, s1) * (1.0 / HEAD_DIM)
    return y * lax.rsqrt(ms + RMS_EPS) * gain


def _const_spec(shape):
    nd = len(shape)
    return pl.BlockSpec(shape, lambda *_: (0,) * nd, pipeline_mode=pl.Buffered(1))


def _layer_spec(stacked, layer):
    nd = stacked.ndim - 1
    return pl.BlockSpec((None,) + stacked.shape[1:], lambda *_: (layer,) + (0,) * nd,
                        pipeline_mode=pl.Buffered(1))


def _qkv_kernel(x_ref, g_ref, w_ref, gains_ref, qa_ref, ka_ref, va_ref, qb_ref, kb_ref, vb_ref):
    lo = _low_lanes()
    h = _rms(x_ref[...], g_ref[...]).astype(BF16)
    n_pair = WIDTH_A // MXU_COLS
    order = sorted(range(GATE_COL // MXU_COLS), key=lambda c: 2 * n_pair <= c < 3 * n_pair)
    for c in order:
        y = _dot(h, w_ref[:, c * MXU_COLS:(c + 1) * MXU_COLS])
        y0, y1 = y[:, :LANES], y[:, LANES:]
        if c < n_pair:
            qa_ref[2 * c] = _head_rms(y0, gains_ref[0:1, :], lo)
            qa_ref[2 * c + 1] = _head_rms(y1, gains_ref[0:1, :], lo)
        elif c < 2 * n_pair:
            ka_ref[2 * (c - n_pair)] = _head_rms(y0, gains_ref[1:2, :], lo)
            ka_ref[2 * (c - n_pair) + 1] = _head_rms(y1, gains_ref[1:2, :], lo)
        elif c < 3 * n_pair:
            va_ref[2 * (c - 2 * n_pair)] = y0
            va_ref[2 * (c - 2 * n_pair) + 1] = y1
        elif c < 4 * n_pair:
            qb_ref[2 * (c - 3 * n_pair)] = _head_rms(y0, gains_ref[2:3, :], lo).astype(BF16)
            qb_ref[2 * (c - 3 * n_pair) + 1] = _head_rms(y1, gains_ref[2:3, :], lo).astype(BF16)
        else:
            kn = _head_rms(y0, gains_ref[3:4, :], lo)
            ks = pltpu.roll(kn, HEAD_DIM, axis=1)
            vs = pltpu.roll(y1, HEAD_DIM, axis=1)
            kb_ref[0] = jnp.where(lo, kn, ks).astype(BF16)
            kb_ref[1] = jnp.where(lo, ks, kn).astype(BF16)
            vb_ref[0] = jnp.where(lo, y1, vs).astype(BF16)
            vb_ref[1] = jnp.where(lo, vs, y1).astype(BF16)


def _cast_blocks(n_rows, n_steps):
    for n_blocks in range(n_steps, 0, -1):
        if n_steps % n_blocks == 0 and n_rows % (n_blocks * BF16_SUBLANES) == 0:
            return n_rows // n_blocks, n_steps // n_blocks
    raise ValueError(f"cannot split {n_rows} rows into bf16 row blocks over {n_steps} steps")


def _cast_specs(to_cast, n_steps, step_of=lambda i: i):
    in_specs, out_specs, shapes = [], [], []
    for w, w_layer in to_cast:
        _, n_rows, n_cols = w.shape
        rb, hold = _cast_blocks(n_rows, n_steps)
        in_specs.append(pl.BlockSpec((None, rb, n_cols),
                                     lambda *idx, w_layer=w_layer, hold=hold: (w_layer, step_of(*idx) // hold, 0)))
        out_specs.append(pl.BlockSpec((rb, n_cols), lambda *idx, hold=hold: (step_of(*idx) // hold, 0)))
        shapes.append(jax.ShapeDtypeStruct((n_rows, n_cols), BF16))
    return in_specs, out_specs, shapes


def _cast_side_job(refs, n_cast):
    for src, dst in zip(refs[:n_cast], refs[n_cast + 1:2 * n_cast + 1]):
        dst[...] = src[...].astype(BF16)
    return refs[n_cast], refs[2 * n_cast + 1:]


def _qkv_call(x2, g, w_in_layer, gains, layer):
    t, d = x2.shape
    n_q = WIDTH_A // LANES
    n_kv = N_KV_B
    assert WIDTH_BQ == WIDTH_A and 2 * WIDTH_BKV == MXU_COLS
    slab = lambda n, dt: jax.ShapeDtypeStruct((n, t, LANES), dt)
    slab_spec = lambda n: pl.BlockSpec((n, TM, LANES), lambda i: (0, i, 0))
    return pl.pallas_call(
        _qkv_kernel,
        grid=(t // TM,),
        in_specs=[pl.BlockSpec((TM, d), lambda i: (i, 0)), _layer_spec(g, layer),
                  _const_spec(w_in_layer.shape), _const_spec(gains.shape)],
        out_specs=[slab_spec(n_q), slab_spec(n_q), slab_spec(n_q), slab_spec(n_q),
                   slab_spec(n_kv), slab_spec(n_kv)],
        out_shape=[slab(n_q, F32), slab(n_q, F32), slab(n_q, F32), slab(n_q, BF16),
                   slab(n_kv, BF16), slab(n_kv, BF16)],
        compiler_params=pltpu.CompilerParams(dimension_semantics=("arbitrary",),
                                             vmem_limit_bytes=VMEM_LIMIT),
        name="qkv",
    )(x2, g, w_in_layer, gains)


def _attn_block(q, kw, vw, tab, lo):
    zero = jnp.zeros_like(q)
    q2 = jnp.concatenate([jnp.where(lo, q, zero), jnp.where(lo, zero, q)], axis=0).astype(BF16)
    s = lax.dot_general(q2, kw.astype(BF16), (((1,), (1,)), ((), ())),
                        preferred_element_type=F32) + tab
    m = jnp.max(s, axis=-1, keepdims=True)
    p = jnp.exp2(s - m)
    l = jnp.sum(p, axis=-1, keepdims=True)
    o2 = _dot(p.astype(BF16), vw.astype(BF16))
    return o2, m, l


def _merge_heads(a2, lo):
    bq = a2.shape[0] // 2
    return jnp.where(lo, a2[:bq], a2[bq:])


def _block_geometry(i, n_blocks, seq_len, win):
    w0 = max(0, min(i * BQ - (win - BQ) // 2, seq_len - win))
    return pl.ds(i * BQ, BQ), pl.ds(w0, win), (0 if i == 0 else (2 if i == n_blocks - 1 else 1))


def _attn_a_kernel(q_ref, k_ref, v_ref, t1_ref, t4_ref, t16_ref, *refs, n_cast):
    out_ref, (o1, lse1, outf, qs, ks, vs, o4, lse4, o16g, lse16g, o16, lse16, outc) = _cast_side_job(refs, n_cast)
    lo = _low_lanes()
    seq = q_ref.shape[0]
    win = t1_ref.shape[-1]
    d4 = DILATED_PATTERNS[1][1]
    step = DILATED_PATTERNS[2][1] // d4
    sub4 = seq // d4

    def block(qr, kr, vr, q_rows, kv_rows, tab, o_out, lse_out, out_rows):
        o2, m, l = _attn_block(qr[q_rows, :], kr[kv_rows, :], vr[kv_rows, :], tab, lo)
        l = _merge_heads(l, lo)
        o_out[out_rows, :] = _merge_heads(o2, lo) / l
        lse_out[out_rows, :] = _merge_heads(m, lo) + jnp.log2(l)

    for r in range(d4):
        res = pl.ds(r, sub4, stride=d4)
        qs[r] = q_ref[res, :]
        ks[r] = k_ref[res, :]
        vs[r] = v_ref[res, :]

    nb = seq // BQ
    for i in range(nb):
        q_rows, kv_rows, tidx = _block_geometry(i, nb, seq, win)
        block(q_ref, k_ref, v_ref, q_rows, kv_rows, t1_ref[tidx], o1, lse1, q_rows)

    nb4 = sub4 // BQ
    for i in range(nb4):
        for r in range(d4):
            q_rows, kv_rows, tidx = _block_geometry(i, nb4, sub4, win)
            block(qs.at[r], ks.at[r], vs.at[r], q_rows, kv_rows, t4_ref[tidx], o4.at[r], lse4.at[r], q_rows)

    for b in range(step):
        for r in range(d4):
            rows = pl.ds(b, BQ, stride=step)
            block(qs.at[r], ks.at[r], vs.at[r], rows, rows, t16_ref[0], o16g.at[r], lse16g.at[r],
                  pl.ds(b * BQ, BQ))

    def merge_class(r, carry):
        for grouped, ordered in ((o16g, o16), (lse16g, lse16)):
            for b in range(step):
                ordered[r, pl.ds(b, BQ, stride=step), :] = grouped[r, pl.ds(b * BQ, BQ), :]
        for i in range(nb4):
            rows = pl.ds(i * BQ, BQ)
            nat = pl.ds(r + d4 * i * BQ, BQ, stride=d4)
            la, lb, lc = lse1[nat, :], lse4[r, rows, :], lse16[r, rows, :]
            mx = jnp.maximum(jnp.maximum(la, lb), lc)
            wa, wb, wc = jnp.exp2(la - mx), jnp.exp2(lb - mx), jnp.exp2(lc - mx)
            num = wa * o1[nat, :] + wb * o4[r, rows, :] + wc * o16[r, rows, :]
            outc[r, rows, :] = num / (wa + wb + wc)
        outf[pl.ds(r, sub4, stride=d4), :] = outc[r]
        return carry

    lax.fori_loop(0, d4, merge_class, 0)
    out_ref[...] = outf[...].astype(out_ref.dtype)


def _attn_a_call(qa, ka, va, t1, t4, t16, batch, seq, to_cast):
    n_slab, t, _ = qa.shape
    assert seq // DILATED_PATTERNS[2][1] == BQ
    qkv_spec = pl.BlockSpec((None, seq, LANES), lambda hp, b: (hp, b, 0))
    tab_spec = lambda tab: pl.BlockSpec((None,) + tab.shape[1:], lambda hp, b: (hp, 0, 0, 0))
    d4 = DILATED_PATTERNS[1][1]
    full = pltpu.VMEM((seq, LANES), F32)
    by_class = pltpu.VMEM((d4, seq // d4, LANES), F32)
    cast_in_specs, cast_out_specs, cast_shapes = _cast_specs(to_cast, n_slab * batch, lambda hp, b: hp * batch + b)
    outs = pl.pallas_call(
        functools.partial(_attn_a_kernel, n_cast=len(to_cast)),
        grid=(n_slab, batch),
        in_specs=[qkv_spec, qkv_spec, qkv_spec, tab_spec(t1), tab_spec(t4), tab_spec(t16)] + cast_in_specs,
        out_specs=[pl.BlockSpec((seq, LANES), lambda hp, b: (b, hp))] + cast_out_specs,
        out_shape=[jax.ShapeDtypeStruct((t, n_slab * LANES), BF16)] + cast_shapes,
        scratch_shapes=[full] * 3 + [by_class] * 10,
        compiler_params=pltpu.CompilerParams(dimension_semantics=("arbitrary", "arbitrary"),
                                             vmem_limit_bytes=VMEM_LIMIT),
        name="attn_a",
    )(qa, ka, va, t1, t4, t16, *[w for w, _ in to_cast])
    return outs[0], outs[1:]


def _attn_b_kernel(q_ref, k_ref, v_ref, tab_ref, sink_ref, o_ref):
    lo = _low_lanes()
    n_slab, seq, _ = q_ref.shape
    win = tab_ref.shape[-1]
    nb = seq // BQ
    zero = jnp.zeros((BQ, LANES), q_ref.dtype)

    for i in range(nb):
        q_rows, kv_rows, tidx = _block_geometry(i, nb, seq, win)
        q = [q_ref[sl, q_rows, :] for sl in range(n_slab)]
        q_all = jnp.concatenate([part for qs in q for part in (jnp.where(lo, qs, zero), jnp.where(lo, zero, qs))],
                                axis=0)
        tab = jnp.concatenate([tab_ref[sl, tidx] for sl in range(n_slab)], axis=0)
        s = lax.dot_general(q_all, k_ref[kv_rows, :], (((1,), (1,)), ((), ())), preferred_element_type=F32) + tab
        m = jnp.max(s, axis=-1, keepdims=True)
        p = jnp.exp2(s - m)
        vw = v_ref[kv_rows, :]
        o2 = _dot(p.astype(BF16), jnp.concatenate([vw, jnp.ones(vw.shape, BF16)], axis=1))
        for sl in range(n_slab):
            rows = slice(sl * 2 * BQ, (sl + 1) * 2 * BQ)
            o = _merge_heads(o2[rows, :LANES], lo)
            mm = _merge_heads(m[rows], lo)
            l = _merge_heads(o2[rows, LANES:], lo)
            sink = sink_ref[sl]
            m_new = jnp.maximum(mm, sink)
            scale = jnp.exp2(mm - m_new)
            den = l * scale + jnp.exp2(sink - m_new)
            o_ref[q_rows, sl * LANES:(sl + 1) * LANES] = (o * (scale / den)).astype(o_ref.dtype)


def _attn_b_call(qb, kb, vb, tab, sink, batch, seq):
    n_slab, t, _ = qb.shape
    n_kv = kb.shape[0]
    group = n_slab // n_kv
    kv_spec = pl.BlockSpec((None, seq, LANES), lambda kv, b: (kv, b, 0))
    return pl.pallas_call(
        _attn_b_kernel,
        grid=(n_kv, batch),
        in_specs=[pl.BlockSpec((group, seq, LANES), lambda kv, b: (kv, b, 0)), kv_spec, kv_spec,
                  pl.BlockSpec((group,) + tab.shape[1:], lambda kv, b: (kv, 0, 0, 0)),
                  pl.BlockSpec((group,) + sink.shape[1:], lambda kv, b: (kv, 0, 0))],
        out_specs=pl.BlockSpec((seq, group * LANES), lambda kv, b: (b, kv)),
        out_shape=jax.ShapeDtypeStruct((t, n_slab * LANES), BF16),
        compiler_params=pltpu.CompilerParams(dimension_semantics=("arbitrary", "arbitrary"),
                                             vmem_limit_bytes=VMEM_LIMIT),
        name="attn_b",
    )(qb, kb, vb, tab, sink)


def _post_call(x2, ya, yb, p2, resident, layer, d_ff, to_cast):
    t, d = x2.shape
    n_steps = t // TM
    row_spec = lambda a: pl.BlockSpec((TM, a.shape[1]), lambda i: (i, 0))
    cast_in_specs, cast_out_specs, cast_shapes = _cast_specs(to_cast, n_steps)
    outs = pl.pallas_call(
        functools.partial(_post_kernel, n_cast=len(to_cast)),
        grid=(n_steps,),
        in_specs=[row_spec(x2), row_spec(ya), row_spec(yb),
                  pl.BlockSpec((None, TM, p2.shape[2]), lambda i: (layer, i, 0))]
        + [_layer_spec(a, layer) if a.ndim == 3 else _const_spec(a.shape) for a in resident] + cast_in_specs,
        out_specs=[pl.BlockSpec((TM, d), lambda i: (i, 0))] + cast_out_specs,
        out_shape=[jax.ShapeDtypeStruct((t, d), F32)] + cast_shapes,
        scratch_shapes=[pltpu.VMEM((TM, d), BF16), pltpu.VMEM((TM, d_ff), BF16), pltpu.VMEM((TM, d), F32)],
        compiler_params=pltpu.CompilerParams(dimension_semantics=("arbitrary",),
                                             vmem_limit_bytes=VMEM_LIMIT),
        name="post",
    )(x2, ya, yb, p2, *resident, *[w for w, _ in to_cast])
    return outs[0], outs[1:]


def _post_kernel(x_ref, ya_ref, yb_ref, p_ref, gmix_ref, gffn_ref, gple_ref, win_ref, wa_ref, wb_ref, wo_ref,
                 wgate_ref, wup_ref, wdown_ref, wpg_ref, wpp_ref, *refs, n_cast):
    o_ref, (merged, hid, xs) = _cast_side_job(refs, n_cast)
    d = x_ref.shape[1]
    chunks = [slice(c * MXU_COLS, (c + 1) * MXU_COLS) for c in range(d // MXU_COLS)]

    h = _rms(x_ref[...], gmix_ref[...]).astype(BF16)
    ya = ya_ref[...]
    yb = yb_ref[...]
    for c, cs in enumerate(chunks):
        ga = _dot(h, win_ref[:, GATE_COL + c * MXU_COLS:GATE_COL + (c + 1) * MXU_COLS])
        gb = _dot(h, win_ref[:, GATE_COL + d + c * MXU_COLS:GATE_COL + d + (c + 1) * MXU_COLS])
        pa = _dot(ya, wa_ref[:, cs])
        pb = _dot(yb, wb_ref[:, cs])
        merged[:, cs] = (jax.nn.sigmoid(ga) * pa + jax.nn.sigmoid(gb) * pb).astype(BF16)
    mg = merged[...]
    for cs in chunks:
        xs[:, cs] = x_ref[:, cs] + _dot(mg, wo_ref[:, cs])

    h = _rms(xs[...], gffn_ref[...]).astype(BF16)
    for c in range(wgate_ref.shape[1] // MXU_COLS):
        cs = slice(c * MXU_COLS, (c + 1) * MXU_COLS)
        a = _dot(h, wgate_ref[:, cs])
        u = _dot(h, wup_ref[:, cs])
        hid[:, cs] = (a * jax.nn.sigmoid(a) * u).astype(BF16)
    hv = hid[...]
    for cs in chunks:
        xs[:, cs] = xs[:, cs] + _dot(hv, wdown_ref[:, cs])

    h = _rms(xs[...], gple_ref[...]).astype(BF16)
    pv = p_ref[...].astype(BF16)
    for cs in chunks:
        o_ref[:, cs] = xs[:, cs] + jax.nn.sigmoid(_dot(h, wpg_ref[:, cs])) * _dot(pv, wpp_ref[:, cs])


def _t5_bucket(rel):
    half_b = NUM_BUCKETS // 2
    max_exact = half_b // 2
    sign = jnp.where(rel > 0, half_b, 0)
    n = jnp.abs(rel)
    nf = jnp.maximum(n, 1).astype(F32)
    large = max_exact + (jnp.log(nf / max_exact) / math.log(MAX_DISTANCE / max_exact)
                         * (half_b - max_exact)).astype(jnp.int32)
    large = jnp.minimum(large, half_b - 1)
    return sign + jnp.where(n < max_exact, n, large)


TABLE_PERIOD = 512


def _block_starts(seq_len, win):
    nb = seq_len // BQ
    blocks = (0, min(1, nb - 1), nb - 1) if nb > 1 else (0,)
    return [max(0, min(i * BQ - (win - BQ) // 2, seq_len - win)) - i * BQ for i in blocks]


def _rel_rows(table, starts, win, half, dilation):
    assert BQ + win - 1 <= TABLE_PERIOD
    c = jnp.arange(TABLE_PERIOD, dtype=jnp.int32)
    d = jnp.where(c < win, c, c - TABLE_PERIOD)
    rel = jnp.asarray(starts, jnp.int32)[:, None] + d[None, :]
    bias = table[_t5_bucket(rel * dilation)].astype(F32) * LOG2_E
    return jnp.transpose(jnp.where((jnp.abs(rel) <= half)[:, :, None], bias, NEG_INF), (0, 2, 1))


def _tables_kernel(rows_ref, *out_refs):
    v = 0
    for out_ref in out_refs:
        n_pair, n_var, _, win = out_ref.shape
        for t in range(n_var):
            for h in range(2 * n_pair):
                row = jnp.broadcast_to(rows_ref[v + t, h:h + 1, :], (BQ, TABLE_PERIOD))
                skew = pltpu.roll(row, 0, 1, stride=1, stride_axis=0)
                out_ref[h // 2, t, (h % 2) * BQ:(h % 2 + 1) * BQ, :] = skew[:, :win]
        v += n_var


def _band_tables(rows_and_wins, n_heads):
    rows = jnp.concatenate([r for r, _ in rows_and_wins], axis=0)
    shapes = [jax.ShapeDtypeStruct((n_heads // 2, r.shape[0], 2 * BQ, win), F32) for r, win in rows_and_wins]
    return pl.pallas_call(_tables_kernel, out_shape=shapes, name="tables",
                          compiler_params=pltpu.CompilerParams(vmem_limit_bytes=VMEM_LIMIT))(rows)


def _pair_gain(g, scale=1.0):
    return jnp.tile(g.astype(F32) * scale, 2)


def kernel(x, p, rel_table, norm_mix_g, w_in, qnorm_a_g, knorm_a_g, qnorm_b_g, knorm_b_g, sink_b,
           w_branch_a, w_branch_b, w_out, norm_ffn_g, w_ffn_gate, w_ffn_up, w_ffn_down,
           norm_ple_g, w_ple_gate, w_ple_proj):
    batch, seq, d = x.shape
    depth = p.shape[0]
    t = batch * seq
    q_scale = HEAD_DIM ** -0.5 * LOG2_E

    table_a = rel_table[:, :N_HEADS_A]
    table_b = rel_table[:, N_HEADS_A:]
    patterns = [(seq // dil, min(2 * BQ, seq // dil), w // (2 * dil), dil, table_a) for w, dil in DILATED_PATTERNS]
    patterns.append((seq, BQ + 2 * WINDOW_B, WINDOW_B, 1, table_b))
    rows = [(_rel_rows(tab, _block_starts(sub, win), win, half, dil), win) for sub, win, half, dil, tab in patterns]
    *tabs_a, tab_b = _band_tables(rows, N_HEADS_A)

    rows = lambda g: g.astype(F32).reshape(depth, 1, -1)
    g_mix, g_ffn, g_ple = rows(norm_mix_g), rows(norm_ffn_g), rows(norm_ple_g)
    p2 = p.reshape(depth, t, -1)
    post_weights = (w_branch_a, w_branch_b, w_out, w_ffn_gate, w_ffn_up, w_ffn_down, w_ple_gate, w_ple_proj)

    x2 = x.reshape(t, d)
    w_in_b = w_in[0].astype(BF16)
    post_b = []
    for l in range(depth):
        gains = jnp.stack([_pair_gain(qnorm_a_g[l], q_scale), _pair_gain(knorm_a_g[l]),
                           _pair_gain(qnorm_b_g[l], q_scale), _pair_gain(knorm_b_g[l])])
        qa, ka, va, qb, kb, vb = _qkv_call(x2, g_mix, w_in_b, gains, l)
        ya, cast = _attn_a_call(qa, ka, va, *tabs_a, batch, seq, [(w, 0) for w in post_weights] if l == 0 else [])
        if l == 0:
            post_b = cast
        sink = jnp.repeat(sink_b[l].astype(F32) * LOG2_E, HEAD_DIM).reshape(N_HEADS_B // 2, 1, LANES)
        yb = _attn_b_call(qb, kb, vb, tab_b, sink, batch, seq)
        to_cast = [(w, l + 1) for w in (w_in,) + post_weights] if l + 1 < depth else []
        x2, cast = _post_call(x2, ya, yb, p2, [g_mix, g_ffn, g_ple, w_in_b, *post_b], l, w_ffn_gate.shape[2],
                              to_cast)
        if to_cast:
            w_in_b, post_b = cast[0], cast[1:]
    return x2.reshape(batch, seq, d)
```

```python
import functools
import math

import jax
import jax.numpy as jnp
from jax import lax
from jax.experimental import pallas as pl
from jax.experimental.pallas import tpu as pltpu

F32 = jnp.float32
BF16 = jnp.bfloat16

HEAD_DIM = 64
N_HEADS_A = 8
N_HEADS_B = 8
N_KV_B = 2
DILATED_PATTERNS = ((128, 1), (512, 4), (2048, 16))
WINDOW_B = 128
NUM_BUCKETS = 32
MAX_DISTANCE = 1024
RMS_EPS = 1e-6
NEG_INF = -1e30
LOG2_E = math.log2(math.e)

WIDTH_A = N_HEADS_A * HEAD_DIM
WIDTH_BQ = N_HEADS_B * HEAD_DIM
WIDTH_BKV = N_KV_B * HEAD_DIM
GATE_COL = 3 * WIDTH_A + WIDTH_BQ + 2 * WIDTH_BKV

LANES = 128
MXU_COLS = 256
BF16_SUBLANES = 16
BQ = 128
TM = 512
VMEM_LIMIT = 56 * 1024 * 1024


def _dot(a, b):
    return jnp.dot(a, b, preferred_element_type=F32)


def _rms(x, g):
    ms = jnp.mean(x * x, axis=-1, keepdims=True)
    return x * lax.rsqrt(ms + RMS_EPS) * g


def _low_lanes():
    return lax.broadcasted_iota(jnp.int32, (1, LANES), 1) < HEAD_DIM


def _head_rms(y, gain, lo):
    sq = y * y
    s0 = jnp.sum(jnp.where(lo, sq, 0.0), axis=-1, keepdims=True)
    s1 = jnp.sum(jnp.where(lo, 0.0, sq), axis=-1, keepdims=True)
    ms = jnp.where(lo, s0, s1) * (1.0 / HEAD_DIM)
    return y * lax.rsqrt(ms + RMS_EPS) * gain


def _const_spec(shape):
    nd = len(shape)
    return pl.BlockSpec(shape, lambda *_: (0,) * nd, pipeline_mode=pl.Buffered(1))


def _layer_spec(stacked, layer):
    nd = stacked.ndim - 1
    return pl.BlockSpec((None,) + stacked.shape[1:], lambda *_: (layer,) + (0,) * nd,
                        pipeline_mode=pl.Buffered(1))


def _qkv_kernel(x_ref, g_ref, w_ref, gains_ref, qa_ref, ka_ref, va_ref, qb_ref, kb_ref, vb_ref):
    lo = _low_lanes()
    h = _rms(x_ref[...], g_ref[...]).astype(BF16)
    n_pair = WIDTH_A // MXU_COLS
    order = sorted(range(GATE_COL // MXU_COLS), key=lambda c: 2 * n_pair <= c < 3 * n_pair)
    for c in order:
        y = _dot(h, w_ref[:, c * MXU_COLS:(c + 1) * MXU_COLS].astype(BF16))
        y0, y1 = y[:, :LANES], y[:, LANES:]
        if c < n_pair:
            qa_ref[2 * c] = _head_rms(y0, gains_ref[0:1, :], lo)
            qa_ref[2 * c + 1] = _head_rms(y1, gains_ref[0:1, :], lo)
        elif c < 2 * n_pair:
            ka_ref[2 * (c - n_pair)] = _head_rms(y0, gains_ref[1:2, :], lo)
            ka_ref[2 * (c - n_pair) + 1] = _head_rms(y1, gains_ref[1:2, :], lo)
        elif c < 3 * n_pair:
            va_ref[2 * (c - 2 * n_pair)] = y0
            va_ref[2 * (c - 2 * n_pair) + 1] = y1
        elif c < 4 * n_pair:
            qb_ref[2 * (c - 3 * n_pair)] = _head_rms(y0, gains_ref[2:3, :], lo).astype(BF16)
            qb_ref[2 * (c - 3 * n_pair) + 1] = _head_rms(y1, gains_ref[2:3, :], lo).astype(BF16)
        else:
            kn = _head_rms(y0, gains_ref[3:4, :], lo)
            ks = pltpu.roll(kn, HEAD_DIM, axis=1)
            vs = pltpu.roll(y1, HEAD_DIM, axis=1)
            kb_ref[0] = jnp.where(lo, kn, ks).astype(BF16)
            kb_ref[1] = jnp.where(lo, ks, kn).astype(BF16)
            vb_ref[0] = jnp.where(lo, y1, vs).astype(BF16)
            vb_ref[1] = jnp.where(lo, vs, y1).astype(BF16)


def _cast_blocks(n_rows, n_steps):
    for n_blocks in range(n_steps, 0, -1):
        if n_steps % n_blocks == 0 and n_rows % (n_blocks * BF16_SUBLANES) == 0:
            return n_rows // n_blocks, n_steps // n_blocks
    raise ValueError(f"cannot split {n_rows} rows into bf16 row blocks over {n_steps} steps")


def _cast_specs(to_cast, n_steps, step_of=lambda i: i):
    in_specs, out_specs, shapes = [], [], []
    for w, w_layer in to_cast:
        _, n_rows, n_cols = w.shape
        rb, hold = _cast_blocks(n_rows, n_steps)
        in_specs.append(pl.BlockSpec((None, rb, n_cols),
                                     lambda *idx, w_layer=w_layer, hold=hold: (w_layer, step_of(*idx) // hold, 0)))
        out_specs.append(pl.BlockSpec((rb, n_cols), lambda *idx, hold=hold: (step_of(*idx) // hold, 0)))
        shapes.append(jax.ShapeDtypeStruct((n_rows, n_cols), BF16))
    return in_specs, out_specs, shapes


def _cast_side_job(refs, n_cast):
    for src, dst in zip(refs[:n_cast], refs[n_cast + 1:2 * n_cast + 1]):
        dst[...] = src[...].astype(BF16)
    return refs[n_cast], refs[2 * n_cast + 1:]


def _qkv_call(x2, g, w_in_layer, gains, layer):
    t, d = x2.shape
    n_q = WIDTH_A // LANES
    n_kv = N_KV_B
    assert WIDTH_BQ == WIDTH_A and 2 * WIDTH_BKV == MXU_COLS
    slab = lambda n, dt: jax.ShapeDtypeStruct((n, t, LANES), dt)
    slab_spec = lambda n: pl.BlockSpec((n, TM, LANES), lambda i: (0, i, 0))
    return pl.pallas_call(
        _qkv_kernel,
        grid=(t // TM,),
        in_specs=[pl.BlockSpec((TM, d), lambda i: (i, 0)), _layer_spec(g, layer),
                  _layer_spec(w_in_layer, layer) if w_in_layer.ndim == 3 else _const_spec(w_in_layer.shape),
                  _const_spec(gains.shape)],
        out_specs=[slab_spec(n_q), slab_spec(n_q), slab_spec(n_q), slab_spec(n_q),
                   slab_spec(n_kv), slab_spec(n_kv)],
        out_shape=[slab(n_q, F32), slab(n_q, F32), slab(n_q, F32), slab(n_q, BF16),
                   slab(n_kv, BF16), slab(n_kv, BF16)],
        compiler_params=pltpu.CompilerParams(dimension_semantics=("arbitrary",),
                                             vmem_limit_bytes=VMEM_LIMIT),
        name="qkv",
    )(x2, g, w_in_layer, gains)


def _attn_block(q, kw, vw, tab, lo):
    zero = jnp.zeros_like(q)
    q2 = jnp.concatenate([jnp.where(lo, q, zero), jnp.where(lo, zero, q)], axis=0).astype(BF16)
    s = lax.dot_general(q2, kw.astype(BF16), (((1,), (1,)), ((), ())),
                        preferred_element_type=F32) + tab
    m = jnp.max(s, axis=-1, keepdims=True)
    p = jnp.exp2(s - m)
    l = jnp.sum(p, axis=-1, keepdims=True)
    o2 = _dot(p.astype(BF16), vw.astype(BF16))
    return o2, m, l


def _merge_heads(a2, lo):
    bq = a2.shape[0] // 2
    return jnp.where(lo, a2[:bq], a2[bq:])


def _block_geometry(i, n_blocks, seq_len, win):
    w0 = max(0, min(i * BQ - (win - BQ) // 2, seq_len - win))
    return pl.ds(i * BQ, BQ), pl.ds(w0, win), (0 if i == 0 else (2 if i == n_blocks - 1 else 1))


def _attn_a_kernel(q_ref, k_ref, v_ref, t1_ref, t4_ref, t16_ref, *refs, n_cast):
    out_ref, (o1, lse1, outf, qs, ks, vs, o4, lse4, o16g, lse16g, o16, lse16, outc) = _cast_side_job(refs, n_cast)
    lo = _low_lanes()
    seq = q_ref.shape[0]
    win = t1_ref.shape[-1]
    d4 = DILATED_PATTERNS[1][1]
    step = DILATED_PATTERNS[2][1] // d4
    sub4 = seq // d4

    def block(qr, kr, vr, q_rows, kv_rows, tab, o_out, lse_out, out_rows):
        o2, m, l = _attn_block(qr[q_rows, :], kr[kv_rows, :], vr[kv_rows, :], tab, lo)
        l = _merge_heads(l, lo)
        o_out[out_rows, :] = _merge_heads(o2, lo) / l
        lse_out[out_rows, :] = _merge_heads(m, lo) + jnp.log2(l)

    for r in range(d4):
        res = pl.ds(r, sub4, stride=d4)
        qs[r] = q_ref[res, :]
        ks[r] = k_ref[res, :]
        vs[r] = v_ref[res, :]

    nb = seq // BQ
    for i in range(nb):
        q_rows, kv_rows, tidx = _block_geometry(i, nb, seq, win)
        block(q_ref, k_ref, v_ref, q_rows, kv_rows, t1_ref[tidx], o1, lse1, q_rows)

    nb4 = sub4 // BQ
    for i in range(nb4):
        for r in range(d4):
            q_rows, kv_rows, tidx = _block_geometry(i, nb4, sub4, win)
            block(qs.at[r], ks.at[r], vs.at[r], q_rows, kv_rows, t4_ref[tidx], o4.at[r], lse4.at[r], q_rows)

    for b in range(step):
        for r in range(d4):
            rows = pl.ds(b, BQ, stride=step)
            block(qs.at[r], ks.at[r], vs.at[r], rows, rows, t16_ref[0], o16g.at[r], lse16g.at[r],
                  pl.ds(b * BQ, BQ))

    def merge_class(r, carry):
        for grouped, ordered in ((o16g, o16), (lse16g, lse16)):
            for b in range(step):
                ordered[r, pl.ds(b, BQ, stride=step), :] = grouped[r, pl.ds(b * BQ, BQ), :]
        for i in range(nb4):
            rows = pl.ds(i * BQ, BQ)
            nat = pl.ds(r + d4 * i * BQ, BQ, stride=d4)
            la, lb, lc = lse1[nat, :], lse4[r, rows, :], lse16[r, rows, :]
            mx = jnp.maximum(jnp.maximum(la, lb), lc)
            wa, wb, wc = jnp.exp2(la - mx), jnp.exp2(lb - mx), jnp.exp2(lc - mx)
            num = wa * o1[nat, :] + wb * o4[r, rows, :] + wc * o16[r, rows, :]
            outc[r, rows, :] = num / (wa + wb + wc)
        outf[pl.ds(r, sub4, stride=d4), :] = outc[r]
        return carry

    lax.fori_loop(0, d4, merge_class, 0)
    out_ref[...] = outf[...].astype(out_ref.dtype)


def _attn_a_call(qa, ka, va, t1, t4, t16, batch, seq, to_cast):
    n_slab, t, _ = qa.shape
    assert seq // DILATED_PATTERNS[2][1] == BQ
    qkv_spec = pl.BlockSpec((None, seq, LANES), lambda hp, b: (hp, b, 0))
    tab_spec = lambda tab: pl.BlockSpec((None,) + tab.shape[1:], lambda hp, b: (hp, 0, 0, 0))
    d4 = DILATED_PATTERNS[1][1]
    full = pltpu.VMEM((seq, LANES), F32)
    by_class = pltpu.VMEM((d4, seq // d4, LANES), F32)
    cast_in_specs, cast_out_specs, cast_shapes = _cast_specs(to_cast, n_slab * batch, lambda hp, b: hp * batch + b)
    outs = pl.pallas_call(
        functools.partial(_attn_a_kernel, n_cast=len(to_cast)),
        grid=(n_slab, batch),
        in_specs=[qkv_spec, qkv_spec, qkv_spec, tab_spec(t1), tab_spec(t4), tab_spec(t16)] + cast_in_specs,
        out_specs=[pl.BlockSpec((seq, LANES), lambda hp, b: (b, hp))] + cast_out_specs,
        out_shape=[jax.ShapeDtypeStruct((t, n_slab * LANES), BF16)] + cast_shapes,
        scratch_shapes=[full] * 3 + [by_class] * 10,
        compiler_params=pltpu.CompilerParams(dimension_semantics=("arbitrary", "arbitrary"),
                                             vmem_limit_bytes=VMEM_LIMIT),
        name="attn_a",
    )(qa, ka, va, t1, t4, t16, *[w for w, _ in to_cast])
    return outs[0], outs[1:]


def _attn_b_kernel(q_ref, k_ref, v_ref, tab_ref, sink_ref, o_ref):
    lo = _low_lanes()
    n_slab, seq, _ = q_ref.shape
    win = tab_ref.shape[-1]
    nb = seq // BQ
    zero = jnp.zeros((BQ, LANES), q_ref.dtype)

    for i in range(nb):
        q_rows, kv_rows, tidx = _block_geometry(i, nb, seq, win)
        q = [q_ref[sl, q_rows, :] for sl in range(n_slab)]
        q_all = jnp.concatenate([part for qs in q for part in (jnp.where(lo, qs, zero), jnp.where(lo, zero, qs))],
                                axis=0)
        tab = jnp.concatenate([tab_ref[sl, tidx] for sl in range(n_slab)], axis=0)
        s = lax.dot_general(q_all, k_ref[kv_rows, :], (((1,), (1,)), ((), ())), preferred_element_type=F32) + tab
        m = jnp.max(s, axis=-1, keepdims=True)
        p = jnp.exp2(s - m)
        vw = v_ref[kv_rows, :]
        o2 = _dot(p.astype(BF16), jnp.concatenate([vw, jnp.ones(vw.shape, BF16)], axis=1))
        for sl in range(n_slab):
            rows = slice(sl * 2 * BQ, (sl + 1) * 2 * BQ)
            o = _merge_heads(o2[rows, :LANES], lo)
            mm = _merge_heads(m[rows], lo)
            l = _merge_heads(o2[rows, LANES:], lo)
            sink = sink_ref[sl]
            m_new = jnp.maximum(mm, sink)
            scale = jnp.exp2(mm - m_new)
            den = l * scale + jnp.exp2(sink - m_new)
            o_ref[q_rows, sl * LANES:(sl + 1) * LANES] = (o * (scale / den)).astype(o_ref.dtype)


def _attn_b_call(qb, kb, vb, tab, sink, batch, seq):
    n_slab, t, _ = qb.shape
    n_kv = kb.shape[0]
    group = n_slab // n_kv
    kv_spec = pl.BlockSpec((None, seq, LANES), lambda kv, b: (kv, b, 0))
    return pl.pallas_call(
        _attn_b_kernel,
        grid=(n_kv, batch),
        in_specs=[pl.BlockSpec((group, seq, LANES), lambda kv, b: (kv, b, 0)), kv_spec, kv_spec,
                  pl.BlockSpec((group,) + tab.shape[1:], lambda kv, b: (kv, 0, 0, 0)),
                  pl.BlockSpec((group,) + sink.shape[1:], lambda kv, b: (kv, 0, 0))],
        out_specs=pl.BlockSpec((seq, group * LANES), lambda kv, b: (b, kv)),
        out_shape=jax.ShapeDtypeStruct((t, n_slab * LANES), BF16),
        compiler_params=pltpu.CompilerParams(dimension_semantics=("arbitrary", "arbitrary"),
                                             vmem_limit_bytes=VMEM_LIMIT),
        name="attn_b",
    )(qb, kb, vb, tab, sink)


def _post_call(x2, ya, yb, p2, resident, layer, d_ff, to_cast):
    t, d = x2.shape
    n_steps = t // TM
    row_spec = lambda a: pl.BlockSpec((TM, a.shape[1]), lambda i: (i, 0))
    cast_in_specs, cast_out_specs, cast_shapes = _cast_specs(to_cast, n_steps)
    outs = pl.pallas_call(
        functools.partial(_post_kernel, n_cast=len(to_cast)),
        grid=(n_steps,),
        in_specs=[row_spec(x2), row_spec(ya), row_spec(yb),
                  pl.BlockSpec((None, TM, p2.shape[2]), lambda i: (layer, i, 0))]
        + [_layer_spec(a, layer) if a.ndim == 3 else _const_spec(a.shape) for a in resident] + cast_in_specs,
        out_specs=[pl.BlockSpec((TM, d), lambda i: (i, 0))] + cast_out_specs,
        out_shape=[jax.ShapeDtypeStruct((t, d), F32)] + cast_shapes,
        scratch_shapes=[pltpu.VMEM((TM, d), BF16), pltpu.VMEM((TM, d_ff), BF16), pltpu.VMEM((TM, d), F32)],
        compiler_params=pltpu.CompilerParams(dimension_semantics=("arbitrary",),
                                             vmem_limit_bytes=VMEM_LIMIT),
        name="post",
    )(x2, ya, yb, p2, *resident, *[w for w, _ in to_cast])
    return outs[0], outs[1:]


def _post_kernel(x_ref, ya_ref, yb_ref, p_ref, gmix_ref, gffn_ref, gple_ref, win_ref, wa_ref, wb_ref, wo_ref,
                 wgate_ref, wup_ref, wdown_ref, wpg_ref, wpp_ref, *refs, n_cast):
    o_ref, (merged, hid, xs) = _cast_side_job(refs, n_cast)
    d = x_ref.shape[1]
    chunks = [slice(c * MXU_COLS, (c + 1) * MXU_COLS) for c in range(d // MXU_COLS)]

    h = _rms(x_ref[...], gmix_ref[...]).astype(BF16)
    ya = ya_ref[...]
    yb = yb_ref[...]
    for c, cs in enumerate(chunks):
        ga = _dot(h, win_ref[:, GATE_COL + c * MXU_COLS:GATE_COL + (c + 1) * MXU_COLS])
        gb = _dot(h, win_ref[:, GATE_COL + d + c * MXU_COLS:GATE_COL + d + (c + 1) * MXU_COLS])
        pa = _dot(ya, wa_ref[:, cs])
        pb = _dot(yb, wb_ref[:, cs])
        merged[:, cs] = (jax.nn.sigmoid(ga) * pa + jax.nn.sigmoid(gb) * pb).astype(BF16)
    mg = merged[...]
    for cs in chunks:
        xs[:, cs] = x_ref[:, cs] + _dot(mg, wo_ref[:, cs])

    h = _rms(xs[...], gffn_ref[...]).astype(BF16)
    for c in range(wgate_ref.shape[1] // MXU_COLS):
        cs = slice(c * MXU_COLS, (c + 1) * MXU_COLS)
        a = _dot(h, wgate_ref[:, cs])
        u = _dot(h, wup_ref[:, cs])
        hid[:, cs] = (a * jax.nn.sigmoid(a) * u).astype(BF16)
    hv = hid[...]
    for cs in chunks:
        xs[:, cs] = xs[:, cs] + _dot(hv, wdown_ref[:, cs])

    h = _rms(xs[...], gple_ref[...]).astype(BF16)
    pv = p_ref[...].astype(BF16)
    for cs in chunks:
        o_ref[:, cs] = xs[:, cs] + jax.nn.sigmoid(_dot(h, wpg_ref[:, cs])) * _dot(pv, wpp_ref[:, cs])


def _t5_bucket(rel):
    half_b = NUM_BUCKETS // 2
    max_exact = half_b // 2
    sign = jnp.where(rel > 0, half_b, 0)
    n = jnp.abs(rel)
    nf = jnp.maximum(n, 1).astype(F32)
    large = max_exact + (jnp.log(nf / max_exact) / math.log(MAX_DISTANCE / max_exact)
                         * (half_b - max_exact)).astype(jnp.int32)
    large = jnp.minimum(large, half_b - 1)
    return sign + jnp.where(n < max_exact, n, large)


TABLE_PERIOD = 512


def _block_starts(seq_len, win):
    nb = seq_len // BQ
    blocks = (0, min(1, nb - 1), nb - 1) if nb > 1 else (0,)
    return [max(0, min(i * BQ - (win - BQ) // 2, seq_len - win)) - i * BQ for i in blocks]


def _rel_rows(table, starts, win, half, dilation):
    assert BQ + win - 1 <= TABLE_PERIOD
    c = jnp.arange(TABLE_PERIOD, dtype=jnp.int32)
    d = jnp.where(c < win, c, c - TABLE_PERIOD)
    rel = jnp.asarray(starts, jnp.int32)[:, None] + d[None, :]
    bias = table[_t5_bucket(rel * dilation)].astype(F32) * LOG2_E
    return jnp.transpose(jnp.where((jnp.abs(rel) <= half)[:, :, None], bias, NEG_INF), (0, 2, 1))


def _tables_kernel(rows_ref, *out_refs):
    v = 0
    for out_ref in out_refs:
        n_pair, n_var, _, win = out_ref.shape
        for t in range(n_var):
            for h in range(2 * n_pair):
                row = jnp.broadcast_to(rows_ref[v + t, h:h + 1, :], (BQ, TABLE_PERIOD))
                skew = pltpu.roll(row, 0, 1, stride=1, stride_axis=0)
                out_ref[h // 2, t, (h % 2) * BQ:(h % 2 + 1) * BQ, :] = skew[:, :win]
        v += n_var


def _band_tables(rows_and_wins, n_heads):
    rows = jnp.concatenate([r for r, _ in rows_and_wins], axis=0)
    shapes = [jax.ShapeDtypeStruct((n_heads // 2, r.shape[0], 2 * BQ, win), F32) for r, win in rows_and_wins]
    return pl.pallas_call(_tables_kernel, out_shape=shapes, name="tables",
                          compiler_params=pltpu.CompilerParams(vmem_limit_bytes=VMEM_LIMIT))(rows)


def _pair_gain(g, scale=1.0):
    return jnp.tile(g.astype(F32) * scale, 2)


def kernel(x, p, rel_table, norm_mix_g, w_in, qnorm_a_g, knorm_a_g, qnorm_b_g, knorm_b_g, sink_b,
           w_branch_a, w_branch_b, w_out, norm_ffn_g, w_ffn_gate, w_ffn_up, w_ffn_down,
           norm_ple_g, w_ple_gate, w_ple_proj):
    batch, seq, d = x.shape
    depth = p.shape[0]
    t = batch * seq
    q_scale = HEAD_DIM ** -0.5 * LOG2_E

    table_a = rel_table[:, :N_HEADS_A]
    table_b = rel_table[:, N_HEADS_A:]
    patterns = [(seq // dil, min(2 * BQ, seq // dil), w // (2 * dil), dil, table_a) for w, dil in DILATED_PATTERNS]
    patterns.append((seq, BQ + 2 * WINDOW_B, WINDOW_B, 1, table_b))
    rows = [(_rel_rows(tab, _block_starts(sub, win), win, half, dil), win) for sub, win, half, dil, tab in patterns]
    *tabs_a, tab_b = _band_tables(rows, N_HEADS_A)

    rows = lambda g: g.astype(F32).reshape(depth, 1, -1)
    g_mix, g_ffn, g_ple = rows(norm_mix_g), rows(norm_ffn_g), rows(norm_ple_g)
    p2 = p.reshape(depth, t, -1)
    post_weights = (w_branch_a, w_branch_b, w_out, w_ffn_gate, w_ffn_up, w_ffn_down, w_ple_gate, w_ple_proj)

    x2 = x.reshape(t, d)
    w_in_b, post_b = None, []
    for l in range(depth):
        gains = jnp.stack([_pair_gain(qnorm_a_g[l], q_scale), _pair_gain(knorm_a_g[l]),
                           _pair_gain(qnorm_b_g[l], q_scale), _pair_gain(knorm_b_g[l])])
        qa, ka, va, qb, kb, vb = _qkv_call(x2, g_mix, w_in if l == 0 else w_in_b, gains, l)
        ya, cast = _attn_a_call(qa, ka, va, *tabs_a, batch, seq,
                                [(w, 0) for w in (w_in,) + post_weights] if l == 0 else [])
        if l == 0:
            w_in_b, post_b = cast[0], cast[1:]
        sink = jnp.repeat(sink_b[l].astype(F32) * LOG2_E, HEAD_DIM).reshape(N_HEADS_B // 2, 1, LANES)
        yb = _attn_b_call(qb, kb, vb, tab_b, sink, batch, seq)
        to_cast = [(w, l + 1) for w in (w_in,) + post_weights] if l + 1 < depth else []
        x2, cast = _post_call(x2, ya, yb, p2, [g_mix, g_ffn, g_ple, w_in_b, *post_b], l, w_ffn_gate.shape[2],
                              to_cast)
        if to_cast:
            w_in_b, post_b = cast[0], cast[1:]
    return x2.reshape(batch, seq, d)
```

```python
import functools
import math

import jax
import jax.numpy as jnp
from jax import lax
from jax.experimental import pallas as pl
from jax.experimental.pallas import tpu as pltpu

F32 = jnp.float32
BF16 = jnp.bfloat16

HEAD_DIM = 64
N_HEADS_A = 8
N_HEADS_B = 8
N_KV_B = 2
DILATED_PATTERNS = ((128, 1), (512, 4), (2048, 16))
WINDOW_B = 128
NUM_BUCKETS = 32
MAX_DISTANCE = 1024
RMS_EPS = 1e-6
NEG_INF = -1e30
LOG2_E = math.log2(math.e)

WIDTH_A = N_HEADS_A * HEAD_DIM
WIDTH_BQ = N_HEADS_B * HEAD_DIM
WIDTH_BKV = N_KV_B * HEAD_DIM
GATE_COL = 3 * WIDTH_A + WIDTH_BQ + 2 * WIDTH_BKV

LANES = 128
MXU_COLS = 256
BF16_SUBLANES = 16
BQ = 128
TM = 512
VMEM_LIMIT = 56 * 1024 * 1024


def _dot(a, b):
    return jnp.dot(a, b, preferred_element_type=F32)


def _rms(x, g):
    ms = jnp.mean(x * x, axis=-1, keepdims=True)
    return x * lax.rsqrt(ms + RMS_EPS) * g


def _low_lanes():
    return lax.broadcasted_iota(jnp.int32, (1, LANES), 1) < HEAD_DIM


def _head_rms(y, gain, lo):
    sq = y * y
    s0 = jnp.sum(jnp.where(lo, sq, 0.0), axis=-1, keepdims=True)
    s1 = jnp.sum(jnp.where(lo, 0.0, sq), axis=-1, keepdims=True)
    ms = jnp.where(lo, s0, s1) * (1.0 / HEAD_DIM)
    return y * lax.rsqrt(ms + RMS_EPS) * gain


def _const_spec(shape):
    nd = len(shape)
    return pl.BlockSpec(shape, lambda *_: (0,) * nd, pipeline_mode=pl.Buffered(1))


def _layer_spec(stacked, layer):
    nd = stacked.ndim - 1
    return pl.BlockSpec((None,) + stacked.shape[1:], lambda *_: (layer,) + (0,) * nd,
                        pipeline_mode=pl.Buffered(1))


def _qkv_kernel(x_ref, g_ref, w_ref, gains_ref, qa_ref, ka_ref, va_ref, qb_ref, kb_ref, vb_ref):
    lo = _low_lanes()
    h = _rms(x_ref[...], g_ref[...]).astype(BF16)
    n_pair = WIDTH_A // MXU_COLS
    order = sorted(range(GATE_COL // MXU_COLS), key=lambda c: 2 * n_pair <= c < 3 * n_pair)
    for c in order:
        y = _dot(h, w_ref[:, c * MXU_COLS:(c + 1) * MXU_COLS].astype(BF16))
        y0, y1 = y[:, :LANES], y[:, LANES:]
        if c < n_pair:
            qa_ref[2 * c] = _head_rms(y0, gains_ref[0:1, :], lo)
            qa_ref[2 * c + 1] = _head_rms(y1, gains_ref[0:1, :], lo)
        elif c < 2 * n_pair:
            ka_ref[2 * (c - n_pair)] = _head_rms(y0, gains_ref[1:2, :], lo)
            ka_ref[2 * (c - n_pair) + 1] = _head_rms(y1, gains_ref[1:2, :], lo)
        elif c < 3 * n_pair:
            va_ref[2 * (c - 2 * n_pair)] = y0
            va_ref[2 * (c - 2 * n_pair) + 1] = y1
        elif c < 4 * n_pair:
            qb_ref[2 * (c - 3 * n_pair)] = _head_rms(y0, gains_ref[2:3, :], lo).astype(BF16)
            qb_ref[2 * (c - 3 * n_pair) + 1] = _head_rms(y1, gains_ref[2:3, :], lo).astype(BF16)
        else:
            kn = _head_rms(y0, gains_ref[3:4, :], lo)
            ks = pltpu.roll(kn, HEAD_DIM, axis=1)
            vs = pltpu.roll(y1, HEAD_DIM, axis=1)
            kb_ref[0] = jnp.where(lo, kn, ks).astype(BF16)
            kb_ref[1] = jnp.where(lo, ks, kn).astype(BF16)
            vb_ref[0] = jnp.where(lo, y1, vs).astype(BF16)
            vb_ref[1] = jnp.where(lo, vs, y1).astype(BF16)


def _cast_blocks(n_rows, n_steps):
    for n_blocks in range(n_steps, 0, -1):
        if n_steps % n_blocks == 0 and n_rows % (n_blocks * BF16_SUBLANES) == 0:
            return n_rows // n_blocks, n_steps // n_blocks
    raise ValueError(f"cannot split {n_rows} rows into bf16 row blocks over {n_steps} steps")


def _cast_specs(to_cast, n_steps, step_of=lambda i: i):
    in_specs, out_specs, shapes = [], [], []
    for w, w_layer in to_cast:
        _, n_rows, n_cols = w.shape
        rb, hold = _cast_blocks(n_rows, n_steps)
        in_specs.append(pl.BlockSpec((None, rb, n_cols),
                                     lambda *idx, w_layer=w_layer, hold=hold: (w_layer, step_of(*idx) // hold, 0)))
        out_specs.append(pl.BlockSpec((rb, n_cols), lambda *idx, hold=hold: (step_of(*idx) // hold, 0)))
        shapes.append(jax.ShapeDtypeStruct((n_rows, n_cols), BF16))
    return in_specs, out_specs, shapes


def _cast_side_job(refs, n_cast):
    for src, dst in zip(refs[:n_cast], refs[n_cast + 1:2 * n_cast + 1]):
        dst[...] = src[...].astype(BF16)
    return refs[n_cast], refs[2 * n_cast + 1:]


def _qkv_call(x2, g, w_in_layer, gains, layer):
    t, d = x2.shape
    n_q = WIDTH_A // LANES
    n_kv = N_KV_B
    assert WIDTH_BQ == WIDTH_A and 2 * WIDTH_BKV == MXU_COLS
    slab = lambda n, dt: jax.ShapeDtypeStruct((n, t, LANES), dt)
    slab_spec = lambda n: pl.BlockSpec((n, TM, LANES), lambda i: (0, i, 0))
    return pl.pallas_call(
        _qkv_kernel,
        grid=(t // TM,),
        in_specs=[pl.BlockSpec((TM, d), lambda i: (i, 0)), _layer_spec(g, layer),
                  _layer_spec(w_in_layer, layer) if w_in_layer.ndim == 3 else _const_spec(w_in_layer.shape),
                  _const_spec(gains.shape)],
        out_specs=[slab_spec(n_q), slab_spec(n_q), slab_spec(n_q), slab_spec(n_q),
                   slab_spec(n_kv), slab_spec(n_kv)],
        out_shape=[slab(n_q, F32), slab(n_q, F32), slab(n_q, F32), slab(n_q, BF16),
                   slab(n_kv, BF16), slab(n_kv, BF16)],
        compiler_params=pltpu.CompilerParams(dimension_semantics=("arbitrary",),
                                             vmem_limit_bytes=VMEM_LIMIT),
        name="qkv",
    )(x2, g, w_in_layer, gains)


def _attn_block(q, kw, vw, tab, lo):
    zero = jnp.zeros_like(q)
    q2 = jnp.concatenate([jnp.where(lo, q, zero), jnp.where(lo, zero, q)], axis=0).astype(BF16)
    s = lax.dot_general(q2, kw.astype(BF16), (((1,), (1,)), ((), ())),
                        preferred_element_type=F32) + tab
    m = jnp.max(s, axis=-1, keepdims=True)
    p = jnp.exp2(s - m)
    l = jnp.sum(p, axis=-1, keepdims=True)
    o2 = _dot(p.astype(BF16), vw.astype(BF16))
    return o2, m, l


def _merge_heads(a2, lo):
    bq = a2.shape[0] // 2
    return jnp.where(lo, a2[:bq], a2[bq:])


def _block_geometry(i, n_blocks, seq_len, win):
    w0 = max(0, min(i * BQ - (win - BQ) // 2, seq_len - win))
    return pl.ds(i * BQ, BQ), pl.ds(w0, win), (0 if i == 0 else (2 if i == n_blocks - 1 else 1))


def _attn_a_kernel(q_ref, k_ref, v_ref, t1_ref, t4_ref, t16_ref, out_ref,
                   o1, lse1, outf, qs, ks, vs, o4, lse4, o16g, lse16g, o16, lse16, outc):
    lo = _low_lanes()
    seq = q_ref.shape[0]
    win = t1_ref.shape[-1]
    d4 = DILATED_PATTERNS[1][1]
    step = DILATED_PATTERNS[2][1] // d4
    sub4 = seq // d4

    def block(qr, kr, vr, q_rows, kv_rows, tab, o_out, lse_out, out_rows):
        o2, m, l = _attn_block(qr[q_rows, :], kr[kv_rows, :], vr[kv_rows, :], tab, lo)
        l = _merge_heads(l, lo)
        o_out[out_rows, :] = _merge_heads(o2, lo) / l
        lse_out[out_rows, :] = _merge_heads(m, lo) + jnp.log2(l)

    for r in range(d4):
        res = pl.ds(r, sub4, stride=d4)
        qs[r] = q_ref[res, :]
        ks[r] = k_ref[res, :]
        vs[r] = v_ref[res, :]

    nb = seq // BQ
    for i in range(nb):
        q_rows, kv_rows, tidx = _block_geometry(i, nb, seq, win)
        block(q_ref, k_ref, v_ref, q_rows, kv_rows, t1_ref[tidx], o1, lse1, q_rows)

    nb4 = sub4 // BQ
    for i in range(nb4):
        for r in range(d4):
            q_rows, kv_rows, tidx = _block_geometry(i, nb4, sub4, win)
            block(qs.at[r], ks.at[r], vs.at[r], q_rows, kv_rows, t4_ref[tidx], o4.at[r], lse4.at[r], q_rows)

    for b in range(step):
        for r in range(d4):
            rows = pl.ds(b, BQ, stride=step)
            block(qs.at[r], ks.at[r], vs.at[r], rows, rows, t16_ref[0], o16g.at[r], lse16g.at[r],
                  pl.ds(b * BQ, BQ))

    def merge_class(r, carry):
        for grouped, ordered in ((o16g, o16), (lse16g, lse16)):
            for b in range(step):
                ordered[r, pl.ds(b, BQ, stride=step), :] = grouped[r, pl.ds(b * BQ, BQ), :]
        for i in range(nb4):
            rows = pl.ds(i * BQ, BQ)
            nat = pl.ds(r + d4 * i * BQ, BQ, stride=d4)
            la, lb, lc = lse1[nat, :], lse4[r, rows, :], lse16[r, rows, :]
            mx = jnp.maximum(jnp.maximum(la, lb), lc)
            wa, wb, wc = jnp.exp2(la - mx), jnp.exp2(lb - mx), jnp.exp2(lc - mx)
            num = wa * o1[nat, :] + wb * o4[r, rows, :] + wc * o16[r, rows, :]
            outc[r, rows, :] = num / (wa + wb + wc)
        outf[pl.ds(r, sub4, stride=d4), :] = outc[r]
        return carry

    lax.fori_loop(0, d4, merge_class, 0)
    out_ref[...] = outf[...].astype(out_ref.dtype)


def _attn_a_call(qa, ka, va, t1, t4, t16, batch, seq):
    n_slab, t, _ = qa.shape
    assert seq // DILATED_PATTERNS[2][1] == BQ
    qkv_spec = pl.BlockSpec((None, seq, LANES), lambda hp, b: (hp, b, 0))
    tab_spec = lambda tab: pl.BlockSpec((None,) + tab.shape[1:], lambda hp, b: (hp, 0, 0, 0))
    d4 = DILATED_PATTERNS[1][1]
    full = pltpu.VMEM((seq, LANES), F32)
    by_class = pltpu.VMEM((d4, seq // d4, LANES), F32)
    return pl.pallas_call(
        _attn_a_kernel,
        grid=(n_slab, batch),
        in_specs=[qkv_spec, qkv_spec, qkv_spec, tab_spec(t1), tab_spec(t4), tab_spec(t16)],
        out_specs=pl.BlockSpec((seq, LANES), lambda hp, b: (b, hp)),
        out_shape=jax.ShapeDtypeStruct((t, n_slab * LANES), BF16),
        scratch_shapes=[full] * 3 + [by_class] * 10,
        compiler_params=pltpu.CompilerParams(dimension_semantics=("arbitrary", "arbitrary"),
                                             vmem_limit_bytes=VMEM_LIMIT),
        name="attn_a",
    )(qa, ka, va, t1, t4, t16)


def _attn_b_kernel(q_ref, k_ref, v_ref, tab_ref, sink_ref, *refs, n_cast):
    o_ref, _ = _cast_side_job(refs, n_cast)
    lo = _low_lanes()
    n_slab, seq, _ = q_ref.shape
    win = tab_ref.shape[-1]
    nb = seq // BQ
    zero = jnp.zeros((BQ, LANES), q_ref.dtype)

    for i in range(nb):
        q_rows, kv_rows, tidx = _block_geometry(i, nb, seq, win)
        q = [q_ref[sl, q_rows, :] for sl in range(n_slab)]
        q_all = jnp.concatenate([part for qs in q for part in (jnp.where(lo, qs, zero), jnp.where(lo, zero, qs))],
                                axis=0)
        tab = jnp.concatenate([tab_ref[sl, tidx] for sl in range(n_slab)], axis=0)
        s = lax.dot_general(q_all, k_ref[kv_rows, :], (((1,), (1,)), ((), ())), preferred_element_type=F32) + tab
        m = jnp.max(s, axis=-1, keepdims=True)
        p = jnp.exp2(s - m)
        vw = v_ref[kv_rows, :]
        o2 = _dot(p.astype(BF16), jnp.concatenate([vw, jnp.ones(vw.shape, BF16)], axis=1))
        for sl in range(n_slab):
            rows = slice(sl * 2 * BQ, (sl + 1) * 2 * BQ)
            o = _merge_heads(o2[rows, :LANES], lo)
            mm = _merge_heads(m[rows], lo)
            l = _merge_heads(o2[rows, LANES:], lo)
            sink = sink_ref[sl]
            m_new = jnp.maximum(mm, sink)
            scale = jnp.exp2(mm - m_new)
            den = l * scale + jnp.exp2(sink - m_new)
            o_ref[q_rows, sl * LANES:(sl + 1) * LANES] = (o * (scale / den)).astype(o_ref.dtype)


def _attn_b_call(qb, kb, vb, tab, sink, batch, seq, to_cast):
    n_slab, t, _ = qb.shape
    n_kv = kb.shape[0]
    group = n_slab // n_kv
    kv_spec = pl.BlockSpec((None, seq, LANES), lambda kv, b: (kv, b, 0))
    cast_in_specs, cast_out_specs, cast_shapes = _cast_specs(to_cast, n_kv * batch, lambda kv, b: kv * batch + b)
    outs = pl.pallas_call(
        functools.partial(_attn_b_kernel, n_cast=len(to_cast)),
        grid=(n_kv, batch),
        in_specs=[pl.BlockSpec((group, seq, LANES), lambda kv, b: (kv, b, 0)), kv_spec, kv_spec,
                  pl.BlockSpec((group,) + tab.shape[1:], lambda kv, b: (kv, 0, 0, 0)),
                  pl.BlockSpec((group,) + sink.shape[1:], lambda kv, b: (kv, 0, 0))] + cast_in_specs,
        out_specs=[pl.BlockSpec((seq, group * LANES), lambda kv, b: (b, kv))] + cast_out_specs,
        out_shape=[jax.ShapeDtypeStruct((t, n_slab * LANES), BF16)] + cast_shapes,
        compiler_params=pltpu.CompilerParams(dimension_semantics=("arbitrary", "arbitrary"),
                                             vmem_limit_bytes=VMEM_LIMIT),
        name="attn_b",
    )(qb, kb, vb, tab, sink, *[w for w, _ in to_cast])
    return outs[0], outs[1:]


def _post_call(x2, ya, yb, p2, resident, layer, d_ff, to_cast):
    t, d = x2.shape
    n_steps = t // TM
    row_spec = lambda a: pl.BlockSpec((TM, a.shape[1]), lambda i: (i, 0))
    cast_in_specs, cast_out_specs, cast_shapes = _cast_specs(to_cast, n_steps)
    outs = pl.pallas_call(
        functools.partial(_post_kernel, n_cast=len(to_cast)),
        grid=(n_steps,),
        in_specs=[row_spec(x2), row_spec(ya), row_spec(yb),
                  pl.BlockSpec((None, TM, p2.shape[2]), lambda i: (layer, i, 0))]
        + [_layer_spec(a, layer) if a.ndim == 3 else _const_spec(a.shape) for a in resident] + cast_in_specs,
        out_specs=[pl.BlockSpec((TM, d), lambda i: (i, 0))] + cast_out_specs,
        out_shape=[jax.ShapeDtypeStruct((t, d), F32)] + cast_shapes,
        scratch_shapes=[pltpu.VMEM((TM, d), BF16), pltpu.VMEM((TM, d_ff), BF16), pltpu.VMEM((TM, d), F32)],
        compiler_params=pltpu.CompilerParams(dimension_semantics=("arbitrary",),
                                             vmem_limit_bytes=VMEM_LIMIT),
        name="post",
    )(x2, ya, yb, p2, *resident, *[w for w, _ in to_cast])
    return outs[0], outs[1:]


def _post_kernel(x_ref, ya_ref, yb_ref, p_ref, gmix_ref, gffn_ref, gple_ref, win_ref, wa_ref, wb_ref, wo_ref,
                 wgate_ref, wup_ref, wdown_ref, wpg_ref, wpp_ref, *refs, n_cast):
    o_ref, (merged, hid, xs) = _cast_side_job(refs, n_cast)
    d = x_ref.shape[1]
    chunks = [slice(c * MXU_COLS, (c + 1) * MXU_COLS) for c in range(d // MXU_COLS)]

    h = _rms(x_ref[...], gmix_ref[...]).astype(BF16)
    ya = ya_ref[...]
    yb = yb_ref[...]
    for c, cs in enumerate(chunks):
        ga = _dot(h, win_ref[:, GATE_COL + c * MXU_COLS:GATE_COL + (c + 1) * MXU_COLS])
        gb = _dot(h, win_ref[:, GATE_COL + d + c * MXU_COLS:GATE_COL + d + (c + 1) * MXU_COLS])
        pa = _dot(ya, wa_ref[:, cs])
        pb = _dot(yb, wb_ref[:, cs])
        merged[:, cs] = (jax.nn.sigmoid(ga) * pa + jax.nn.sigmoid(gb) * pb).astype(BF16)
    mg = merged[...]
    for cs in chunks:
        xs[:, cs] = x_ref[:, cs] + _dot(mg, wo_ref[:, cs])

    h = _rms(xs[...], gffn_ref[...]).astype(BF16)
    for c in range(wgate_ref.shape[1] // MXU_COLS):
        cs = slice(c * MXU_COLS, (c + 1) * MXU_COLS)
        a = _dot(h, wgate_ref[:, cs])
        u = _dot(h, wup_ref[:, cs])
        hid[:, cs] = (a * jax.nn.sigmoid(a) * u).astype(BF16)
    hv = hid[...]
    for cs in chunks:
        xs[:, cs] = xs[:, cs] + _dot(hv, wdown_ref[:, cs])

    h = _rms(xs[...], gple_ref[...]).astype(BF16)
    pv = p_ref[...].astype(BF16)
    for cs in chunks:
        o_ref[:, cs] = xs[:, cs] + jax.nn.sigmoid(_dot(h, wpg_ref[:, cs])) * _dot(pv, wpp_ref[:, cs])


def _t5_bucket(rel):
    half_b = NUM_BUCKETS // 2
    max_exact = half_b // 2
    sign = jnp.where(rel > 0, half_b, 0)
    n = jnp.abs(rel)
    nf = jnp.maximum(n, 1).astype(F32)
    large = max_exact + (jnp.log(nf / max_exact) / math.log(MAX_DISTANCE / max_exact)
                         * (half_b - max_exact)).astype(jnp.int32)
    large = jnp.minimum(large, half_b - 1)
    return sign + jnp.where(n < max_exact, n, large)


TABLE_PERIOD = 512


def _block_starts(seq_len, win):
    nb = seq_len // BQ
    blocks = (0, min(1, nb - 1), nb - 1) if nb > 1 else (0,)
    return [max(0, min(i * BQ - (win - BQ) // 2, seq_len - win)) - i * BQ for i in blocks]


def _rel_rows(table, starts, win, half, dilation):
    assert BQ + win - 1 <= TABLE_PERIOD
    c = jnp.arange(TABLE_PERIOD, dtype=jnp.int32)
    d = jnp.where(c < win, c, c - TABLE_PERIOD)
    rel = jnp.asarray(starts, jnp.int32)[:, None] + d[None, :]
    bias = table[_t5_bucket(rel * dilation)].astype(F32) * LOG2_E
    return jnp.transpose(jnp.where((jnp.abs(rel) <= half)[:, :, None], bias, NEG_INF), (0, 2, 1))


def _tables_kernel(rows_ref, *out_refs):
    v = 0
    for out_ref in out_refs:
        n_pair, n_var, _, win = out_ref.shape
        for t in range(n_var):
            for h in range(2 * n_pair):
                row = jnp.broadcast_to(rows_ref[v + t, h:h + 1, :], (BQ, TABLE_PERIOD))
                skew = pltpu.roll(row, 0, 1, stride=1, stride_axis=0)
                out_ref[h // 2, t, (h % 2) * BQ:(h % 2 + 1) * BQ, :] = skew[:, :win]
        v += n_var


def _band_tables(rows_and_wins, n_heads):
    rows = jnp.concatenate([r for r, _ in rows_and_wins], axis=0)
    shapes = [jax.ShapeDtypeStruct((n_heads // 2, r.shape[0], 2 * BQ, win), F32) for r, win in rows_and_wins]
    return pl.pallas_call(_tables_kernel, out_shape=shapes, name="tables",
                          compiler_params=pltpu.CompilerParams(vmem_limit_bytes=VMEM_LIMIT))(rows)


def _pair_gain(g, scale=1.0):
    return jnp.tile(g.astype(F32) * scale, 2)


def kernel(x, p, rel_table, norm_mix_g, w_in, qnorm_a_g, knorm_a_g, qnorm_b_g, knorm_b_g, sink_b,
           w_branch_a, w_branch_b, w_out, norm_ffn_g, w_ffn_gate, w_ffn_up, w_ffn_down,
           norm_ple_g, w_ple_gate, w_ple_proj):
    batch, seq, d = x.shape
    depth = p.shape[0]
    t = batch * seq
    q_scale = HEAD_DIM ** -0.5 * LOG2_E

    table_a = rel_table[:, :N_HEADS_A]
    table_b = rel_table[:, N_HEADS_A:]
    patterns = [(seq // dil, min(2 * BQ, seq // dil), w // (2 * dil), dil, table_a) for w, dil in DILATED_PATTERNS]
    patterns.append((seq, BQ + 2 * WINDOW_B, WINDOW_B, 1, table_b))
    rows = [(_rel_rows(tab, _block_starts(sub, win), win, half, dil), win) for sub, win, half, dil, tab in patterns]
    *tabs_a, tab_b = _band_tables(rows, N_HEADS_A)

    rows = lambda g: g.astype(F32).reshape(depth, 1, -1)
    g_mix, g_ffn, g_ple = rows(norm_mix_g), rows(norm_ffn_g), rows(norm_ple_g)
    p2 = p.reshape(depth, t, -1)
    post_weights = (w_branch_a, w_branch_b, w_out, w_ffn_gate, w_ffn_up, w_ffn_down, w_ple_gate, w_ple_proj)

    x2 = x.reshape(t, d)
    w_in_b, post_b = None, []
    for l in range(depth):
        gains = jnp.stack([_pair_gain(qnorm_a_g[l], q_scale), _pair_gain(knorm_a_g[l]),
                           _pair_gain(qnorm_b_g[l], q_scale), _pair_gain(knorm_b_g[l])])
        qa, ka, va, qb, kb, vb = _qkv_call(x2, g_mix, w_in if l == 0 else w_in_b, gains, l)
        ya = _attn_a_call(qa, ka, va, *tabs_a, batch, seq)
        sink = jnp.repeat(sink_b[l].astype(F32) * LOG2_E, HEAD_DIM).reshape(N_HEADS_B // 2, 1, LANES)
        yb, cast = _attn_b_call(qb, kb, vb, tab_b, sink, batch, seq,
                                [(w, 0) for w in (w_in,) + post_weights] if l == 0 else [])
        if l == 0:
            w_in_b, post_b = cast[0], cast[1:]
        to_cast = [(w, l + 1) for w in (w_in,) + post_weights] if l + 1 < depth else []
        x2, cast = _post_call(x2, ya, yb, p2, [g_mix, g_ffn, g_ple, w_in_b, *post_b], l, w_ffn_gate.shape[2],
                              to_cast)
        if to_cast:
            w_in_b, post_b = cast[0], cast[1:]
    return x2.reshape(batch, seq, d)
```

```python
import functools
import math

import jax
import jax.numpy as jnp
from jax import lax
from jax.experimental import pallas as pl
from jax.experimental.pallas import tpu as pltpu

F32 = jnp.float32
BF16 = jnp.bfloat16

HEAD_DIM = 64
N_HEADS_A = 8
N_HEADS_B = 8
N_KV_B = 2
DILATED_PATTERNS = ((128, 1), (512, 4), (2048, 16))
WINDOW_B = 128
NUM_BUCKETS = 32
MAX_DISTANCE = 1024
RMS_EPS = 1e-6
NEG_INF = -1e30
LOG2_E = math.log2(math.e)

WIDTH_A = N_HEADS_A * HEAD_DIM
WIDTH_BQ = N_HEADS_B * HEAD_DIM
WIDTH_BKV = N_KV_B * HEAD_DIM
GATE_COL = 3 * WIDTH_A + WIDTH_BQ + 2 * WIDTH_BKV

LANES = 128
MXU_COLS = 256
BF16_SUBLANES = 16
BQ = 128
TM = 512
VMEM_LIMIT = 56 * 1024 * 1024


def _dot(a, b):
    return jnp.dot(a, b, preferred_element_type=F32)


def _rms(x, g):
    ms = jnp.mean(x * x, axis=-1, keepdims=True)
    return x * lax.rsqrt(ms + RMS_EPS) * g


def _low_lanes():
    return lax.broadcasted_iota(jnp.int32, (1, LANES), 1) < HEAD_DIM


def _head_rms(y, gain, lo):
    sq = y * y
    s0 = jnp.sum(jnp.where(lo, sq, 0.0), axis=-1, keepdims=True)
    s1 = jnp.sum(jnp.where(lo, 0.0, sq), axis=-1, keepdims=True)
    ms = jnp.where(lo, s0, s1) * (1.0 / HEAD_DIM)
    return y * lax.rsqrt(ms + RMS_EPS) * gain


def _const_spec(shape):
    nd = len(shape)
    return pl.BlockSpec(shape, lambda *_: (0,) * nd, pipeline_mode=pl.Buffered(1))


def _layer_spec(stacked, layer):
    nd = stacked.ndim - 1
    return pl.BlockSpec((None,) + stacked.shape[1:], lambda *_: (layer,) + (0,) * nd,
                        pipeline_mode=pl.Buffered(1))


def _qkv_kernel(x_ref, g_ref, w_ref, gains_ref, qa_ref, ka_ref, va_ref, qb_ref, kb_ref, vb_ref):
    lo = _low_lanes()
    h = _rms(x_ref[...], g_ref[...]).astype(BF16)
    n_pair = WIDTH_A // MXU_COLS
    order = sorted(range(GATE_COL // MXU_COLS), key=lambda c: 2 * n_pair <= c < 3 * n_pair)
    for c in order:
        y = _dot(h, w_ref[:, c * MXU_COLS:(c + 1) * MXU_COLS].astype(BF16))
        y0, y1 = y[:, :LANES], y[:, LANES:]
        if c < n_pair:
            qa_ref[2 * c] = _head_rms(y0, gains_ref[0:1, :], lo)
            qa_ref[2 * c + 1] = _head_rms(y1, gains_ref[0:1, :], lo)
        elif c < 2 * n_pair:
            ka_ref[2 * (c - n_pair)] = _head_rms(y0, gains_ref[1:2, :], lo)
            ka_ref[2 * (c - n_pair) + 1] = _head_rms(y1, gains_ref[1:2, :], lo)
        elif c < 3 * n_pair:
            va_ref[2 * (c - 2 * n_pair)] = y0
            va_ref[2 * (c - 2 * n_pair) + 1] = y1
        elif c < 4 * n_pair:
            qb_ref[2 * (c - 3 * n_pair)] = _head_rms(y0, gains_ref[2:3, :], lo).astype(BF16)
            qb_ref[2 * (c - 3 * n_pair) + 1] = _head_rms(y1, gains_ref[2:3, :], lo).astype(BF16)
        else:
            kn = _head_rms(y0, gains_ref[3:4, :], lo)
            ks = pltpu.roll(kn, HEAD_DIM, axis=1)
            vs = pltpu.roll(y1, HEAD_DIM, axis=1)
            kb_ref[0] = jnp.where(lo, kn, ks).astype(BF16)
            kb_ref[1] = jnp.where(lo, ks, kn).astype(BF16)
            vb_ref[0] = jnp.where(lo, y1, vs).astype(BF16)
            vb_ref[1] = jnp.where(lo, vs, y1).astype(BF16)


def _cast_blocks(n_rows, n_steps):
    for n_blocks in range(n_steps, 0, -1):
        if n_steps % n_blocks == 0 and n_rows % (n_blocks * BF16_SUBLANES) == 0:
            return n_rows // n_blocks, n_steps // n_blocks
    raise ValueError(f"cannot split {n_rows} rows into bf16 row blocks over {n_steps} steps")


def _cast_specs(to_cast, n_steps, step_of=lambda i: i):
    in_specs, out_specs, shapes = [], [], []
    for w, w_layer in to_cast:
        _, n_rows, n_cols = w.shape
        rb, hold = _cast_blocks(n_rows, n_steps)
        in_specs.append(pl.BlockSpec((None, rb, n_cols),
                                     lambda *idx, w_layer=w_layer, hold=hold: (w_layer, step_of(*idx) // hold, 0)))
        out_specs.append(pl.BlockSpec((rb, n_cols), lambda *idx, hold=hold: (step_of(*idx) // hold, 0)))
        shapes.append(jax.ShapeDtypeStruct((n_rows, n_cols), BF16))
    return in_specs, out_specs, shapes


def _cast_side_job(refs, n_cast):
    for src, dst in zip(refs[:n_cast], refs[n_cast + 1:2 * n_cast + 1]):
        dst[...] = src[...].astype(BF16)
    return refs[n_cast], refs[2 * n_cast + 1:]


def _qkv_call(x2, g, w_in_layer, gains, layer):
    t, d = x2.shape
    n_q = WIDTH_A // LANES
    n_kv = N_KV_B
    assert WIDTH_BQ == WIDTH_A and 2 * WIDTH_BKV == MXU_COLS
    slab = lambda n, dt: jax.ShapeDtypeStruct((n, t, LANES), dt)
    slab_spec = lambda n: pl.BlockSpec((n, TM, LANES), lambda i: (0, i, 0))
    return pl.pallas_call(
        _qkv_kernel,
        grid=(t // TM,),
        in_specs=[pl.BlockSpec((TM, d), lambda i: (i, 0)), _layer_spec(g, layer),
                  _layer_spec(w_in_layer, layer) if w_in_layer.ndim == 3 else _const_spec(w_in_layer.shape),
                  _const_spec(gains.shape)],
        out_specs=[slab_spec(n_q), slab_spec(n_q), slab_spec(n_q), slab_spec(n_q),
                   slab_spec(n_kv), slab_spec(n_kv)],
        out_shape=[slab(n_q, F32), slab(n_q, F32), slab(n_q, F32), slab(n_q, BF16),
                   slab(n_kv, BF16), slab(n_kv, BF16)],
        compiler_params=pltpu.CompilerParams(dimension_semantics=("arbitrary",),
                                             vmem_limit_bytes=VMEM_LIMIT),
        name="qkv",
    )(x2, g, w_in_layer, gains)


def _attn_block(q, kw, vw, tab, lo):
    zero = jnp.zeros_like(q)
    q2 = jnp.concatenate([jnp.where(lo, q, zero), jnp.where(lo, zero, q)], axis=0).astype(BF16)
    s = lax.dot_general(q2, kw.astype(BF16), (((1,), (1,)), ((), ())),
                        preferred_element_type=F32) + tab
    m = jnp.max(s, axis=-1, keepdims=True)
    p = jnp.exp2(s - m)
    l = jnp.sum(p, axis=-1, keepdims=True)
    o2 = _dot(p.astype(BF16), vw.astype(BF16))
    return o2, m, l


def _merge_heads(a2, lo):
    bq = a2.shape[0] // 2
    return jnp.where(lo, a2[:bq], a2[bq:])


def _block_geometry(i, n_blocks, seq_len, win):
    w0 = max(0, min(i * BQ - (win - BQ) // 2, seq_len - win))
    return pl.ds(i * BQ, BQ), pl.ds(w0, win), (0 if i == 0 else (2 if i == n_blocks - 1 else 1))


def _attn_a_kernel(q_ref, k_ref, v_ref, t1_ref, t4_ref, t16_ref, out_ref,
                   o1, lse1, outf, qs, ks, vs, o4, lse4, o16g, lse16g, o16, lse16, outc):
    lo = _low_lanes()
    seq = q_ref.shape[0]
    win = t1_ref.shape[-1]
    d4 = DILATED_PATTERNS[1][1]
    step = DILATED_PATTERNS[2][1] // d4
    sub4 = seq // d4

    def block(qr, kr, vr, q_rows, kv_rows, tab, o_out, lse_out, out_rows):
        o2, m, l = _attn_block(qr[q_rows, :], kr[kv_rows, :], vr[kv_rows, :], tab, lo)
        l = _merge_heads(l, lo)
        o_out[out_rows, :] = _merge_heads(o2, lo) / l
        lse_out[out_rows, :] = _merge_heads(m, lo) + jnp.log2(l)

    for r in range(d4):
        res = pl.ds(r, sub4, stride=d4)
        qs[r] = q_ref[res, :]
        ks[r] = k_ref[res, :]
        vs[r] = v_ref[res, :]

    nb = seq // BQ
    for i in range(nb):
        q_rows, kv_rows, tidx = _block_geometry(i, nb, seq, win)
        block(q_ref, k_ref, v_ref, q_rows, kv_rows, t1_ref[tidx], o1, lse1, q_rows)

    nb4 = sub4 // BQ
    for i in range(nb4):
        for r in range(d4):
            q_rows, kv_rows, tidx = _block_geometry(i, nb4, sub4, win)
            block(qs.at[r], ks.at[r], vs.at[r], q_rows, kv_rows, t4_ref[tidx], o4.at[r], lse4.at[r], q_rows)

    for b in range(step):
        for r in range(d4):
            rows = pl.ds(b, BQ, stride=step)
            block(qs.at[r], ks.at[r], vs.at[r], rows, rows, t16_ref[0], o16g.at[r], lse16g.at[r],
                  pl.ds(b * BQ, BQ))

    def merge_class(r, carry):
        for grouped, ordered in ((o16g, o16), (lse16g, lse16)):
            for b in range(step):
                ordered[r, pl.ds(b, BQ, stride=step), :] = grouped[r, pl.ds(b * BQ, BQ), :]
        for i in range(nb4):
            rows = pl.ds(i * BQ, BQ)
            nat = pl.ds(r + d4 * i * BQ, BQ, stride=d4)
            la, lb, lc = lse1[nat, :], lse4[r, rows, :], lse16[r, rows, :]
            mx = jnp.maximum(jnp.maximum(la, lb), lc)
            wa, wb, wc = jnp.exp2(la - mx), jnp.exp2(lb - mx), jnp.exp2(lc - mx)
            num = wa * o1[nat, :] + wb * o4[r, rows, :] + wc * o16[r, rows, :]
            outc[r, rows, :] = num / (wa + wb + wc)
        outf[pl.ds(r, sub4, stride=d4), :] = outc[r]
        return carry

    for r in range(d4):
        merge_class(r, 0)
    out_ref[...] = outf[...].astype(out_ref.dtype)


def _attn_a_call(qa, ka, va, t1, t4, t16, batch, seq):
    n_slab, t, _ = qa.shape
    assert seq // DILATED_PATTERNS[2][1] == BQ
    qkv_spec = pl.BlockSpec((None, seq, LANES), lambda hp, b: (hp, b, 0))
    tab_spec = lambda tab: pl.BlockSpec((None,) + tab.shape[1:], lambda hp, b: (hp, 0, 0, 0))
    d4 = DILATED_PATTERNS[1][1]
    full = pltpu.VMEM((seq, LANES), F32)
    by_class = pltpu.VMEM((d4, seq // d4, LANES), F32)
    return pl.pallas_call(
        _attn_a_kernel,
        grid=(n_slab, batch),
        in_specs=[qkv_spec, qkv_spec, qkv_spec, tab_spec(t1), tab_spec(t4), tab_spec(t16)],
        out_specs=pl.BlockSpec((seq, LANES), lambda hp, b: (b, hp)),
        out_shape=jax.ShapeDtypeStruct((t, n_slab * LANES), BF16),
        scratch_shapes=[full] * 3 + [by_class] * 10,
        compiler_params=pltpu.CompilerParams(dimension_semantics=("arbitrary", "arbitrary"),
                                             vmem_limit_bytes=VMEM_LIMIT),
        name="attn_a",
    )(qa, ka, va, t1, t4, t16)


def _attn_b_kernel(q_ref, k_ref, v_ref, tab_ref, sink_ref, *refs, n_cast):
    o_ref, _ = _cast_side_job(refs, n_cast)
    lo = _low_lanes()
    n_slab, seq, _ = q_ref.shape
    win = tab_ref.shape[-1]
    nb = seq // BQ
    zero = jnp.zeros((BQ, LANES), q_ref.dtype)

    for i in range(nb):
        q_rows, kv_rows, tidx = _block_geometry(i, nb, seq, win)
        q = [q_ref[sl, q_rows, :] for sl in range(n_slab)]
        q_all = jnp.concatenate([part for qs in q for part in (jnp.where(lo, qs, zero), jnp.where(lo, zero, qs))],
                                axis=0)
        tab = jnp.concatenate([tab_ref[sl, tidx] for sl in range(n_slab)], axis=0)
        s = lax.dot_general(q_all, k_ref[kv_rows, :], (((1,), (1,)), ((), ())), preferred_element_type=F32) + tab
        m = jnp.max(s, axis=-1, keepdims=True)
        p = jnp.exp2(s - m)
        vw = v_ref[kv_rows, :]
        o2 = _dot(p.astype(BF16), jnp.concatenate([vw, jnp.ones(vw.shape, BF16)], axis=1))
        for sl in range(n_slab):
            rows = slice(sl * 2 * BQ, (sl + 1) * 2 * BQ)
            o = _merge_heads(o2[rows, :LANES], lo)
            mm = _merge_heads(m[rows], lo)
            l = _merge_heads(o2[rows, LANES:], lo)
            sink = sink_ref[sl]
            m_new = jnp.maximum(mm, sink)
            scale = jnp.exp2(mm - m_new)
            den = l * scale + jnp.exp2(sink - m_new)
            o_ref[q_rows, sl * LANES:(sl + 1) * LANES] = (o * (scale / den)).astype(o_ref.dtype)


def _attn_b_call(qb, kb, vb, tab, sink, batch, seq, to_cast):
    n_slab, t, _ = qb.shape
    n_kv = kb.shape[0]
    group = n_slab // n_kv
    kv_spec = pl.BlockSpec((None, seq, LANES), lambda kv, b: (kv, b, 0))
    cast_in_specs, cast_out_specs, cast_shapes = _cast_specs(to_cast, n_kv * batch, lambda kv, b: kv * batch + b)
    outs = pl.pallas_call(
        functools.partial(_attn_b_kernel, n_cast=len(to_cast)),
        grid=(n_kv, batch),
        in_specs=[pl.BlockSpec((group, seq, LANES), lambda kv, b: (kv, b, 0)), kv_spec, kv_spec,
                  pl.BlockSpec((group,) + tab.shape[1:], lambda kv, b: (kv, 0, 0, 0)),
                  pl.BlockSpec((group,) + sink.shape[1:], lambda kv, b: (kv, 0, 0))] + cast_in_specs,
        out_specs=[pl.BlockSpec((seq, group * LANES), lambda kv, b: (b, kv))] + cast_out_specs,
        out_shape=[jax.ShapeDtypeStruct((t, n_slab * LANES), BF16)] + cast_shapes,
        compiler_params=pltpu.CompilerParams(dimension_semantics=("arbitrary", "arbitrary"),
                                             vmem_limit_bytes=VMEM_LIMIT),
        name="attn_b",
    )(qb, kb, vb, tab, sink, *[w for w, _ in to_cast])
    return outs[0], outs[1:]


def _post_call(x2, ya, yb, p2, resident, layer, d_ff, to_cast):
    t, d = x2.shape
    n_steps = t // TM
    row_spec = lambda a: pl.BlockSpec((TM, a.shape[1]), lambda i: (i, 0))
    cast_in_specs, cast_out_specs, cast_shapes = _cast_specs(to_cast, n_steps)
    outs = pl.pallas_call(
        functools.partial(_post_kernel, n_cast=len(to_cast)),
        grid=(n_steps,),
        in_specs=[row_spec(x2), row_spec(ya), row_spec(yb),
                  pl.BlockSpec((None, TM, p2.shape[2]), lambda i: (layer, i, 0))]
        + [_layer_spec(a, layer) if a.ndim == 3 else _const_spec(a.shape) for a in resident] + cast_in_specs,
        out_specs=[pl.BlockSpec((TM, d), lambda i: (i, 0))] + cast_out_specs,
        out_shape=[jax.ShapeDtypeStruct((t, d), F32)] + cast_shapes,
        scratch_shapes=[pltpu.VMEM((TM, d), BF16), pltpu.VMEM((TM, d_ff), BF16), pltpu.VMEM((TM, d), F32)],
        compiler_params=pltpu.CompilerParams(dimension_semantics=("arbitrary",),
                                             vmem_limit_bytes=VMEM_LIMIT),
        name="post",
    )(x2, ya, yb, p2, *resident, *[w for w, _ in to_cast])
    return outs[0], outs[1:]


def _post_kernel(x_ref, ya_ref, yb_ref, p_ref, gmix_ref, gffn_ref, gple_ref, win_ref, wa_ref, wb_ref, wo_ref,
                 wgate_ref, wup_ref, wdown_ref, wpg_ref, wpp_ref, *refs, n_cast):
    o_ref, (merged, hid, xs) = _cast_side_job(refs, n_cast)
    d = x_ref.shape[1]
    chunks = [slice(c * MXU_COLS, (c + 1) * MXU_COLS) for c in range(d // MXU_COLS)]

    h = _rms(x_ref[...], gmix_ref[...]).astype(BF16)
    ya = ya_ref[...]
    yb = yb_ref[...]
    for c, cs in enumerate(chunks):
        ga = _dot(h, win_ref[:, GATE_COL + c * MXU_COLS:GATE_COL + (c + 1) * MXU_COLS])
        gb = _dot(h, win_ref[:, GATE_COL + d + c * MXU_COLS:GATE_COL + d + (c + 1) * MXU_COLS])
        pa = _dot(ya, wa_ref[:, cs])
        pb = _dot(yb, wb_ref[:, cs])
        merged[:, cs] = (jax.nn.sigmoid(ga) * pa + jax.nn.sigmoid(gb) * pb).astype(BF16)
    mg = merged[...]
    for cs in chunks:
        xs[:, cs] = x_ref[:, cs] + _dot(mg, wo_ref[:, cs])

    h = _rms(xs[...], gffn_ref[...]).astype(BF16)
    for c in range(wgate_ref.shape[1] // MXU_COLS):
        cs = slice(c * MXU_COLS, (c + 1) * MXU_COLS)
        a = _dot(h, wgate_ref[:, cs])
        u = _dot(h, wup_ref[:, cs])
        hid[:, cs] = (a * jax.nn.sigmoid(a) * u).astype(BF16)
    hv = hid[...]
    for cs in chunks:
        xs[:, cs] = xs[:, cs] + _dot(hv, wdown_ref[:, cs])

    h = _rms(xs[...], gple_ref[...]).astype(BF16)
    pv = p_ref[...].astype(BF16)
    for cs in chunks:
        o_ref[:, cs] = xs[:, cs] + jax.nn.sigmoid(_dot(h, wpg_ref[:, cs])) * _dot(pv, wpp_ref[:, cs])


def _t5_bucket(rel):
    half_b = NUM_BUCKETS // 2
    max_exact = half_b // 2
    sign = jnp.where(rel > 0, half_b, 0)
    n = jnp.abs(rel)
    nf = jnp.maximum(n, 1).astype(F32)
    large = max_exact + (jnp.log(nf / max_exact) / math.log(MAX_DISTANCE / max_exact)
                         * (half_b - max_exact)).astype(jnp.int32)
    large = jnp.minimum(large, half_b - 1)
    return sign + jnp.where(n < max_exact, n, large)


TABLE_PERIOD = 512


def _block_starts(seq_len, win):
    nb = seq_len // BQ
    blocks = (0, min(1, nb - 1), nb - 1) if nb > 1 else (0,)
    return [max(0, min(i * BQ - (win - BQ) // 2, seq_len - win)) - i * BQ for i in blocks]


def _rel_rows(table, starts, win, half, dilation):
    assert BQ + win - 1 <= TABLE_PERIOD
    c = jnp.arange(TABLE_PERIOD, dtype=jnp.int32)
    d = jnp.where(c < win, c, c - TABLE_PERIOD)
    rel = jnp.asarray(starts, jnp.int32)[:, None] + d[None, :]
    bias = table[_t5_bucket(rel * dilation)].astype(F32) * LOG2_E
    return jnp.transpose(jnp.where((jnp.abs(rel) <= half)[:, :, None], bias, NEG_INF), (0, 2, 1))


def _tables_kernel(rows_ref, *out_refs):
    v = 0
    for out_ref in out_refs:
        n_pair, n_var, _, win = out_ref.shape
        for t in range(n_var):
            for h in range(2 * n_pair):
                row = jnp.broadcast_to(rows_ref[v + t, h:h + 1, :], (BQ, TABLE_PERIOD))
                skew = pltpu.roll(row, 0, 1, stride=1, stride_axis=0)
                out_ref[h // 2, t, (h % 2) * BQ:(h % 2 + 1) * BQ, :] = skew[:, :win]
        v += n_var


def _band_tables(rows_and_wins, n_heads):
    rows = jnp.concatenate([r for r, _ in rows_and_wins], axis=0)
    shapes = [jax.ShapeDtypeStruct((n_heads // 2, r.shape[0], 2 * BQ, win), F32) for r, win in rows_and_wins]
    return pl.pallas_call(_tables_kernel, out_shape=shapes, name="tables",
                          compiler_params=pltpu.CompilerParams(vmem_limit_bytes=VMEM_LIMIT))(rows)


def _pair_gain(g, scale=1.0):
    return jnp.tile(g.astype(F32) * scale, 2)


def kernel(x, p, rel_table, norm_mix_g, w_in, qnorm_a_g, knorm_a_g, qnorm_b_g, knorm_b_g, sink_b,
           w_branch_a, w_branch_b, w_out, norm_ffn_g, w_ffn_gate, w_ffn_up, w_ffn_down,
           norm_ple_g, w_ple_gate, w_ple_proj):
    batch, seq, d = x.shape
    depth = p.shape[0]
    t = batch * seq
    q_scale = HEAD_DIM ** -0.5 * LOG2_E

    table_a = rel_table[:, :N_HEADS_A]
    table_b = rel_table[:, N_HEADS_A:]
    patterns = [(seq // dil, min(2 * BQ, seq // dil), w // (2 * dil), dil, table_a) for w, dil in DILATED_PATTERNS]
    patterns.append((seq, BQ + 2 * WINDOW_B, WINDOW_B, 1, table_b))
    rows = [(_rel_rows(tab, _block_starts(sub, win), win, half, dil), win) for sub, win, half, dil, tab in patterns]
    *tabs_a, tab_b = _band_tables(rows, N_HEADS_A)

    rows = lambda g: g.astype(F32).reshape(depth, 1, -1)
    g_mix, g_ffn, g_ple = rows(norm_mix_g), rows(norm_ffn_g), rows(norm_ple_g)
    p2 = p.reshape(depth, t, -1)
    post_weights = (w_branch_a, w_branch_b, w_out, w_ffn_gate, w_ffn_up, w_ffn_down, w_ple_gate, w_ple_proj)

    x2 = x.reshape(t, d)
    w_in_b, post_b = None, []
    for l in range(depth):
        gains = jnp.stack([_pair_gain(qnorm_a_g[l], q_scale), _pair_gain(knorm_a_g[l]),
                           _pair_gain(qnorm_b_g[l], q_scale), _pair_gain(knorm_b_g[l])])
        qa, ka, va, qb, kb, vb = _qkv_call(x2, g_mix, w_in if l == 0 else w_in_b, gains, l)
        ya = _attn_a_call(qa, ka, va, *tabs_a, batch, seq)
        sink = jnp.repeat(sink_b[l].astype(F32) * LOG2_E, HEAD_DIM).reshape(N_HEADS_B // 2, 1, LANES)
        yb, cast = _attn_b_call(qb, kb, vb, tab_b, sink, batch, seq,
                                [(w, 0) for w in (w_in,) + post_weights] if l == 0 else [])
        if l == 0:
            w_in_b, post_b = cast[0], cast[1:]
        to_cast = [(w, l + 1) for w in (w_in,) + post_weights] if l + 1 < depth else []
        x2, cast = _post_call(x2, ya, yb, p2, [g_mix, g_ffn, g_ple, w_in_b, *post_b], l, w_ffn_gate.shape[2],
                              to_cast)
        if to_cast:
            w_in_b, post_b = cast[0], cast[1:]
    return x2.reshape(batch, seq, d)
```
